```python
import jax, jax.numpy as jnp
from jax import lax
import numpy as np

D_MODEL = 1024
BATCH = 8
SEQ = 2048
DEPTH = 2
DEC_BATCH = 128
DEC_SEQ = 4
PAST_LEN = 16384
PAGE_SIZE = 128

D_CONV = D_MODEL // 2
CONV_WIDTH = 31
CONV_BUF = CONV_WIDTH - 1
D_POOL = D_MODEL // 2
POOL_WINDOWS = (2, 4, 8, 16)
N_POOL_GROUPS = len(POOL_WINDOWS)
POOL_GROUP_IN = D_POOL // N_POOL_GROUPS
POOL_GROUP_OUT = D_MODEL // N_POOL_GROUPS
POOL_BUF = max(POOL_WINDOWS) - 1
IN_COLS = 2 * D_CONV + D_POOL + 2 * D_MODEL
N_EXPERTS = 32
TOP_K = 4
D_EXPERT = D_MODEL
SWIGLU_LIMIT = 7.0
SWIGLU_ALPHA = 1.702
MOE_BLOCK = 128
N_MOD = 6
EPS = 1e-6

kernel_name = "gated_conv_pool_moe_decoder_step"


def rmsnorm(x, g):
    xf = x.astype(jnp.float32)
    y = xf * lax.rsqrt(jnp.mean(xf * xf, axis=-1, keepdims=True) + EPS)
    return (y * g.astype(jnp.float32)).astype(x.dtype)


def layernorm(x, g, b):
    xf = x.astype(jnp.float32)
    mu = jnp.mean(xf, axis=-1, keepdims=True)
    var = jnp.mean(jnp.square(xf - mu), axis=-1, keepdims=True)
    y = (xf - mu) * lax.rsqrt(var + EPS)
    return (y * g.astype(jnp.float32) + b.astype(jnp.float32)).astype(x.dtype)


def causal_depthwise_conv(u, buf, w, b):
    full = jnp.concatenate([buf.astype(u.dtype), u], axis=1)
    y = lax.conv_general_dilated(
        full, w[:, None, :].astype(u.dtype), window_strides=(1,), padding="VALID",
        dimension_numbers=("NWC", "WIO", "NWC"), feature_group_count=u.shape[-1])
    return y + b, full[:, -CONV_BUF:, :]


def causal_multiscale_pool(u, buf, pos0):
    T = u.shape[1]
    full = jnp.concatenate([buf.astype(u.dtype), u], axis=1)
    fullf = full.astype(jnp.float32)
    cs = jnp.pad(jnp.cumsum(fullf, axis=1), ((0, 0), (1, 0), (0, 0)))
    pos = pos0 + jnp.arange(T, dtype=jnp.int32)
    outs = []
    for g, win in enumerate(POOL_WINDOWS):
        lo, hi = g * POOL_GROUP_IN, (g + 1) * POOL_GROUP_IN
        end = cs[:, POOL_BUF + 1:POOL_BUF + 1 + T, lo:hi]
        start = cs[:, POOL_BUF + 1 - win:POOL_BUF + 1 - win + T, lo:hi]
        cnt = jnp.minimum(pos + 1, win).astype(jnp.float32)[None, :, None]
        outs.append((end - start) / cnt - fullf[:, POOL_BUF:, lo:hi])
    pooled = jnp.stack(outs, axis=2).astype(u.dtype)
    return pooled, full[:, -POOL_BUF:, :]


def token_mixer(h, conv_buf, pool_buf, pos0, w_in, b_in, conv_w, conv_b, ln_g, ln_b,
                w_conv_out, w_pool, pool_scale, w_out):
    B, T, _ = h.shape
    z = h @ w_in + b_in
    u_val, u_gate, u_pool, g_conv, g_pool = jnp.split(
        z, [D_CONV, 2 * D_CONV, 2 * D_CONV + D_POOL, 2 * D_CONV + D_POOL + D_MODEL], axis=-1)
    u = u_val * jax.nn.sigmoid(u_gate)
    v, new_conv_buf = causal_depthwise_conv(u, conv_buf, conv_w, conv_b)
    v = jax.nn.silu(layernorm(v, ln_g, ln_b))
    y_conv = v @ w_conv_out
    pooled, new_pool_buf = causal_multiscale_pool(u_pool, pool_buf, pos0)
    y_pool = jnp.einsum("btgi,gio->btgo", pooled, w_pool).reshape(B, T, D_MODEL) * pool_scale
    m = jax.nn.sigmoid(g_conv) * y_conv + jax.nn.sigmoid(g_pool) * y_pool
    return m @ w_out, new_conv_buf, new_pool_buf


def routed_experts(h, w_router, b_router, w_up, b_up, w_down, b_down):
    B, T, D = h.shape
    n_tok = B * T
    x = h.reshape(n_tok, D)
    logits = (x @ w_router + b_router).astype(jnp.float32)
    top_logit, top_idx = lax.top_k(logits, TOP_K)
    top_w = jax.nn.softmax(top_logit, axis=-1)
    n_assign = n_tok * TOP_K
    n_blocks = n_assign // MOE_BLOCK + N_EXPERTS + 1
    n_slots = n_blocks * MOE_BLOCK
    flat_e = top_idx.reshape(n_assign)
    flat_tok = jnp.arange(n_assign, dtype=jnp.int32) // TOP_K
    flat_w = top_w.reshape(n_assign)
    order = jnp.argsort(flat_e)
    e_sorted = flat_e[order]
    counts = jnp.bincount(flat_e, length=N_EXPERTS)
    padded = (counts + MOE_BLOCK - 1) // MOE_BLOCK * MOE_BLOCK
    pad_end = jnp.cumsum(padded)
    pad_start = pad_end - padded
    raw_start = jnp.cumsum(counts) - counts
    dest = pad_start[e_sorted] + jnp.arange(n_assign, dtype=jnp.int32) - raw_start[e_sorted]
    slot_tok = jnp.full((n_slots,), n_tok, jnp.int32).at[dest].set(flat_tok[order])
    slot_w = jnp.zeros((n_slots,), jnp.float32).at[dest].set(flat_w[order])
    block_expert = jnp.minimum(
        jnp.searchsorted(pad_end, jnp.arange(n_blocks, dtype=jnp.int32) * MOE_BLOCK, side="right"),
        N_EXPERTS - 1)
    x_pad = jnp.concatenate([x, jnp.zeros((1, D), x.dtype)], axis=0)
    xb = x_pad[slot_tok].reshape(n_blocks, MOE_BLOCK, D)

    def expert_block(args):
        xk, e = args
        a = xk @ w_up[e] + b_up[e]
        a_glu = jnp.minimum(a[:, :D_EXPERT], SWIGLU_LIMIT)
        a_lin = jnp.clip(a[:, D_EXPERT:], -SWIGLU_LIMIT, SWIGLU_LIMIT)
        o = a_glu * jax.nn.sigmoid(SWIGLU_ALPHA * a_glu) * (a_lin + 1)
        return o @ w_down[e] + b_down[e]

    yb = lax.map(expert_block, (xb, block_expert))
    y = yb.reshape(n_slots, D).astype(jnp.float32) * slot_w[:, None]
    out = jax.ops.segment_sum(y, slot_tok, num_segments=n_tok + 1)[:n_tok]
    return out.astype(h.dtype).reshape(B, T, D)


def trunk(x, c, conv_bufs, pool_bufs, pos0, w_ada, b_ada, norm_mix, w_in, b_in, conv_w, conv_b,
          conv_ln_g, conv_ln_b, w_conv_out, w_pool, pool_scale, w_out, norm_ffn, w_router,
          b_router, w_up, b_up, w_down, b_down, norm_final):
    new_conv, new_pool = [], []
    for l in range(DEPTH):
        mod = (jax.nn.silu(c) @ w_ada[l] + b_ada[l])[:, None, :]
        sh1, sc1, gt1, sh2, sc2, gt2 = jnp.split(mod, N_MOD, axis=-1)
        h = rmsnorm(x, norm_mix[l]) * (1 + sc1) + sh1
        y, cb, pb = token_mixer(h, conv_bufs[l], pool_bufs[l], pos0, w_in[l], b_in[l], conv_w[l],
                                conv_b[l], conv_ln_g[l], conv_ln_b[l], w_conv_out[l], w_pool[l],
                                pool_scale[l], w_out[l])
        x = x + gt1 * y
        h = rmsnorm(x, norm_ffn[l]) * (1 + sc2) + sh2
        x = x + gt2 * routed_experts(h, w_router[l], b_router[l], w_up[l], b_up[l], w_down[l], b_down[l])
        new_conv.append(cb)
        new_pool.append(pb)
    return rmsnorm(x, norm_final), jnp.stack(new_conv), jnp.stack(new_pool)


def setup_inputs(seed: int = 0) -> dict:
    key = jax.random.key(seed)
    ks = jax.random.split(key, 32)
    f32 = jnp.float32
    nrm = lambda k, shape, s: jax.random.normal(k, shape, f32) * s
    D = D_MODEL
    return {
        "x_prompt": nrm(ks[0], (BATCH, SEQ, D), 1.0),
        "x_sample": nrm(ks[1], (DEC_BATCH, DEC_SEQ, D), 1.0),
        "state_conv": nrm(ks[2], (DEPTH, DEC_BATCH, CONV_BUF, D_CONV), 0.5),
        "state_pool": nrm(ks[3], (DEPTH, DEC_BATCH, POOL_BUF, D_POOL), 1.0),
        "c_prompt": nrm(ks[4], (BATCH, D), 1.0),
        "c_sample": nrm(ks[5], (DEC_BATCH, D), 1.0),
        "w_ada": nrm(ks[6], (DEPTH, D, N_MOD * D), 0.5 * D ** -0.5),
        "b_ada": nrm(ks[7], (DEPTH, N_MOD * D), 0.02),
        "norm_mix": 1.0 + nrm(ks[8], (DEPTH, D), 0.05),
        "w_in": nrm(ks[9], (DEPTH, D, IN_COLS), D ** -0.5),
        "b_in": nrm(ks[10], (DEPTH, IN_COLS), 0.02),
        "conv_w": nrm(ks[11], (DEPTH, CONV_WIDTH, D_CONV), CONV_WIDTH ** -0.5),
        "conv_b": nrm(ks[12], (DEPTH, D_CONV), 0.02),
        "conv_ln_g": 1.0 + nrm(ks[13], (DEPTH, D_CONV), 0.05),
        "conv_ln_b": nrm(ks[14], (DEPTH, D_CONV), 0.02),
        "w_conv_out": nrm(ks[15], (DEPTH, D_CONV, D), D_CONV ** -0.5),
        "w_pool": nrm(ks[16], (DEPTH, N_POOL_GROUPS, POOL_GROUP_IN, POOL_GROUP_OUT), POOL_GROUP_IN ** -0.5),
        "pool_scale": 1.0 + nrm(ks[17], (DEPTH, D), 0.1),
        "w_out": nrm(ks[18], (DEPTH, D, D), D ** -0.5),
        "norm_ffn": 1.0 + nrm(ks[19], (DEPTH, D), 0.05),
        "w_router": nrm(ks[20], (DEPTH, D, N_EXPERTS), D ** -0.5),
        "b_router": nrm(ks[21], (DEPTH, N_EXPERTS), 0.01),
        "w_up": nrm(ks[22], (DEPTH, N_EXPERTS, D, 2 * D_EXPERT), D ** -0.5),
        "b_up": nrm(ks[23], (DEPTH, N_EXPERTS, 2 * D_EXPERT), 0.02),
        "w_down": nrm(ks[24], (DEPTH, N_EXPERTS, D_EXPERT, D), D_EXPERT ** -0.5),
        "b_down": nrm(ks[25], (DEPTH, N_EXPERTS, D), 0.02),
        "norm_final": 1.0 + nrm(ks[26], (D,), 0.05),
    }


def reference(x_prompt, x_sample, state_conv, state_pool, c_prompt, c_sample, w_ada, b_ada,
              norm_mix, w_in, b_in, conv_w, conv_b, conv_ln_g, conv_ln_b, w_conv_out, w_pool,
              pool_scale, w_out, norm_ffn, w_router, b_router, w_up, b_up, w_down, b_down,
              norm_final):
    weights = (w_ada, b_ada, norm_mix, w_in, b_in, conv_w, conv_b, conv_ln_g, conv_ln_b,
               w_conv_out, w_pool, pool_scale, w_out, norm_ffn, w_router, b_router, w_up, b_up,
               w_down, b_down, norm_final)
    B = x_prompt.shape[0]
    zero_conv = jnp.zeros((DEPTH, B, CONV_BUF, D_CONV), x_prompt.dtype)
    zero_pool = jnp.zeros((DEPTH, B, POOL_BUF, D_POOL), x_prompt.dtype)
    y_prompt, new_conv_prompt, new_pool_prompt = trunk(
        x_prompt, c_prompt, zero_conv, zero_pool, 0, *weights)
    y_sample, new_conv_sample, new_pool_sample = trunk(
        x_sample, c_sample, state_conv, state_pool, PAST_LEN, *weights)
    return (y_prompt, y_sample, new_conv_prompt, new_pool_prompt, new_conv_sample, new_pool_sample)
```

```python
import functools

import jax
import jax.numpy as jnp
from jax.experimental import pallas as pl
from jax.experimental.pallas import tpu as pltpu

F32 = jnp.float32
BF16 = jnp.bfloat16

D_MODEL = 1024
DEPTH = 2
D_CONV = 512
CONV_WIDTH = 31
CONV_BUF = CONV_WIDTH - 1
D_POOL = 512
POOL_WINDOWS = (2, 4, 8, 16)
POOL_GROUP_IN = D_POOL // len(POOL_WINDOWS)
POOL_GROUP_OUT = D_MODEL // len(POOL_WINDOWS)
POOL_BUF = max(POOL_WINDOWS) - 1
IN_COLS = 2 * D_CONV + D_POOL + 2 * D_MODEL
N_EXPERTS = 32
TOP_K = 4
D_EXPERT = D_MODEL
SWIGLU_LIMIT = 7.0
SWIGLU_ALPHA = 1.702
N_MOD = 6
EPS = 1e-6
PAST_LEN = 16384

LANES = 128
SUBLANES = 8
VMEM_LIMIT = 52 * 1024 * 1024

TT = 256
CONV_HALO = 32
POOL_HALO = 16
CONV_ROWS = 64
SB = 32
BM = 256
ADA_NT = 1536


def _rms(x, g):
    return x * jax.lax.rsqrt(jnp.mean(x * x, axis=-1, keepdims=True) + EPS) * g


def _dot(a, b):
    return jnp.dot(a, b, preferred_element_type=F32)


def _adaln_kernel(c_ref, w_ref, b_ref, o_ref):
    c = c_ref[...]
    a = (c * jax.nn.sigmoid(c)).astype(BF16)
    o_ref[...] = _dot(a, w_ref[...].astype(BF16)) + b_ref[...]


def _adaln(c_all, w_ada, b_ada):
    n = c_all.shape[0]
    cols = N_MOD * D_MODEL
    return pl.pallas_call(
        _adaln_kernel,
        grid=(DEPTH, cols // ADA_NT),
        in_specs=[
            pl.BlockSpec((n, D_MODEL), lambda l, j: (0, 0)),
            pl.BlockSpec((None, D_MODEL, ADA_NT), lambda l, j: (l, 0, j)),
            pl.BlockSpec((None, 1, ADA_NT), lambda l, j: (l, 0, j)),
        ],
        out_specs=pl.BlockSpec((None, n, ADA_NT), lambda l, j: (l, 0, j)),
        out_shape=jax.ShapeDtypeStruct((DEPTH, n, cols), F32),
        compiler_params=pltpu.CompilerParams(
            dimension_semantics=("arbitrary", "arbitrary"), vmem_limit_bytes=VMEM_LIMIT),
        name="adaln",
    )(c_all, w_ada, b_ada.reshape(DEPTH, 1, cols))


def _in_proj(x, sh1, sc1, nmix_ref, win_ref, bin_ref):
    h = (_rms(x, nmix_ref[...]) * (1.0 + sc1) + sh1).astype(BF16)
    c0, c1, c2 = 2 * D_CONV, 2 * D_CONV + D_POOL, IN_COLS
    zu = _dot(h, win_ref[:, 0:c0]) + bin_ref[:, 0:c0]
    u = zu[:, :D_CONV] * jax.nn.sigmoid(zu[:, D_CONV:])
    up = _dot(h, win_ref[:, c0:c1]) + bin_ref[:, c0:c1]
    zg = _dot(h, win_ref[:, c1:c2]) + bin_ref[:, c1:c2]
    return u, up, zg[:, :D_MODEL], zg[:, D_MODEL:]


def _conv_act(acc, cb_ref, lng_ref, lnb_ref):
    v = acc + cb_ref[...]
    mu = jnp.mean(v, axis=-1, keepdims=True)
    d = v - mu
    var = jnp.mean(d * d, axis=-1, keepdims=True)
    vn = d * jax.lax.rsqrt(var + EPS) * lng_ref[...] + lnb_ref[...]
    return vn * jax.nn.sigmoid(vn)


def _merge_and_route(x, v_bf, pooled, gc, gp, gt1, sh2, sc2, wco_ref, wpool_ref, pscale_ref,
                     wout_ref, nffn_ref, wr_ref, br_ref):
    rows = x.shape[0]
    y_conv = _dot(v_bf, wco_ref[...])
    y_pool = jnp.concatenate(
        [_dot(pooled[g].astype(BF16), wpool_ref[g]) for g in range(len(POOL_WINDOWS))], axis=-1)
    y_pool = y_pool * pscale_ref[...]
    m = jax.nn.sigmoid(gc) * y_conv + jax.nn.sigmoid(gp) * y_pool
    x1 = x + gt1 * _dot(m.astype(BF16), wout_ref[...])
    h2 = _rms(x1, nffn_ref[...]) * (1.0 + sc2) + sh2
    logits = _dot(h2.astype(BF16), wr_ref[...]) + br_ref[...]
    lane = jax.lax.broadcasted_iota(jnp.int32, (rows, LANES), 1)
    lane_f = lane.astype(F32)
    neg = jnp.full((rows, LANES), -jnp.inf, F32)
    l = jnp.where(lane < N_EXPERTS, logits, neg)
    vals, idxs = [], []
    for _ in range(TOP_K):
        mx = jnp.max(l, axis=-1, keepdims=True)
        ix = jnp.min(jnp.where(l == mx, lane_f, float(LANES)), axis=-1, keepdims=True)
        l = jnp.where(lane_f == ix, neg, l)
        vals.append(mx)
        idxs.append(ix)
    es = [jnp.exp(v - vals[0]) for v in vals]
    den = es[0] + es[1] + es[2] + es[3]
    tw = jnp.zeros((rows, LANES), F32)
    ti = jnp.zeros((rows, LANES), F32)
    for k in range(TOP_K):
        tw = jnp.where(lane == k, es[k] / den, tw)
        ti = jnp.where(lane == k, idxs[k], ti)
    return x1, h2, tw, ti.astype(jnp.int32)


_WEIGHT_NAMES = ("nmix", "win", "bin", "cw", "cb", "lng", "lnb", "wco", "wpool", "pscale", "wout",
                 "nffn", "wr", "br")


def _weight_specs(l, n_grid):
    def spec(*shape):
        zeros = (0,) * len(shape)
        if n_grid == 2:
            return pl.BlockSpec((None,) + shape, lambda b, t: (l,) + zeros)
        return pl.BlockSpec((None,) + shape, lambda i: (l,) + zeros)
    return [
        spec(1, D_MODEL),
        spec(D_MODEL, IN_COLS),
        spec(1, IN_COLS),
        spec(CONV_WIDTH, D_CONV),
        spec(1, D_CONV),
        spec(1, D_CONV),
        spec(1, D_CONV),
        spec(D_CONV, D_MODEL),
        spec(len(POOL_WINDOWS), POOL_GROUP_IN, POOL_GROUP_OUT),
        spec(1, D_MODEL),
        spec(D_MODEL, D_MODEL),
        spec(1, D_MODEL),
        spec(D_MODEL, LANES),
        spec(1, LANES),
    ]


def _prompt_mixer_kernel(*refs, has_moe):
    if has_moe:
        x_ref, moe_ref, gtp_ref = refs[:3]
        refs = refs[3:]
    else:
        x_ref = refs[0]
        refs = refs[1:]
    mod_ref = refs[0]
    w = dict(zip(_WEIGHT_NAMES, refs[1:1 + len(_WEIGHT_NAMES)]))
    (x1_ref, h2_ref, tw_ref, ti_ref, nconv_ref, npool_ref,
     uhist, phist, vbuf) = refs[1 + len(_WEIGHT_NAMES):]

    t = pl.program_id(1)
    nt = pl.num_programs(1)

    x = x_ref[...]
    if has_moe:
        x = x + gtp_ref[...] * moe_ref[...]
    sh1, sc1, gt1 = mod_ref[0:1, :], mod_ref[1:2, :], mod_ref[2:3, :]
    sh2, sc2 = mod_ref[3:4, :], mod_ref[4:5, :]

    @pl.when(t == 0)
    def _():
        uhist[0:CONV_HALO, :] = jnp.zeros((CONV_HALO, D_CONV), F32)
        phist[0:POOL_HALO, :] = jnp.zeros((POOL_HALO, D_POOL), F32)

    u, up, gc, gp = _in_proj(x, sh1, sc1, w["nmix"], w["win"], w["bin"])
    uhist[CONV_HALO:CONV_HALO + TT, :] = u
    phist[POOL_HALO:POOL_HALO + TT, :] = up

    for c in range(TT // CONV_ROWS):
        acc = jnp.zeros((CONV_ROWS, D_CONV), F32)
        for k in range(CONV_WIDTH):
            start = CONV_HALO - CONV_BUF + k + c * CONV_ROWS
            acc = acc + w["cw"][k:k + 1, :] * uhist[start:start + CONV_ROWS, :]
        s = _conv_act(acc, w["cb"], w["lng"], w["lnb"])
        vbuf[c * CONV_ROWS:(c + 1) * CONV_ROWS, :] = s.astype(BF16)

    pos = t * TT + jax.lax.broadcasted_iota(jnp.int32, (TT, 1), 0)
    pooled = []
    for g, win in enumerate(POOL_WINDOWS):
        lo, hi = g * POOL_GROUP_IN, (g + 1) * POOL_GROUP_IN
        cur = phist[POOL_HALO:POOL_HALO + TT, lo:hi]
        ssum = cur
        for i in range(1, win):
            ssum = ssum + phist[POOL_HALO - i:POOL_HALO - i + TT, lo:hi]
        cnt = jnp.minimum(pos + 1, win).astype(F32)
        pooled.append(ssum / cnt - cur)

    x1, h2, tw, ti = _merge_and_route(
        x, vbuf[...], pooled, gc, gp, gt1, sh2, sc2, w["wco"], w["wpool"], w["pscale"],
        w["wout"], w["nffn"], w["wr"], w["br"])
    x1_ref[...] = x1
    h2_ref[...] = h2
    tw_ref[...] = tw
    ti_ref[...] = ti

    @pl.when(t == nt - 1)
    def _():
        nconv_ref[...] = uhist[CONV_HALO + TT - CONV_BUF:CONV_HALO + TT, :]
        npool_ref[...] = phist[POOL_HALO + TT - POOL_BUF:POOL_HALO + TT, :]

    uhist[0:CONV_HALO, :] = uhist[TT:TT + CONV_HALO, :]
    phist[0:POOL_HALO, :] = phist[TT:TT + POOL_HALO, :]


def _prompt_mixer(l, x, moe, gtp, mod, weights, batch, seq):
    nt = seq // TT
    n_tok = batch * seq
    has_moe = moe is not None
    row_spec = pl.BlockSpec((TT, D_MODEL), lambda b, t: (b * nt + t, 0))
    lane_spec = pl.BlockSpec((TT, LANES), lambda b, t: (b * nt + t, 0))
    in_specs = [row_spec]
    args = [x]
    if has_moe:
        in_specs += [row_spec, pl.BlockSpec((None, 1, D_MODEL), lambda b, t: (b, 0, 0))]
        args += [moe, gtp]
    in_specs.append(pl.BlockSpec((None, N_MOD, D_MODEL), lambda b, t: (b, 0, 0)))
    args.append(mod)
    in_specs += _weight_specs(l, 2)
    args += list(weights)
    out_shape = (
        jax.ShapeDtypeStruct((n_tok, D_MODEL), F32),
        jax.ShapeDtypeStruct((n_tok, D_MODEL), F32),
        jax.ShapeDtypeStruct((n_tok, LANES), F32),
        jax.ShapeDtypeStruct((n_tok, LANES), jnp.int32),
        jax.ShapeDtypeStruct((batch, CONV_BUF, D_CONV), F32),
        jax.ShapeDtypeStruct((batch, POOL_BUF, D_POOL), F32),
    )
    out_specs = (
        row_spec, row_spec, lane_spec, lane_spec,
        pl.BlockSpec((None, CONV_BUF, D_CONV), lambda b, t: (b, 0, 0)),
        pl.BlockSpec((None, POOL_BUF, D_POOL), lambda b, t: (b, 0, 0)),
    )
    return pl.pallas_call(
        functools.partial(_prompt_mixer_kernel, has_moe=has_moe),
        grid=(batch, nt),
        in_specs=in_specs,
        out_specs=out_specs,
        out_shape=out_shape,
        scratch_shapes=[
            pltpu.VMEM((CONV_HALO + TT, D_CONV), F32),
            pltpu.VMEM((POOL_HALO + TT, D_POOL), F32),
            pltpu.VMEM((TT, D_CONV), BF16),
        ],
        compiler_params=pltpu.CompilerParams(
            dimension_semantics=("arbitrary", "arbitrary"), vmem_limit_bytes=VMEM_LIMIT),
        name=f"prompt_mixer_l{l}",
    )(*args)


def _sample_mixer_kernel(*refs, has_moe, dec_seq, pos0):
    if has_moe:
        x_ref, moe_ref, gtp_ref = refs[:3]
        refs = refs[3:]
    else:
        x_ref = refs[0]
        refs = refs[1:]
    mod_ref, cs_ref, ps_ref = refs[:3]
    w = dict(zip(_WEIGHT_NAMES, refs[3:3 + len(_WEIGHT_NAMES)]))
    (x1_ref, h2_ref, tw_ref, ti_ref, nconv_ref, npool_ref,
     ufull, pfull, vbuf) = refs[3 + len(_WEIGHT_NAMES):]
    rows = dec_seq * SB

    def per_row(v):
        return jnp.concatenate([v] * dec_seq, axis=0)

    x = x_ref[...].reshape(rows, D_MODEL)
    if has_moe:
        x = x + per_row(gtp_ref[...]) * moe_ref[...].reshape(rows, D_MODEL)
    sh1, sc1, gt1 = per_row(mod_ref[0]), per_row(mod_ref[1]), per_row(mod_ref[2])
    sh2, sc2 = per_row(mod_ref[3]), per_row(mod_ref[4])

    u, up, gc, gp = _in_proj(x, sh1, sc1, w["nmix"], w["win"], w["bin"])
    ufull[0:CONV_BUF] = cs_ref[...]
    pfull[0:POOL_BUF] = ps_ref[...]
    for j in range(dec_seq):
        ufull[CONV_BUF + j] = u[j * SB:(j + 1) * SB, :]
        pfull[POOL_BUF + j] = up[j * SB:(j + 1) * SB, :]
    nconv_ref[...] = ufull[dec_seq:dec_seq + CONV_BUF]
    npool_ref[...] = pfull[dec_seq:dec_seq + POOL_BUF]

    for j in range(dec_seq):
        acc = jnp.zeros((SB, D_CONV), F32)
        for k in range(CONV_WIDTH):
            acc = acc + w["cw"][k:k + 1, :] * ufull[j + k]
        s = _conv_act(acc, w["cb"], w["lng"], w["lnb"])
        vbuf[j * SB:(j + 1) * SB, :] = s.astype(BF16)

    pooled = []
    for g, win in enumerate(POOL_WINDOWS):
        lo, hi = g * POOL_GROUP_IN, (g + 1) * POOL_GROUP_IN
        parts = []
        for j in range(dec_seq):
            cur = pfull[POOL_BUF + j, :, lo:hi]
            ssum = cur
            for i in range(1, win):
                ssum = ssum + pfull[POOL_BUF + j - i, :, lo:hi]
            cnt = float(min(pos0 + j + 1, win))
            parts.append(ssum / cnt - cur)
        pooled.append(jnp.concatenate(parts, axis=0))

    x1, h2, tw, ti = _merge_and_route(
        x, vbuf[...], pooled, gc, gp, gt1, sh2, sc2, w["wco"], w["wpool"], w["pscale"],
        w["wout"], w["nffn"], w["wr"], w["br"])
    x1_ref[...] = x1.reshape(dec_seq, SB, D_MODEL)
    h2_ref[...] = h2.reshape(dec_seq, SB, D_MODEL)
    tw_ref[...] = tw.reshape(dec_seq, SB, LANES)
    ti_ref[...] = ti.reshape(dec_seq, SB, LANES)


def _sample_mixer(l, x, moe, gtp, mod, conv_state, pool_state, weights, dec_batch, dec_seq, pos0):
    has_moe = moe is not None
    tok_spec = pl.BlockSpec((dec_seq, SB, D_MODEL), lambda i: (0, i, 0))
    lane_spec = pl.BlockSpec((dec_seq, SB, LANES), lambda i: (0, i, 0))
    in_specs = [tok_spec]
    args = [x]
    if has_moe:
        in_specs += [tok_spec, pl.BlockSpec((SB, D_MODEL), lambda i: (i, 0))]
        args += [moe, gtp]
    conv_spec = pl.BlockSpec((CONV_BUF, SB, D_CONV), lambda i: (0, i, 0))
    pool_spec = pl.BlockSpec((POOL_BUF, SB, D_POOL), lambda i: (0, i, 0))
    in_specs += [pl.BlockSpec((N_MOD, SB, D_MODEL), lambda i: (0, i, 0)), conv_spec, pool_spec]
    args += [mod, conv_state, pool_state]
    in_specs += _weight_specs(l, 1)
    args += list(weights)
    out_shape = (
        jax.ShapeDtypeStruct((dec_seq, dec_batch, D_MODEL), F32),
        jax.ShapeDtypeStruct((dec_seq, dec_batch, D_MODEL), F32),
        jax.ShapeDtypeStruct((dec_seq, dec_batch, LANES), F32),
        jax.ShapeDtypeStruct((dec_seq, dec_batch, LANES), jnp.int32),
        jax.ShapeDtypeStruct((CONV_BUF, dec_batch, D_CONV), F32),
        jax.ShapeDtypeStruct((POOL_BUF, dec_batch, D_POOL), F32),
    )
    out_specs = (tok_spec, tok_spec, lane_spec, lane_spec, conv_spec, pool_spec)
    return pl.pallas_call(
        functools.partial(_sample_mixer_kernel, has_moe=has_moe, dec_seq=dec_seq, pos0=pos0),
        grid=(dec_batch // SB,),
        in_specs=in_specs,
        out_specs=out_specs,
        out_shape=out_shape,
        scratch_shapes=[
            pltpu.VMEM((CONV_BUF + dec_seq, SB, D_CONV), F32),
            pltpu.VMEM((POOL_BUF + dec_seq, SB, D_POOL), F32),
            pltpu.VMEM((dec_seq * SB, D_CONV), BF16),
        ],
        compiler_params=pltpu.CompilerParams(
            dimension_semantics=("arbitrary",), vmem_limit_bytes=VMEM_LIMIT),
        name=f"sample_mixer_l{l}",
    )(*args)


def _experts_kernel(be_ref, nu_ref, x_ref, wup_ref, bup_ref, wdn_ref, bdn_ref, o_ref, wup_bf, wdn_bf):
    b = pl.program_id(0)

    @pl.when(b < nu_ref[0])
    def _():
        e = be_ref[b]
        prev = be_ref[jnp.maximum(b - 1, 0)]

        @pl.when((b == 0) | (e != prev))
        def _():
            for r in range(0, D_MODEL, LANES):
                wup_bf[r:r + LANES, :] = wup_ref[r:r + LANES, :].astype(BF16)
                wdn_bf[r:r + LANES, :] = wdn_ref[r:r + LANES, :].astype(BF16)

        a = _dot(x_ref[...].astype(BF16), wup_bf[...]) + bup_ref[...]
        a_glu = jnp.minimum(a[:, :D_EXPERT], SWIGLU_LIMIT)
        a_lin = jnp.clip(a[:, D_EXPERT:], -SWIGLU_LIMIT, SWIGLU_LIMIT)
        o = a_glu * jax.nn.sigmoid(SWIGLU_ALPHA * a_glu) * (a_lin + 1.0)
        o_ref[...] = _dot(o.astype(BF16), wdn_bf[...]) + bdn_ref[...]


def _experts(l, x_sorted, block_expert, n_used, w_up, b_up, w_down, b_down):
    n_slots = x_sorted.shape[0]
    nb = n_slots // BM

    def row_map(b, be, nu):
        return (jnp.minimum(b, nu[0] - 1), 0)

    grid_spec = pltpu.PrefetchScalarGridSpec(
        num_scalar_prefetch=2,
        grid=(nb,),
        in_specs=[
            pl.BlockSpec((BM, D_MODEL), row_map),
            pl.BlockSpec((None, None, D_MODEL, 2 * D_EXPERT), lambda b, be, nu: (l, be[b], 0, 0)),
            pl.BlockSpec((None, None, 1, 2 * D_EXPERT), lambda b, be, nu: (l, be[b], 0, 0)),
            pl.BlockSpec((None, None, D_EXPERT, D_MODEL), lambda b, be, nu: (l, be[b], 0, 0)),
            pl.BlockSpec((None, None, 1, D_MODEL), lambda b, be, nu: (l, be[b], 0, 0)),
        ],
        out_specs=pl.BlockSpec((BM, D_MODEL), row_map),
        scratch_shapes=[
            pltpu.VMEM((D_MODEL, 2 * D_EXPERT), BF16),
            pltpu.VMEM((D_EXPERT, D_MODEL), BF16),
        ],
    )
    return pl.pallas_call(
        _experts_kernel,
        grid_spec=grid_spec,
        out_shape=jax.ShapeDtypeStruct((n_slots, D_MODEL), F32),
        compiler_params=pltpu.CompilerParams(
            dimension_semantics=("arbitrary",), vmem_limit_bytes=VMEM_LIMIT),
        name=f"experts_l{l}",
    )(block_expert, n_used, x_sorted, w_up,
      b_up.reshape(DEPTH, N_EXPERTS, 1, 2 * D_EXPERT), w_down,
      b_down.reshape(DEPTH, N_EXPERTS, 1, D_MODEL))


def _route(ti_all):
    n_tok = ti_all.shape[0]
    n_assign = n_tok * TOP_K
    nb = n_assign // BM + N_EXPERTS
    flat_e = ti_all[:, :TOP_K].reshape(n_assign)
    onehot = (flat_e[:, None] == jnp.arange(N_EXPERTS, dtype=jnp.int32)[None, :]).astype(jnp.int32)
    csum = jnp.cumsum(onehot, axis=0)
    rank = jnp.take_along_axis(csum, flat_e[:, None], axis=1)[:, 0] - 1
    counts = csum[-1]
    nblk = (counts + BM - 1) // BM
    blk_end = jnp.cumsum(nblk)
    pad_start = (blk_end - nblk) * BM
    dest = pad_start[flat_e] + rank
    block_expert = jnp.minimum(
        jnp.searchsorted(blk_end, jnp.arange(nb, dtype=jnp.int32), side="right"),
        N_EXPERTS - 1).astype(jnp.int32)
    slot_tok = jnp.zeros((nb * BM,), jnp.int32).at[dest].set(
        jnp.arange(n_assign, dtype=jnp.int32) // TOP_K)
    return dest, slot_tok, block_expert, blk_end[-1:].astype(jnp.int32)


def _final_kernel(x1_ref, moe_ref, gt_ref, g_ref, o_ref):
    x = x1_ref[...] + gt_ref[...] * moe_ref[...]
    o_ref[...] = _rms(x, g_ref[...])


def _final(x1, moe, gt_rows, norm_final, rows_per_gt):
    n = x1.shape[0]
    tile = min(rows_per_gt, 512)
    per = rows_per_gt // tile
    row_spec = pl.BlockSpec((tile, D_MODEL), lambda i: (i, 0))
    return pl.pallas_call(
        _final_kernel,
        grid=(n // tile,),
        in_specs=[row_spec, row_spec,
                  pl.BlockSpec((None, 1, D_MODEL), lambda i: (i // per, 0, 0)),
                  pl.BlockSpec((1, D_MODEL), lambda i: (0, 0))],
        out_specs=row_spec,
        out_shape=jax.ShapeDtypeStruct((n, D_MODEL), F32),
        compiler_params=pltpu.CompilerParams(dimension_semantics=("arbitrary",)),
        name="final_norm",
    )(x1, moe, gt_rows, norm_final.reshape(1, D_MODEL))


def _final_sample_kernel(x1_ref, moe_ref, gt_ref, g_ref, o_ref):
    x = x1_ref[...] + gt_ref[...][None, :, :] * moe_ref[...]
    o_ref[...] = _rms(x, g_ref[...])


def _final_sample(x1, moe, gt, norm_final):
    dec_seq, dec_batch, _ = x1.shape
    spec = pl.BlockSpec((dec_seq, dec_batch, D_MODEL), lambda i: (0, 0, 0))
    return pl.pallas_call(
        _final_sample_kernel,
        grid=(1,),
        in_specs=[spec, spec, pl.BlockSpec((dec_batch, D_MODEL), lambda i: (0, 0)),
                  pl.BlockSpec((1, D_MODEL), lambda i: (0, 0))],
        out_specs=spec,
        out_shape=jax.ShapeDtypeStruct(x1.shape, F32),
        name="final_norm_sample",
    )(x1, moe, gt, norm_final.reshape(1, D_MODEL))


def kernel(x_prompt, x_sample, state_conv, state_pool, c_prompt, c_sample, w_ada, b_ada, norm_mix, w_in, b_in, conv_w, conv_b, conv_ln_g, conv_ln_b, w_conv_out, w_pool, pool_scale, w_out, norm_ffn, w_router, b_router, w_up, b_up, w_down, b_down, norm_final):
    batch, seq, _ = x_prompt.shape
    dec_batch, dec_seq, _ = x_sample.shape
    n_p = batch * seq
    n_s = dec_batch * dec_seq

    vec = lambda a: a.reshape(DEPTH, 1, a.shape[-1])
    weights = (
        vec(norm_mix), w_in.astype(BF16), vec(b_in), conv_w, vec(conv_b), vec(conv_ln_g),
        vec(conv_ln_b), w_conv_out.astype(BF16), w_pool.astype(BF16), vec(pool_scale),
        w_out.astype(BF16), vec(norm_ffn),
        jnp.pad(w_router, ((0, 0), (0, 0), (0, LANES - N_EXPERTS))).astype(BF16),
        vec(jnp.pad(b_router, ((0, 0), (0, LANES - N_EXPERTS)))),
    )

    mod = _adaln(jnp.concatenate([c_prompt, c_sample], axis=0), w_ada, b_ada)
    mod = mod.reshape(DEPTH, batch + dec_batch, N_MOD, D_MODEL)
    mod_p = mod[:, :batch]
    mod_s = jnp.transpose(mod[:, batch:], (0, 2, 1, 3))

    xp = x_prompt.reshape(n_p, D_MODEL)
    xs = jnp.transpose(x_sample, (1, 0, 2))
    cs = jnp.transpose(state_conv, (0, 2, 1, 3))
    ps = jnp.transpose(state_pool, (0, 2, 1, 3))

    moe_p = moe_s = gtp_p = gtp_s = None
    new_conv_p, new_pool_p, new_conv_s, new_pool_s = [], [], [], []
    for l in range(DEPTH):
        x1p, h2p, twp, tip, ncp, npp = _prompt_mixer(l, xp, moe_p, gtp_p, mod_p[l], weights, batch, seq)
        x1s, h2s, tws, tis, ncs, nps = _sample_mixer(
            l, xs, moe_s, gtp_s, mod_s[l], cs[l], ps[l], weights, dec_batch, dec_seq, PAST_LEN)
        new_conv_p.append(ncp)
        new_pool_p.append(npp)
        new_conv_s.append(jnp.transpose(ncs, (1, 0, 2)))
        new_pool_s.append(jnp.transpose(nps, (1, 0, 2)))

        h2 = jnp.concatenate([h2p, h2s.reshape(n_s, D_MODEL)], axis=0)
        tw = jnp.concatenate([twp, tws.reshape(n_s, LANES)], axis=0)[:, :TOP_K]
        ti = jnp.concatenate([tip, tis.reshape(n_s, LANES)], axis=0)
        dest, slot_tok, block_expert, n_used = _route(ti)
        y_sorted = _experts(l, h2[slot_tok], block_expert, n_used, w_up, b_up, w_down, b_down)
        moe = jnp.sum(y_sorted[dest].reshape(n_p + n_s, TOP_K, D_MODEL) * tw[:, :, None], axis=1)

        xp, xs = x1p, x1s
        moe_p = moe[:n_p]
        moe_s = moe[n_p:].reshape(dec_seq, dec_batch, D_MODEL)
        gtp_p = mod_p[l][:, 5:6, :]
        gtp_s = mod_s[l][5]

    y_prompt = _final(xp, moe_p, gtp_p, norm_final, seq).reshape(batch, seq, D_MODEL)
    y_sample = jnp.transpose(_final_sample(xs, moe_s, gtp_s, norm_final), (1, 0, 2))
    return (y_prompt, y_sample, jnp.stack(new_conv_p), jnp.stack(new_pool_p),
            jnp.stack(new_conv_s), jnp.stack(new_pool_s))
```

```python
import functools

import jax
import jax.numpy as jnp
from jax.experimental import pallas as pl
from jax.experimental.pallas import tpu as pltpu

F32 = jnp.float32
BF16 = jnp.bfloat16
I32 = jnp.int32

D_MODEL = 1024
DEPTH = 2
D_CONV = 512
CONV_WIDTH = 31
CONV_BUF = CONV_WIDTH - 1
D_POOL = 512
POOL_WINDOWS = (2, 4, 8, 16)
POOL_GROUP_IN = D_POOL // len(POOL_WINDOWS)
POOL_GROUP_OUT = D_MODEL // len(POOL_WINDOWS)
POOL_BUF = max(POOL_WINDOWS) - 1
IN_COLS = 2 * D_CONV + D_POOL + 2 * D_MODEL
N_EXPERTS = 32
TOP_K = 4
D_EXPERT = D_MODEL
SWIGLU_LIMIT = 7.0
SWIGLU_ALPHA = 1.702
N_MOD = 6
EPS = 1e-6
PAST_LEN = 16384

LANES = 128
SUBLANES = 8
VMEM_LIMIT = 52 * 1024 * 1024

TT = 256
CONV_HALO = 32
POOL_HALO = 16
CONV_ROWS = 64
SB = 32
BM = 256
ADA_NT = 1536
CH = 512
ROW_TILE = D_MODEL // LANES
EXPERT_BITS = 5
CONV_SHIFT_ROWS = TT + CONV_HALO - SUBLANES
assert N_EXPERTS == 1 << EXPERT_BITS and ROW_TILE == SUBLANES


def _rms(x, g):
    return x * jax.lax.rsqrt(jnp.mean(x * x, axis=-1, keepdims=True) + EPS) * g


def _dot(a, b):
    return jnp.dot(a, b, preferred_element_type=F32)


def _store_token_tiles(ref, v):
    rows = v.shape[0]
    for s in range(ROW_TILE):
        ref[pl.ds(s, rows, stride=ROW_TILE), :] = v[:, s * LANES:(s + 1) * LANES]


def _load_token_tiles(ref, first_token, rows):
    return jnp.concatenate(
        [ref[pl.ds(first_token * ROW_TILE + s, rows, stride=ROW_TILE), :] for s in range(ROW_TILE)], axis=-1)


def _token_tile(ref, t):
    return ref.at[pl.ds(pl.multiple_of(t * ROW_TILE, ROW_TILE), ROW_TILE), :]


def _adaln_kernel(c_ref, w_ref, b_ref, o_ref):
    c = c_ref[...]
    a = (c * jax.nn.sigmoid(c)).astype(BF16)
    o_ref[...] = _dot(a, w_ref[...].astype(BF16)) + b_ref[...]


def _adaln(c_all, w_ada, b_ada):
    n = c_all.shape[0]
    cols = N_MOD * D_MODEL
    return pl.pallas_call(
        _adaln_kernel,
        grid=(DEPTH, cols // ADA_NT),
        in_specs=[
            pl.BlockSpec((n, D_MODEL), lambda l, j: (0, 0)),
            pl.BlockSpec((None, D_MODEL, ADA_NT), lambda l, j: (l, 0, j)),
            pl.BlockSpec((None, 1, ADA_NT), lambda l, j: (l, 0, j)),
        ],
        out_specs=pl.BlockSpec((None, n, ADA_NT), lambda l, j: (l, 0, j)),
        out_shape=jax.ShapeDtypeStruct((DEPTH, n, cols), F32),
        compiler_params=pltpu.CompilerParams(
            dimension_semantics=("arbitrary", "arbitrary"), vmem_limit_bytes=VMEM_LIMIT),
        name="adaln",
    )(c_all, w_ada, b_ada.reshape(DEPTH, 1, cols))


def _in_proj(x, sh1, sc1, nmix_ref, win_ref, bin_ref):
    h = (_rms(x, nmix_ref[...]) * (1.0 + sc1) + sh1).astype(BF16)
    c0, c1, c2 = 2 * D_CONV, 2 * D_CONV + D_POOL, IN_COLS
    zu = _dot(h, win_ref[:, 0:c0]) + bin_ref[:, 0:c0]
    u = zu[:, :D_CONV] * jax.nn.sigmoid(zu[:, D_CONV:])
    up = _dot(h, win_ref[:, c0:c1]) + bin_ref[:, c0:c1]
    zg = _dot(h, win_ref[:, c1:c2]) + bin_ref[:, c1:c2]
    return u, up, zg[:, :D_MODEL], zg[:, D_MODEL:]


def _conv_act(acc, cb_ref, lng_ref, lnb_ref):
    v = acc + cb_ref[...]
    mu = jnp.mean(v, axis=-1, keepdims=True)
    d = v - mu
    var = jnp.mean(d * d, axis=-1, keepdims=True)
    vn = d * jax.lax.rsqrt(var + EPS) * lng_ref[...] + lnb_ref[...]
    return vn * jax.nn.sigmoid(vn)


def _merge_and_route(x, v_bf, pooled, gc, gp, gt1, sh2, sc2, carry, w):
    rows = x.shape[0]
    y_conv = _dot(v_bf, w["wco"][...])
    y_pool = jnp.concatenate(
        [_dot(pooled[g].astype(BF16), w["wpool"][g]) for g in range(len(POOL_WINDOWS))], axis=-1)
    y_pool = y_pool * w["pscale"][...]
    m = jax.nn.sigmoid(gc) * y_conv + jax.nn.sigmoid(gp) * y_pool
    x1 = x + gt1 * _dot(m.astype(BF16), w["wout"][...])
    h2 = _rms(x1, w["nffn"][...]) * (1.0 + sc2) + sh2
    logits = _dot(h2.astype(BF16), w["wr"][...]) + w["br"][...]
    lane = jax.lax.broadcasted_iota(I32, (rows, LANES), 1)
    lane_f = lane.astype(F32)
    neg = jnp.full((rows, LANES), -jnp.inf, F32)
    l = jnp.where(lane < N_EXPERTS, logits, neg)
    vals, idxs, sels = [], [], []
    for _ in range(TOP_K):
        mx = jnp.max(l, axis=-1, keepdims=True)
        ix = jnp.min(jnp.where(l == mx, lane_f, float(LANES)), axis=-1, keepdims=True)
        sel = lane_f == ix
        l = jnp.where(sel, neg, l)
        vals.append(mx)
        idxs.append(ix)
        sels.append(sel)
    es = [jnp.exp(v - vals[0]) for v in vals]
    den = es[0] + es[1] + es[2] + es[3]

    onehot = jnp.where(sels[0] | sels[1] | sels[2] | sels[3], 1.0, 0.0)
    r_i = jax.lax.broadcasted_iota(I32, (rows, rows), 0)
    c_i = jax.lax.broadcasted_iota(I32, (rows, rows), 1)
    before = jnp.where(c_i < r_i, 1.0, 0.0).astype(BF16)
    base = _dot(before, onehot.astype(BF16)) + carry
    new_carry = carry + jnp.sum(onehot, axis=0, keepdims=True)

    tw = jnp.zeros((rows, LANES), F32)
    pk = jnp.zeros((rows, LANES), F32)
    for k in range(TOP_K):
        rank = jnp.sum(jnp.where(sels[k], base, 0.0), axis=-1, keepdims=True)
        tw = jnp.where(lane == k, es[k] / den, tw)
        pk = jnp.where(lane == k, rank * float(N_EXPERTS) + idxs[k], pk)
    pk8 = jnp.transpose(pk)[0:SUBLANES, :].astype(I32)
    return x1, h2, tw, pk8, new_carry


_WEIGHT_NAMES = ("nmix", "win", "bin", "cw", "cb", "lng", "lnb", "wco", "wpool", "pscale", "wout",
                 "nffn", "wr", "br")


def _weight_specs(l, n_grid):
    def spec(*shape):
        zeros = (0,) * len(shape)
        if n_grid == 2:
            return pl.BlockSpec((None,) + shape, lambda b, t: (l,) + zeros)
        return pl.BlockSpec((None,) + shape, lambda i: (l,) + zeros)
    return [
        spec(1, D_MODEL),
        spec(D_MODEL, IN_COLS),
        spec(1, IN_COLS),
        spec(CONV_WIDTH, D_CONV),
        spec(1, D_CONV),
        spec(1, D_CONV),
        spec(1, D_CONV),
        spec(D_CONV, D_MODEL),
        spec(len(POOL_WINDOWS), POOL_GROUP_IN, POOL_GROUP_OUT),
        spec(1, D_MODEL),
        spec(D_MODEL, D_MODEL),
        spec(1, D_MODEL),
        spec(D_MODEL, LANES),
        spec(1, LANES),
    ]


def _mixer_out_shapes(n):
    return (
        jax.ShapeDtypeStruct((n, D_MODEL), F32),
        jax.ShapeDtypeStruct((n * ROW_TILE, LANES), F32),
        jax.ShapeDtypeStruct((n, LANES), F32),
        jax.ShapeDtypeStruct((SUBLANES, n), I32),
        jax.ShapeDtypeStruct((SUBLANES, LANES), F32),
    )


def _prompt_mixer_kernel(x_ref, mod_ref, *refs):
    w = dict(zip(_WEIGHT_NAMES, refs[:len(_WEIGHT_NAMES)]))
    (x1_ref, h2_ref, tw_ref, pk_ref, cnt_ref, nconv_ref, npool_ref,
     uhist, ushift, phist, vbuf, carry) = refs[len(_WEIGHT_NAMES):]

    b = pl.program_id(0)
    t = pl.program_id(1)
    nt = pl.num_programs(1)

    x = x_ref[...]
    sh1, sc1, gt1 = mod_ref[0:1, :], mod_ref[1:2, :], mod_ref[2:3, :]
    sh2, sc2 = mod_ref[3:4, :], mod_ref[4:5, :]

    @pl.when((b == 0) & (t == 0))
    def _():
        carry[...] = jnp.zeros((SUBLANES, LANES), F32)

    @pl.when(t == 0)
    def _():
        uhist[0:CONV_HALO, :] = jnp.zeros((CONV_HALO, D_CONV), F32)
        phist[0:POOL_HALO, :] = jnp.zeros((POOL_HALO, D_POOL), F32)

    u, up, gc, gp = _in_proj(x, sh1, sc1, w["nmix"], w["win"], w["bin"])
    uhist[CONV_HALO:CONV_HALO + TT, :] = u
    phist[POOL_HALO:POOL_HALO + TT, :] = up

    for r in range(1, SUBLANES):
        ushift[r - 1] = uhist[r:r + CONV_SHIFT_ROWS, :]
    for c in range(TT // CONV_ROWS):
        acc = jnp.zeros((CONV_ROWS, D_CONV), F32)
        for k in range(CONV_WIDTH):
            q, r = divmod(CONV_HALO - CONV_BUF + k, SUBLANES)
            start = q * SUBLANES + c * CONV_ROWS
            if r == 0:
                tap = uhist[start:start + CONV_ROWS, :]
            else:
                tap = ushift[r - 1, start:start + CONV_ROWS, :]
            acc = acc + w["cw"][k:k + 1, :] * tap
        s = _conv_act(acc, w["cb"], w["lng"], w["lnb"])
        vbuf[c * CONV_ROWS:(c + 1) * CONV_ROWS, :] = s.astype(BF16)

    pos = t * TT + jax.lax.broadcasted_iota(I32, (TT, 1), 0)
    pooled = []
    for g, win in enumerate(POOL_WINDOWS):
        lo, hi = g * POOL_GROUP_IN, (g + 1) * POOL_GROUP_IN
        cur = phist[POOL_HALO:POOL_HALO + TT, lo:hi]
        ssum = cur
        for i in range(1, win):
            ssum = ssum + phist[POOL_HALO - i:POOL_HALO - i + TT, lo:hi]
        cnt = jnp.minimum(pos + 1, win).astype(F32)
        pooled.append(ssum / cnt - cur)

    x1, h2, tw, pk8, new_carry = _merge_and_route(
        x, vbuf[...], pooled, gc, gp, gt1, sh2, sc2, carry[0:1, :], w)
    x1_ref[...] = x1
    _store_token_tiles(h2_ref, h2)
    tw_ref[...] = tw
    pk_ref[...] = pk8
    carry[...] = jnp.broadcast_to(new_carry, (SUBLANES, LANES))
    cnt_ref[...] = jnp.broadcast_to(new_carry, (SUBLANES, LANES))

    @pl.when(t == nt - 1)
    def _():
        nconv_ref[...] = uhist[CONV_HALO + TT - CONV_BUF:CONV_HALO + TT, :]
        npool_ref[...] = phist[POOL_HALO + TT - POOL_BUF:POOL_HALO + TT, :]

    uhist[0:CONV_HALO, :] = uhist[TT:TT + CONV_HALO, :]
    phist[0:POOL_HALO, :] = phist[TT:TT + POOL_HALO, :]


def _prompt_mixer(l, x, mod, weights, batch, seq):
    nt = seq // TT
    n_tok = batch * seq
    row = lambda width: pl.BlockSpec((TT, width), lambda b, t: (b * nt + t, 0))
    in_specs = [row(D_MODEL), pl.BlockSpec((None, N_MOD, D_MODEL), lambda b, t: (b, 0, 0))]
    in_specs += _weight_specs(l, 2)
    out_shape = _mixer_out_shapes(n_tok) + (
        jax.ShapeDtypeStruct((batch, CONV_BUF, D_CONV), F32),
        jax.ShapeDtypeStruct((batch, POOL_BUF, D_POOL), F32),
    )
    out_specs = (
        row(D_MODEL), pl.BlockSpec((TT * ROW_TILE, LANES), lambda b, t: (b * nt + t, 0)), row(LANES),
        pl.BlockSpec((SUBLANES, TT), lambda b, t: (0, b * nt + t)),
        pl.BlockSpec((SUBLANES, LANES), lambda b, t: (0, 0)),
        pl.BlockSpec((None, CONV_BUF, D_CONV), lambda b, t: (b, 0, 0)),
        pl.BlockSpec((None, POOL_BUF, D_POOL), lambda b, t: (b, 0, 0)),
    )
    return pl.pallas_call(
        _prompt_mixer_kernel,
        grid=(batch, nt),
        in_specs=in_specs,
        out_specs=out_specs,
        out_shape=out_shape,
        scratch_shapes=[
            pltpu.VMEM((CONV_HALO + TT, D_CONV), F32),
            pltpu.VMEM((SUBLANES - 1, CONV_SHIFT_ROWS, D_CONV), F32),
            pltpu.VMEM((POOL_HALO + TT, D_POOL), F32),
            pltpu.VMEM((TT, D_CONV), BF16),
            pltpu.VMEM((SUBLANES, LANES), F32),
        ],
        compiler_params=pltpu.CompilerParams(
            dimension_semantics=("arbitrary", "arbitrary"), vmem_limit_bytes=VMEM_LIMIT),
        name=f"prompt_mixer_l{l}",
    )(x, mod, *weights)


def _sample_mixer_kernel(x_ref, mod_ref, cs_ref, ps_ref, cnt0_ref, *refs, dec_seq, pos0):
    w = dict(zip(_WEIGHT_NAMES, refs[:len(_WEIGHT_NAMES)]))
    (x1_ref, h2_ref, tw_ref, pk_ref, cnt_ref, nconv_ref, npool_ref,
     ufull, pfull, vbuf, carry) = refs[len(_WEIGHT_NAMES):]

    def per_row(v):
        return jnp.concatenate([v] * dec_seq, axis=0)

    @pl.when(pl.program_id(0) == 0)
    def _():
        carry[...] = cnt0_ref[...]

    x = x_ref[...]
    sh1, sc1, gt1 = per_row(mod_ref[0]), per_row(mod_ref[1]), per_row(mod_ref[2])
    sh2, sc2 = per_row(mod_ref[3]), per_row(mod_ref[4])

    u, up, gc, gp = _in_proj(x, sh1, sc1, w["nmix"], w["win"], w["bin"])
    ufull[0:CONV_BUF] = cs_ref[...]
    pfull[0:POOL_BUF] = ps_ref[...]
    for j in range(dec_seq):
        ufull[CONV_BUF + j] = u[j * SB:(j + 1) * SB, :]
        pfull[POOL_BUF + j] = up[j * SB:(j + 1) * SB, :]
    nconv_ref[...] = ufull[dec_seq:dec_seq + CONV_BUF]
    npool_ref[...] = pfull[dec_seq:dec_seq + POOL_BUF]

    for j in range(dec_seq):
        acc = jnp.zeros((SB, D_CONV), F32)
        for k in range(CONV_WIDTH):
            acc = acc + w["cw"][k:k + 1, :] * ufull[j + k]
        s = _conv_act(acc, w["cb"], w["lng"], w["lnb"])
        vbuf[j * SB:(j + 1) * SB, :] = s.astype(BF16)

    pooled = []
    for g, win in enumerate(POOL_WINDOWS):
        lo, hi = g * POOL_GROUP_IN, (g + 1) * POOL_GROUP_IN
        parts = []
        for j in range(dec_seq):
            cur = pfull[POOL_BUF + j, :, lo:hi]
            ssum = cur
            for i in range(1, win):
                ssum = ssum + pfull[POOL_BUF + j - i, :, lo:hi]
            cnt = float(min(pos0 + j + 1, win))
            parts.append(ssum / cnt - cur)
        pooled.append(jnp.concatenate(parts, axis=0))

    x1, h2, tw, pk8, new_carry = _merge_and_route(
        x, vbuf[...], pooled, gc, gp, gt1, sh2, sc2, carry[0:1, :], w)
    x1_ref[...] = x1
    _store_token_tiles(h2_ref, h2)
    tw_ref[...] = tw
    pk_ref[...] = pk8
    carry[...] = jnp.broadcast_to(new_carry, (SUBLANES, LANES))
    cnt_ref[...] = jnp.broadcast_to(new_carry, (SUBLANES, LANES))


def _sample_mixer(l, x, mod, conv_state, pool_state, cnt0, weights, dec_batch, dec_seq, pos0):
    rows = dec_seq * SB
    n_s = dec_batch * dec_seq
    row = lambda width: pl.BlockSpec((rows, width), lambda i: (i, 0))
    conv_spec = pl.BlockSpec((None, CONV_BUF, SB, D_CONV), lambda i: (i, 0, 0, 0))
    pool_spec = pl.BlockSpec((None, POOL_BUF, SB, D_POOL), lambda i: (i, 0, 0, 0))
    cnt_spec = pl.BlockSpec((SUBLANES, LANES), lambda i: (0, 0))
    in_specs = [row(D_MODEL), pl.BlockSpec((N_MOD, SB, D_MODEL), lambda i: (0, i, 0)),
                conv_spec, pool_spec, cnt_spec]
    in_specs += _weight_specs(l, 1)
    out_shape = _mixer_out_shapes(n_s) + (
        jax.ShapeDtypeStruct(conv_state.shape, F32),
        jax.ShapeDtypeStruct(pool_state.shape, F32),
    )
    out_specs = (row(D_MODEL), pl.BlockSpec((rows * ROW_TILE, LANES), lambda i: (i, 0)), row(LANES),
                 pl.BlockSpec((SUBLANES, rows), lambda i: (0, i)), cnt_spec, conv_spec, pool_spec)
    return pl.pallas_call(
        functools.partial(_sample_mixer_kernel, dec_seq=dec_seq, pos0=pos0),
        grid=(dec_batch // SB,),
        in_specs=in_specs,
        out_specs=out_specs,
        out_shape=out_shape,
        scratch_shapes=[
            pltpu.VMEM((CONV_BUF + dec_seq, SB, D_CONV), F32),
            pltpu.VMEM((POOL_BUF + dec_seq, SB, D_POOL), F32),
            pltpu.VMEM((rows, D_CONV), BF16),
            pltpu.VMEM((SUBLANES, LANES), F32),
        ],
        compiler_params=pltpu.CompilerParams(
            dimension_semantics=("arbitrary",), vmem_limit_bytes=VMEM_LIMIT),
        name=f"sample_mixer_l{l}",
    )(x, mod, conv_state, pool_state, cnt0, *weights)


def _fetch_slots(slots_hbm, n_tok, first_token, count, idx, first_idx, sem):
    copies = [
        pltpu.make_async_copy(slots_hbm.at[pl.ds(k * n_tok + first_token, count)],
                              idx.at[pl.ds(first_idx + k * count, count)], sem)
        for k in range(TOP_K)]
    for cp in copies:
        cp.start()
    for cp in copies:
        cp.wait()


def _dispatch_kernel(slots_hbm, h2p_ref, h2s_ref, xs_hbm, idx, isem, rsem, *, n_prompt_steps, n_tok):
    c = pl.program_id(0)
    _fetch_slots(slots_hbm, n_tok, c * CH, CH, idx, 0, isem)

    def scatter_rows(src_ref):
        def body(t, carry):
            for k in range(TOP_K):
                pltpu.make_async_copy(_token_tile(src_ref, t), _token_tile(xs_hbm, idx[k * CH + t]),
                                      rsem).start(priority=k % 2)
            return carry
        jax.lax.fori_loop(0, CH, body, 0, unroll=4)
        for _ in range(TOP_K):
            pltpu.make_async_copy(src_ref, xs_hbm.at[pl.ds(0, CH * ROW_TILE), :], rsem).wait()

    @pl.when(c < n_prompt_steps)
    def _():
        scatter_rows(h2p_ref)

    @pl.when(c >= n_prompt_steps)
    def _():
        scatter_rows(h2s_ref)


def _dispatch(slots, h2p, h2s, n_slots):
    n_p, n_s = h2p.shape[0] // ROW_TILE, h2s.shape[0] // ROW_TILE
    np_steps, ns_steps = n_p // CH, n_s // CH
    return pl.pallas_call(
        functools.partial(_dispatch_kernel, n_prompt_steps=np_steps, n_tok=n_p + n_s),
        grid=(np_steps + ns_steps,),
        in_specs=[
            pl.BlockSpec(memory_space=pl.ANY),
            pl.BlockSpec((CH * ROW_TILE, LANES), lambda c: (jnp.minimum(c, np_steps - 1), 0)),
            pl.BlockSpec((CH * ROW_TILE, LANES), lambda c: (jnp.maximum(c - np_steps, 0), 0)),
        ],
        out_specs=pl.BlockSpec(memory_space=pl.ANY),
        scratch_shapes=[
            pltpu.SMEM((TOP_K * CH,), I32),
            pltpu.SemaphoreType.DMA,
            pltpu.SemaphoreType.DMA,
        ],
        out_shape=jax.ShapeDtypeStruct((n_slots * ROW_TILE, LANES), F32),
        compiler_params=pltpu.CompilerParams(
            dimension_semantics=("arbitrary",), vmem_limit_bytes=VMEM_LIMIT),
        name="dispatch",
    )(slots, h2p, h2s)


def _experts_kernel(be_ref, nv_ref, nu_ref, x_ref, wup_ref, bup_ref, wdn_ref, bdn_ref, o_ref,
                    wup_bf, wdn_bf):
    b = pl.program_id(0)

    @pl.when(b < nu_ref[0])
    def _():
        e = be_ref[b]
        prev = be_ref[jnp.maximum(b - 1, 0)]

        @pl.when((b == 0) | (e != prev))
        def _():
            for r in range(0, D_MODEL, LANES):
                wup_bf[r:r + LANES, :] = wup_ref[r:r + LANES, :].astype(BF16)
                wdn_bf[r:r + LANES, :] = wdn_ref[r:r + LANES, :].astype(BF16)

        rid = jax.lax.broadcasted_iota(I32, (BM, D_MODEL), 0)
        x = jnp.where(rid < nv_ref[b], _load_token_tiles(x_ref, 0, BM), 0.0)
        a = _dot(x.astype(BF16), wup_bf[...]) + bup_ref[...]
        a_glu = jnp.minimum(a[:, :D_EXPERT], SWIGLU_LIMIT)
        a_lin = jnp.clip(a[:, D_EXPERT:], -SWIGLU_LIMIT, SWIGLU_LIMIT)
        o = a_glu * jax.nn.sigmoid(SWIGLU_ALPHA * a_glu) * (a_lin + 1.0)
        _store_token_tiles(o_ref, _dot(o.astype(BF16), wdn_bf[...]) + bdn_ref[...])

    @pl.when(b >= nu_ref[0])
    def _():
        o_ref[...] = jnp.zeros((BM * ROW_TILE, LANES), F32)


def _experts(l, x_sorted, block_expert, n_valid, n_used, w_up, b_up, w_down, b_down):
    n_slots = x_sorted.shape[0] // ROW_TILE
    nb = n_slots // BM

    def used_map(b, be, nv, nu):
        return (jnp.minimum(b, nu[0] - 1), 0)

    wmap = lambda b, be, nv, nu: (l, be[b], 0, 0)
    grid_spec = pltpu.PrefetchScalarGridSpec(
        num_scalar_prefetch=3,
        grid=(nb,),
        in_specs=[
            pl.BlockSpec((BM * ROW_TILE, LANES), used_map),
            pl.BlockSpec((None, None, D_MODEL, 2 * D_EXPERT), wmap),
            pl.BlockSpec((None, None, 1, 2 * D_EXPERT), wmap),
            pl.BlockSpec((None, None, D_EXPERT, D_MODEL), wmap),
            pl.BlockSpec((None, None, 1, D_MODEL), wmap),
        ],
        out_specs=pl.BlockSpec((BM * ROW_TILE, LANES), lambda b, be, nv, nu: (b, 0)),
        scratch_shapes=[
            pltpu.VMEM((D_MODEL, 2 * D_EXPERT), BF16),
            pltpu.VMEM((D_EXPERT, D_MODEL), BF16),
        ],
    )
    return pl.pallas_call(
        _experts_kernel,
        grid_spec=grid_spec,
        out_shape=jax.ShapeDtypeStruct((n_slots * ROW_TILE, LANES), F32),
        compiler_params=pltpu.CompilerParams(
            dimension_semantics=("arbitrary",), vmem_limit_bytes=VMEM_LIMIT),
        name=f"experts_l{l}",
    )(block_expert, n_valid, n_used, x_sorted, w_up,
      b_up.reshape(DEPTH, N_EXPERTS, 1, 2 * D_EXPERT), w_down,
      b_down.reshape(DEPTH, N_EXPERTS, 1, D_MODEL))


def _block_tables(counts, nb):
    nblk = (counts + BM - 1) // BM
    blk_end = jnp.cumsum(nblk)
    blk_start = blk_end - nblk
    bidx = jnp.arange(nb, dtype=I32)
    block_expert = jnp.minimum(
        jnp.sum((blk_end[None, :] <= bidx[:, None]).astype(I32), axis=1), N_EXPERTS - 1)
    n_valid = jnp.clip(counts[block_expert] - (bidx - blk_start[block_expert]) * BM, 0, BM)
    return ((blk_start * BM).astype(I32), block_expert.astype(I32), n_valid.astype(I32),
            blk_end[-1:].astype(I32))


def _combine_kernel(slots_hbm, y_hbm, x1_ref, tw_ref, gt_ref, g_ref, o_ref, ybuf0, ybuf1, idx,
                    isem, rsem, *, rows, tok0, n_tok, gate_copies, final):
    i = pl.program_id(0)
    n = pl.num_programs(0)
    ybuf = (ybuf0, ybuf1)

    def fetch_indices(step, slot):
        _fetch_slots(slots_hbm, n_tok, tok0 + step * rows, rows, idx, slot * TOP_K * rows, isem.at[slot])

    def gather_rows(slot):
        def body(t, carry):
            for k in range(TOP_K):
                d = idx[(slot * TOP_K + k) * rows + t]
                pltpu.make_async_copy(_token_tile(y_hbm, d), _token_tile(ybuf[slot], k * rows + t),
                                      rsem.at[slot]).start(priority=k % 2)
            return carry
        jax.lax.fori_loop(0, rows, body, 0, unroll=4)

    @pl.when(i == 0)
    def _():
        fetch_indices(0, 0)
        gather_rows(0)

    for s in range(2):
        @pl.when((i % 2 == s) & (i + 1 < n))
        def _():
            fetch_indices(i + 1, 1 - s)
            gather_rows(1 - s)

    for s in range(2):
        @pl.when(i % 2 == s)
        def _():
            pltpu.make_async_copy(y_hbm.at[pl.ds(0, TOP_K * rows * ROW_TILE), :], ybuf[s], rsem.at[s]).wait()
            tw = tw_ref[...]
            moe = tw[:, 0:1] * _load_token_tiles(ybuf[s], 0, rows)
            for k in range(1, TOP_K):
                moe = moe + tw[:, k:k + 1] * _load_token_tiles(ybuf[s], k * rows, rows)
            gt = gt_ref[...]
            if gate_copies > 1:
                gt = jnp.concatenate([gt] * gate_copies, axis=0)
            x = x1_ref[...] + gt * moe
            o_ref[...] = _rms(x, g_ref[...]) if final else x


def _combine(slots, y_sorted, x1, tw, gate, norm_final, *, rows, tok0, gate_spec, gate_copies,
             final, name):
    n = x1.shape[0]
    row = lambda width: pl.BlockSpec((rows, width), lambda i: (i, 0))
    return pl.pallas_call(
        functools.partial(_combine_kernel, rows=rows, tok0=tok0, n_tok=slots.shape[0] // TOP_K,
                          gate_copies=gate_copies, final=final),
        grid=(n // rows,),
        in_specs=[
            pl.BlockSpec(memory_space=pl.ANY),
            pl.BlockSpec(memory_space=pl.ANY),
            row(D_MODEL), row(LANES), gate_spec,
            pl.BlockSpec((1, D_MODEL), lambda i: (0, 0)),
        ],
        out_specs=row(D_MODEL),
        scratch_shapes=[
            pltpu.VMEM((TOP_K * rows * ROW_TILE, LANES), F32),
            pltpu.VMEM((TOP_K * rows * ROW_TILE, LANES), F32),
            pltpu.SMEM((2 * TOP_K * rows,), I32),
            pltpu.SemaphoreType.DMA((2,)),
            pltpu.SemaphoreType.DMA((2,)),
        ],
        out_shape=jax.ShapeDtypeStruct((n, D_MODEL), F32),
        compiler_params=pltpu.CompilerParams(
            dimension_semantics=("arbitrary",), vmem_limit_bytes=VMEM_LIMIT),
        name=name,
    )(slots, y_sorted, x1, tw, gate, norm_final.reshape(1, D_MODEL))


def kernel(x_prompt, x_sample, state_conv, state_pool, c_prompt, c_sample, w_ada, b_ada, norm_mix, w_in, b_in, conv_w, conv_b, conv_ln_g, conv_ln_b, w_conv_out, w_pool, pool_scale, w_out, norm_ffn, w_router, b_router, w_up, b_up, w_down, b_down, norm_final):
    batch, seq, _ = x_prompt.shape
    dec_batch, dec_seq, _ = x_sample.shape
    n_p = batch * seq
    n_s = dec_batch * dec_seq
    n_blocks = dec_batch // SB
    n_assign = (n_p + n_s) * TOP_K
    nb = n_assign // BM + N_EXPERTS

    vec = lambda a: a.reshape(DEPTH, 1, a.shape[-1])
    weights = (
        vec(norm_mix), w_in.astype(BF16), vec(b_in), conv_w, vec(conv_b), vec(conv_ln_g),
        vec(conv_ln_b), w_conv_out.astype(BF16), w_pool.astype(BF16), vec(pool_scale),
        w_out.astype(BF16), vec(norm_ffn),
        jnp.pad(w_router, ((0, 0), (0, 0), (0, LANES - N_EXPERTS))).astype(BF16),
        vec(jnp.pad(b_router, ((0, 0), (0, LANES - N_EXPERTS)))),
    )

    mod = _adaln(jnp.concatenate([c_prompt, c_sample], axis=0), w_ada, b_ada)
    mod = mod.reshape(DEPTH, batch + dec_batch, N_MOD, D_MODEL)
    mod_p = mod[:, :batch]
    mod_s = jnp.transpose(mod[:, batch:], (0, 2, 1, 3))

    def to_sample_order(a, rows):
        return jnp.transpose(a.reshape(n_blocks, SB, rows, a.shape[-1]), (0, 2, 1, 3))

    def from_sample_order(a):
        return jnp.transpose(a, (0, 2, 1, 3)).reshape(dec_batch, a.shape[1], a.shape[-1])

    xp = x_prompt.reshape(n_p, D_MODEL)
    xs = to_sample_order(x_sample, dec_seq).reshape(n_s, D_MODEL)

    gate_p = pl.BlockSpec((None, 1, D_MODEL), lambda i: (i // (seq // TT), 0, 0))
    gate_s = pl.BlockSpec((SB, D_MODEL), lambda i: (i, 0))
    expert_ids = jnp.arange(N_EXPERTS, dtype=I32)

    new_conv_p, new_pool_p, new_conv_s, new_pool_s = [], [], [], []
    for l in range(DEPTH):
        x1p, h2p, twp, pkp, cntp, ncp, npp = _prompt_mixer(l, xp, mod_p[l], weights, batch, seq)
        x1s, h2s, tws, pks, cnt, ncs, nps = _sample_mixer(
            l, xs, mod_s[l], to_sample_order(state_conv[l], CONV_BUF),
            to_sample_order(state_pool[l], POOL_BUF), cntp, weights, dec_batch, dec_seq, PAST_LEN)
        new_conv_p.append(ncp)
        new_pool_p.append(npp)
        new_conv_s.append(from_sample_order(ncs))
        new_pool_s.append(from_sample_order(nps))

        counts = cnt[0, :N_EXPERTS].astype(I32)
        pstart, block_expert, n_valid, n_used = _block_tables(counts, nb)
        pk = jnp.concatenate([pkp[:TOP_K], pks[:TOP_K]], axis=1)
        hit = (pk & (N_EXPERTS - 1))[:, :, None] == expert_ids
        slots = (jnp.sum(jnp.where(hit, pstart, 0), axis=-1) + (pk >> EXPERT_BITS)).reshape(-1)
        x_sorted = _dispatch(slots, h2p, h2s, nb * BM)
        y_sorted = _experts(l, x_sorted, block_expert, n_valid, n_used, w_up, b_up, w_down, b_down)

        final = l == DEPTH - 1
        xp = _combine(slots, y_sorted, x1p, twp, mod_p[l][:, 5:6, :], norm_final,
                      rows=TT, tok0=0, gate_spec=gate_p, gate_copies=1, final=final,
                      name=f"combine_prompt_l{l}")
        xs = _combine(slots, y_sorted, x1s, tws, mod_s[l][5], norm_final,
                      rows=dec_seq * SB, tok0=n_p, gate_spec=gate_s, gate_copies=dec_seq, final=final,
                      name=f"combine_sample_l{l}")

    y_prompt = xp.reshape(batch, seq, D_MODEL)
    y_sample = from_sample_order(xs.reshape(n_blocks, dec_seq, SB, D_MODEL))
    return (y_prompt, y_sample, jnp.stack(new_conv_p), jnp.stack(new_pool_p),
            jnp.stack(new_conv_s), jnp.stack(new_pool_s))
```

```python
import functools

import jax
import jax.numpy as jnp
from jax.experimental import pallas as pl
from jax.experimental.pallas import tpu as pltpu

F32 = jnp.float32
BF16 = jnp.bfloat16
I32 = jnp.int32

D_MODEL = 1024
DEPTH = 2
D_CONV = 512
CONV_WIDTH = 31
CONV_BUF = CONV_WIDTH - 1
D_POOL = 512
POOL_WINDOWS = (2, 4, 8, 16)
POOL_GROUP_IN = D_POOL // len(POOL_WINDOWS)
POOL_GROUP_OUT = D_MODEL // len(POOL_WINDOWS)
POOL_BUF = max(POOL_WINDOWS) - 1
IN_COLS = 2 * D_CONV + D_POOL + 2 * D_MODEL
N_EXPERTS = 32
TOP_K = 4
D_EXPERT = D_MODEL
SWIGLU_LIMIT = 7.0
SWIGLU_ALPHA = 1.702
N_MOD = 6
EPS = 1e-6
PAST_LEN = 16384

LANES = 128
SUBLANES = 8
VMEM_LIMIT = 52 * 1024 * 1024

TT = 256
NSEQ = 2
CONV_HALO = 32
POOL_HALO = 16
CONV_ROWS = 64
SB = 32
BM = 512
ADA_NT = 1536
CH = 512
ROW_TILE = D_MODEL // LANES
EXPERT_BITS = 5
CONV_SHIFT_ROWS = TT + CONV_HALO - SUBLANES
assert N_EXPERTS == 1 << EXPERT_BITS and ROW_TILE == SUBLANES


def _rms(x, g):
    return x * jax.lax.rsqrt(jnp.mean(x * x, axis=-1, keepdims=True) + EPS) * g


def _dot(a, b):
    return jnp.dot(a, b, preferred_element_type=F32)


def _store_token_tiles(ref, v):
    rows = v.shape[0]
    for s in range(ROW_TILE):
        ref[pl.ds(s, rows, stride=ROW_TILE), :] = v[:, s * LANES:(s + 1) * LANES]


def _load_token_tiles(ref, first_token, rows):
    return jnp.concatenate(
        [ref[pl.ds(first_token * ROW_TILE + s, rows, stride=ROW_TILE), :] for s in range(ROW_TILE)], axis=-1)


def _token_tile(ref, t):
    return ref.at[pl.ds(pl.multiple_of(t * ROW_TILE, ROW_TILE), ROW_TILE), :]


def _adaln_kernel(c_ref, w_ref, b_ref, o_ref):
    c = c_ref[...]
    a = (c * jax.nn.sigmoid(c)).astype(BF16)
    o_ref[...] = _dot(a, w_ref[...].astype(BF16)) + b_ref[...]


def _adaln(c_all, w_ada, b_ada):
    n = c_all.shape[0]
    cols = N_MOD * D_MODEL
    return pl.pallas_call(
        _adaln_kernel,
        grid=(DEPTH, cols // ADA_NT),
        in_specs=[
            pl.BlockSpec((n, D_MODEL), lambda l, j: (0, 0)),
            pl.BlockSpec((None, D_MODEL, ADA_NT), lambda l, j: (l, 0, j)),
            pl.BlockSpec((None, 1, ADA_NT), lambda l, j: (l, 0, j)),
        ],
        out_specs=pl.BlockSpec((None, n, ADA_NT), lambda l, j: (l, 0, j)),
        out_shape=jax.ShapeDtypeStruct((DEPTH, n, cols), F32),
        compiler_params=pltpu.CompilerParams(
            dimension_semantics=("arbitrary", "arbitrary"), vmem_limit_bytes=VMEM_LIMIT),
        name="adaln",
    )(c_all, w_ada, b_ada.reshape(DEPTH, 1, cols))


def _in_proj(x, sh1, sc1, nmix_ref, win_ref, bin_ref):
    h = (_rms(x, nmix_ref[...]) * (1.0 + sc1) + sh1).astype(BF16)
    c0, c1, c2 = 2 * D_CONV, 2 * D_CONV + D_POOL, IN_COLS
    zu = _dot(h, win_ref[:, 0:c0]) + bin_ref[:, 0:c0]
    u = zu[:, :D_CONV] * jax.nn.sigmoid(zu[:, D_CONV:])
    up = _dot(h, win_ref[:, c0:c1]) + bin_ref[:, c0:c1]
    zg = _dot(h, win_ref[:, c1:c2]) + bin_ref[:, c1:c2]
    return u, up, zg[:, :D_MODEL], zg[:, D_MODEL:]


def _conv_act(acc, cb_ref, lng_ref, lnb_ref):
    v = acc + cb_ref[...]
    mu = jnp.mean(v, axis=-1, keepdims=True)
    d = v - mu
    var = jnp.mean(d * d, axis=-1, keepdims=True)
    vn = d * jax.lax.rsqrt(var + EPS) * lng_ref[...] + lnb_ref[...]
    return vn * jax.nn.sigmoid(vn)


def _merge_and_route(x, v_bf, pooled, gc, gp, gt1, sh2, sc2, carry, w):
    rows = x.shape[0]
    y_conv = _dot(v_bf, w["wco"][...])
    y_pool = jnp.concatenate(
        [_dot(pooled[g].astype(BF16), w["wpool"][g]) for g in range(len(POOL_WINDOWS))], axis=-1)
    y_pool = y_pool * w["pscale"][...]
    m = jax.nn.sigmoid(gc) * y_conv + jax.nn.sigmoid(gp) * y_pool
    x1 = x + gt1 * _dot(m.astype(BF16), w["wout"][...])
    h2 = _rms(x1, w["nffn"][...]) * (1.0 + sc2) + sh2
    logits = _dot(h2.astype(BF16), w["wr"][...]) + w["br"][...]
    lane = jax.lax.broadcasted_iota(I32, (rows, LANES), 1)
    lane_f = lane.astype(F32)
    neg = jnp.full((rows, LANES), -jnp.inf, F32)
    l = jnp.where(lane < N_EXPERTS, logits, neg)
    vals, idxs, sels = [], [], []
    for _ in range(TOP_K):
        mx = jnp.max(l, axis=-1, keepdims=True)
        ix = jnp.min(jnp.where(l == mx, lane_f, float(LANES)), axis=-1, keepdims=True)
        sel = lane_f == ix
        l = jnp.where(sel, neg, l)
        vals.append(mx)
        idxs.append(ix)
        sels.append(sel)
    es = [jnp.exp(v - vals[0]) for v in vals]
    den = es[0] + es[1] + es[2] + es[3]

    onehot = jnp.where(sels[0] | sels[1] | sels[2] | sels[3], 1.0, 0.0)
    r_i = jax.lax.broadcasted_iota(I32, (rows, rows), 0)
    c_i = jax.lax.broadcasted_iota(I32, (rows, rows), 1)
    before = jnp.where(c_i < r_i, 1.0, 0.0).astype(BF16)
    base = _dot(before, onehot.astype(BF16)) + carry
    new_carry = carry + jnp.sum(onehot, axis=0, keepdims=True)

    tw = jnp.zeros((rows, LANES), F32)
    pk = jnp.zeros((rows, LANES), F32)
    for k in range(TOP_K):
        rank = jnp.sum(jnp.where(sels[k], base, 0.0), axis=-1, keepdims=True)
        tw = jnp.where(lane == k, es[k] / den, tw)
        pk = jnp.where(lane == k, rank * float(N_EXPERTS) + idxs[k], pk)
    pk8 = jnp.transpose(pk)[0:SUBLANES, :].astype(I32)
    return x1, h2, tw, pk8, new_carry


_WEIGHT_NAMES = ("nmix", "win", "bin", "cw", "cb", "lng", "lnb", "wco", "wpool", "pscale", "wout",
                 "nffn", "wr", "br")


def _weight_specs(l, n_grid):
    def spec(*shape):
        zeros = (0,) * len(shape)
        if n_grid == 2:
            return pl.BlockSpec((None,) + shape, lambda b, t: (l,) + zeros)
        return pl.BlockSpec((None,) + shape, lambda i: (l,) + zeros)
    return [
        spec(1, D_MODEL),
        spec(D_MODEL, IN_COLS),
        spec(1, IN_COLS),
        spec(CONV_WIDTH, D_CONV),
        spec(1, D_CONV),
        spec(1, D_CONV),
        spec(1, D_CONV),
        spec(D_CONV, D_MODEL),
        spec(len(POOL_WINDOWS), POOL_GROUP_IN, POOL_GROUP_OUT),
        spec(1, D_MODEL),
        spec(D_MODEL, D_MODEL),
        spec(1, D_MODEL),
        spec(D_MODEL, LANES),
        spec(1, LANES),
    ]


def _mixer_out_shapes(n):
    return (
        jax.ShapeDtypeStruct((n, D_MODEL), F32),
        jax.ShapeDtypeStruct((n * ROW_TILE, LANES), F32),
        jax.ShapeDtypeStruct((n, LANES), F32),
        jax.ShapeDtypeStruct((SUBLANES, n), I32),
        jax.ShapeDtypeStruct((SUBLANES, LANES), F32),
    )


def _prompt_mixer_kernel(x_ref, mod_ref, *refs):
    w = dict(zip(_WEIGHT_NAMES, refs[:len(_WEIGHT_NAMES)]))
    (x1_ref, h2_ref, tw_ref, pk_ref, cnt_ref, nconv_ref, npool_ref,
     uhist, ushift, phist, vbuf, carry) = refs[len(_WEIGHT_NAMES):]

    b = pl.program_id(0)
    t = pl.program_id(1)
    nt = pl.num_programs(1)

    @pl.when((b == 0) & (t == 0))
    def _():
        carry[...] = jnp.zeros((SUBLANES, LANES), F32)

    @pl.when(t == 0)
    def _():
        for q in range(NSEQ):
            uhist[q, 0:CONV_HALO, :] = jnp.zeros((CONV_HALO, D_CONV), F32)
            phist[q, 0:POOL_HALO, :] = jnp.zeros((POOL_HALO, D_POOL), F32)

    gates = []
    for q in range(NSEQ):
        u, up, gc, gp = _in_proj(x_ref[q], mod_ref[q, 0:1, :], mod_ref[q, 1:2, :],
                                 w["nmix"], w["win"], w["bin"])
        uhist[q, CONV_HALO:CONV_HALO + TT, :] = u
        phist[q, POOL_HALO:POOL_HALO + TT, :] = up
        gates.append((gc, gp))

    pooled_all = []
    for q in range(NSEQ):
        for r in range(1, SUBLANES):
            ushift[q, r - 1] = uhist[q, r:r + CONV_SHIFT_ROWS, :]
        for c in range(TT // CONV_ROWS):
            acc = jnp.zeros((CONV_ROWS, D_CONV), F32)
            for k in range(CONV_WIDTH):
                qq, r = divmod(CONV_HALO - CONV_BUF + k, SUBLANES)
                start = qq * SUBLANES + c * CONV_ROWS
                if r == 0:
                    tap = uhist[q, start:start + CONV_ROWS, :]
                else:
                    tap = ushift[q, r - 1, start:start + CONV_ROWS, :]
                acc = acc + w["cw"][k:k + 1, :] * tap
            s = _conv_act(acc, w["cb"], w["lng"], w["lnb"])
            vbuf[q, c * CONV_ROWS:(c + 1) * CONV_ROWS, :] = s.astype(BF16)

        pos = t * TT + jax.lax.broadcasted_iota(I32, (TT, 1), 0)
        pooled = []
        for g, win in enumerate(POOL_WINDOWS):
            lo, hi = g * POOL_GROUP_IN, (g + 1) * POOL_GROUP_IN
            cur = phist[q, POOL_HALO:POOL_HALO + TT, lo:hi]
            ssum = cur
            for i in range(1, win):
                ssum = ssum + phist[q, POOL_HALO - i:POOL_HALO - i + TT, lo:hi]
            cnt = jnp.minimum(pos + 1, win).astype(F32)
            pooled.append(ssum / cnt - cur)
        pooled_all.append(pooled)

    cur_carry = carry[0:1, :]
    for q in range(NSEQ):
        gc, gp = gates[q]
        x1, h2, tw, pk8, cur_carry = _merge_and_route(
            x_ref[q], vbuf[q], pooled_all[q], gc, gp, mod_ref[q, 2:3, :], mod_ref[q, 3:4, :],
            mod_ref[q, 4:5, :], cur_carry, w)
        x1_ref[q] = x1
        _store_token_tiles(h2_ref.at[q], h2)
        tw_ref[q] = tw
        pk_ref[q] = pk8

        uhist[q, 0:CONV_HALO, :] = uhist[q, TT:TT + CONV_HALO, :]
        phist[q, 0:POOL_HALO, :] = phist[q, TT:TT + POOL_HALO, :]

    carry[...] = jnp.broadcast_to(cur_carry, (SUBLANES, LANES))
    cnt_ref[...] = jnp.broadcast_to(cur_carry, (SUBLANES, LANES))

    @pl.when(t == nt - 1)
    def _():
        for q in range(NSEQ):
            nconv_ref[q] = uhist[q, CONV_HALO + TT - CONV_BUF:CONV_HALO + TT, :]
            npool_ref[q] = phist[q, POOL_HALO + TT - POOL_BUF:POOL_HALO + TT, :]


def _prompt_mixer(l, x, mod, weights, batch, seq):
    nt = seq // TT
    n_tok = batch * seq
    half = n_tok // NSEQ
    row = lambda rows, width: pl.BlockSpec((NSEQ, rows, width), lambda b, t: (0, b * nt + t, 0))
    in_specs = [row(TT, D_MODEL), pl.BlockSpec((NSEQ, None, N_MOD, D_MODEL), lambda b, t: (0, b, 0, 0))]
    in_specs += _weight_specs(l, 2)
    out_shape = (
        jax.ShapeDtypeStruct((NSEQ, half, D_MODEL), F32),
        jax.ShapeDtypeStruct((NSEQ, half * ROW_TILE, LANES), F32),
        jax.ShapeDtypeStruct((NSEQ, half, LANES), F32),
        jax.ShapeDtypeStruct((NSEQ, SUBLANES, half), I32),
        jax.ShapeDtypeStruct((SUBLANES, LANES), F32),
        jax.ShapeDtypeStruct((NSEQ, batch // NSEQ, CONV_BUF, D_CONV), F32),
        jax.ShapeDtypeStruct((NSEQ, batch // NSEQ, POOL_BUF, D_POOL), F32),
    )
    out_specs = (
        row(TT, D_MODEL), row(TT * ROW_TILE, LANES), row(TT, LANES),
        pl.BlockSpec((NSEQ, SUBLANES, TT), lambda b, t: (0, 0, b * nt + t)),
        pl.BlockSpec((SUBLANES, LANES), lambda b, t: (0, 0)),
        pl.BlockSpec((NSEQ, None, CONV_BUF, D_CONV), lambda b, t: (0, b, 0, 0)),
        pl.BlockSpec((NSEQ, None, POOL_BUF, D_POOL), lambda b, t: (0, b, 0, 0)),
    )
    x1, h2, tw, pk8, cnt, nconv, npool = pl.pallas_call(
        _prompt_mixer_kernel,
        grid=(batch // NSEQ, nt),
        in_specs=in_specs,
        out_specs=out_specs,
        out_shape=out_shape,
        scratch_shapes=[
            pltpu.VMEM((NSEQ, CONV_HALO + TT, D_CONV), F32),
            pltpu.VMEM((NSEQ, SUBLANES - 1, CONV_SHIFT_ROWS, D_CONV), F32),
            pltpu.VMEM((NSEQ, POOL_HALO + TT, D_POOL), F32),
            pltpu.VMEM((NSEQ, TT, D_CONV), BF16),
            pltpu.VMEM((SUBLANES, LANES), F32),
        ],
        compiler_params=pltpu.CompilerParams(
            dimension_semantics=("arbitrary", "arbitrary"), vmem_limit_bytes=VMEM_LIMIT),
        name=f"prompt_mixer_l{l}",
    )(x.reshape(NSEQ, half, D_MODEL), mod.reshape(NSEQ, batch // NSEQ, N_MOD, D_MODEL), *weights)
    return (x1.reshape(n_tok, D_MODEL), h2.reshape(n_tok * ROW_TILE, LANES), tw.reshape(n_tok, LANES),
            jnp.transpose(pk8, (1, 0, 2)).reshape(SUBLANES, n_tok), cnt,
            nconv.reshape(batch, CONV_BUF, D_CONV), npool.reshape(batch, POOL_BUF, D_POOL))


def _sample_mixer_kernel(x_ref, mod_ref, cs_ref, ps_ref, cnt0_ref, *refs, dec_seq, pos0):
    w = dict(zip(_WEIGHT_NAMES, refs[:len(_WEIGHT_NAMES)]))
    (x1_ref, h2_ref, tw_ref, pk_ref, cnt_ref, nconv_ref, npool_ref,
     ufull, pfull, vbuf, carry) = refs[len(_WEIGHT_NAMES):]

    def per_row(v):
        return jnp.concatenate([v] * dec_seq, axis=0)

    @pl.when(pl.program_id(0) == 0)
    def _():
        carry[...] = cnt0_ref[...]

    x = x_ref[...]
    sh1, sc1, gt1 = per_row(mod_ref[0]), per_row(mod_ref[1]), per_row(mod_ref[2])
    sh2, sc2 = per_row(mod_ref[3]), per_row(mod_ref[4])

    u, up, gc, gp = _in_proj(x, sh1, sc1, w["nmix"], w["win"], w["bin"])
    ufull[0:CONV_BUF] = cs_ref[...]
    pfull[0:POOL_BUF] = ps_ref[...]
    for j in range(dec_seq):
        ufull[CONV_BUF + j] = u[j * SB:(j + 1) * SB, :]
        pfull[POOL_BUF + j] = up[j * SB:(j + 1) * SB, :]
    nconv_ref[...] = ufull[dec_seq:dec_seq + CONV_BUF]
    npool_ref[...] = pfull[dec_seq:dec_seq + POOL_BUF]

    for j in range(dec_seq):
        acc = jnp.zeros((SB, D_CONV), F32)
        for k in range(CONV_WIDTH):
            acc = acc + w["cw"][k:k + 1, :] * ufull[j + k]
        s = _conv_act(acc, w["cb"], w["lng"], w["lnb"])
        vbuf[j * SB:(j + 1) * SB, :] = s.astype(BF16)

    pooled = []
    for g, win in enumerate(POOL_WINDOWS):
        lo, hi = g * POOL_GROUP_IN, (g + 1) * POOL_GROUP_IN
        parts = []
        for j in range(dec_seq):
            cur = pfull[POOL_BUF + j, :, lo:hi]
            ssum = cur
            for i in range(1, win):
                ssum = ssum + pfull[POOL_BUF + j - i, :, lo:hi]
            cnt = float(min(pos0 + j + 1, win))
            parts.append(ssum / cnt - cur)
        pooled.append(jnp.concatenate(parts, axis=0))

    x1, h2, tw, pk8, new_carry = _merge_and_route(
        x, vbuf[...], pooled, gc, gp, gt1, sh2, sc2, carry[0:1, :], w)
    x1_ref[...] = x1
    _store_token_tiles(h2_ref, h2)
    tw_ref[...] = tw
    pk_ref[...] = pk8
    carry[...] = jnp.broadcast_to(new_carry, (SUBLANES, LANES))
    cnt_ref[...] = jnp.broadcast_to(new_carry, (SUBLANES, LANES))


def _sample_mixer(l, x, mod, conv_state, pool_state, cnt0, weights, dec_batch, dec_seq, pos0):
    rows = dec_seq * SB
    n_s = dec_batch * dec_seq
    row = lambda width: pl.BlockSpec((rows, width), lambda i: (i, 0))
    conv_spec = pl.BlockSpec((None, CONV_BUF, SB, D_CONV), lambda i: (i, 0, 0, 0))
    pool_spec = pl.BlockSpec((None, POOL_BUF, SB, D_POOL), lambda i: (i, 0, 0, 0))
    cnt_spec = pl.BlockSpec((SUBLANES, LANES), lambda i: (0, 0))
    in_specs = [row(D_MODEL), pl.BlockSpec((N_MOD, SB, D_MODEL), lambda i: (0, i, 0)),
                conv_spec, pool_spec, cnt_spec]
    in_specs += _weight_specs(l, 1)
    out_shape = _mixer_out_shapes(n_s) + (
        jax.ShapeDtypeStruct(conv_state.shape, F32),
        jax.ShapeDtypeStruct(pool_state.shape, F32),
    )
    out_specs = (row(D_MODEL), pl.BlockSpec((rows * ROW_TILE, LANES), lambda i: (i, 0)), row(LANES),
                 pl.BlockSpec((SUBLANES, rows), lambda i: (0, i)), cnt_spec, conv_spec, pool_spec)
    return pl.pallas_call(
        functools.partial(_sample_mixer_kernel, dec_seq=dec_seq, pos0=pos0),
        grid=(dec_batch // SB,),
        in_specs=in_specs,
        out_specs=out_specs,
        out_shape=out_shape,
        scratch_shapes=[
            pltpu.VMEM((CONV_BUF + dec_seq, SB, D_CONV), F32),
            pltpu.VMEM((POOL_BUF + dec_seq, SB, D_POOL), F32),
            pltpu.VMEM((rows, D_CONV), BF16),
            pltpu.VMEM((SUBLANES, LANES), F32),
        ],
        compiler_params=pltpu.CompilerParams(
            dimension_semantics=("arbitrary",), vmem_limit_bytes=VMEM_LIMIT),
        name=f"sample_mixer_l{l}",
    )(x, mod, conv_state, pool_state, cnt0, *weights)


def _fetch_slots(slots_hbm, n_tok, first_token, count, idx, first_idx, sem):
    copies = [
        pltpu.make_async_copy(slots_hbm.at[pl.ds(k * n_tok + first_token, count)],
                              idx.at[pl.ds(first_idx + k * count, count)], sem)
        for k in range(TOP_K)]
    for cp in copies:
        cp.start()
    for cp in copies:
        cp.wait()


def _dispatch_kernel(slots_hbm, h2p_ref, h2s_ref, xs_hbm, idx, isem, rsem, *, n_prompt_steps, n_tok):
    c = pl.program_id(0)
    _fetch_slots(slots_hbm, n_tok, c * CH, CH, idx, 0, isem)

    def scatter_rows(src_ref):
        def body(t, carry):
            for k in range(TOP_K):
                pltpu.make_async_copy(_token_tile(src_ref, t), _token_tile(xs_hbm, idx[k * CH + t]),
                                      rsem).start(priority=k % 2)
            return carry
        jax.lax.fori_loop(0, CH, body, 0, unroll=4)
        for _ in range(TOP_K):
            pltpu.make_async_copy(src_ref, xs_hbm.at[pl.ds(0, CH * ROW_TILE), :], rsem).wait()

    @pl.when(c < n_prompt_steps)
    def _():
        scatter_rows(h2p_ref)

    @pl.when(c >= n_prompt_steps)
    def _():
        scatter_rows(h2s_ref)


def _dispatch(slots, h2p, h2s, n_slots):
    n_p, n_s = h2p.shape[0] // ROW_TILE, h2s.shape[0] // ROW_TILE
    np_steps, ns_steps = n_p // CH, n_s // CH
    return pl.pallas_call(
        functools.partial(_dispatch_kernel, n_prompt_steps=np_steps, n_tok=n_p + n_s),
        grid=(np_steps + ns_steps,),
        in_specs=[
            pl.BlockSpec(memory_space=pl.ANY),
            pl.BlockSpec((CH * ROW_TILE, LANES), lambda c: (jnp.minimum(c, np_steps - 1), 0)),
            pl.BlockSpec((CH * ROW_TILE, LANES), lambda c: (jnp.maximum(c - np_steps, 0), 0)),
        ],
        out_specs=pl.BlockSpec(memory_space=pl.ANY),
        scratch_shapes=[
            pltpu.SMEM((TOP_K * CH,), I32),
            pltpu.SemaphoreType.DMA,
            pltpu.SemaphoreType.DMA,
        ],
        out_shape=jax.ShapeDtypeStruct((n_slots * ROW_TILE, LANES), F32),
        compiler_params=pltpu.CompilerParams(
            dimension_semantics=("arbitrary",), vmem_limit_bytes=VMEM_LIMIT),
        name="dispatch",
    )(slots, h2p, h2s)


def _experts_kernel(be_ref, nv_ref, nu_ref, nx_ref, x_ref, wup_hbm, bup_ref, wdn_hbm, bdn_ref, o_ref,
                    wup_f32, wdn_f32, wup_bf, wdn_bf, wsem, *, layer):
    b = pl.program_id(0)

    def weight_copies(e):
        return (pltpu.make_async_copy(wup_hbm.at[layer, e], wup_f32, wsem.at[0]),
                pltpu.make_async_copy(wdn_hbm.at[layer, e], wdn_f32, wsem.at[1]))

    @pl.when(b < nu_ref[0])
    def _():
        e = be_ref[b]
        prev = be_ref[jnp.maximum(b - 1, 0)]

        @pl.when((b == 0) | (e != prev))
        def _():
            @pl.when(b == 0)
            def _():
                for cp in weight_copies(e):
                    cp.start()
            for cp in weight_copies(e):
                cp.wait()
            for r in range(0, D_MODEL, LANES):
                wup_bf[r:r + LANES, :] = wup_f32[r:r + LANES, :].astype(BF16)
                wdn_bf[r:r + LANES, :] = wdn_f32[r:r + LANES, :].astype(BF16)
            nxt = nx_ref[e]

            @pl.when(nxt >= 0)
            def _():
                for cp in weight_copies(nxt):
                    cp.start()

        rid = jax.lax.broadcasted_iota(I32, (BM, D_MODEL), 0)
        x = jnp.where(rid < nv_ref[b], _load_token_tiles(x_ref, 0, BM), 0.0)
        a = _dot(x.astype(BF16), wup_bf[...]) + bup_ref[...]
        a_glu = jnp.minimum(a[:, :D_EXPERT], SWIGLU_LIMIT)
        a_lin = jnp.clip(a[:, D_EXPERT:], -SWIGLU_LIMIT, SWIGLU_LIMIT)
        o = a_glu * jax.nn.sigmoid(SWIGLU_ALPHA * a_glu) * (a_lin + 1.0)
        _store_token_tiles(o_ref, _dot(o.astype(BF16), wdn_bf[...]) + bdn_ref[...])

    @pl.when(b >= nu_ref[0])
    def _():
        o_ref[...] = jnp.zeros((BM * ROW_TILE, LANES), F32)


def _experts(l, x_sorted, block_expert, n_valid, n_used, next_expert, w_up, b_up, w_down, b_down):
    n_slots = x_sorted.shape[0] // ROW_TILE
    nb = n_slots // BM

    def used_map(b, be, nv, nu, nx):
        return (jnp.minimum(b, nu[0] - 1), 0)

    bmap = lambda b, be, nv, nu, nx: (l, be[b], 0, 0)
    grid_spec = pltpu.PrefetchScalarGridSpec(
        num_scalar_prefetch=4,
        grid=(nb,),
        in_specs=[
            pl.BlockSpec((BM * ROW_TILE, LANES), used_map),
            pl.BlockSpec(memory_space=pl.ANY),
            pl.BlockSpec((None, None, 1, 2 * D_EXPERT), bmap),
            pl.BlockSpec(memory_space=pl.ANY),
            pl.BlockSpec((None, None, 1, D_MODEL), bmap),
        ],
        out_specs=pl.BlockSpec((BM * ROW_TILE, LANES), lambda b, be, nv, nu, nx: (b, 0)),
        scratch_shapes=[
            pltpu.VMEM((D_MODEL, 2 * D_EXPERT), F32),
            pltpu.VMEM((D_EXPERT, D_MODEL), F32),
            pltpu.VMEM((D_MODEL, 2 * D_EXPERT), BF16),
            pltpu.VMEM((D_EXPERT, D_MODEL), BF16),
            pltpu.SemaphoreType.DMA((2,)),
        ],
    )
    return pl.pallas_call(
        functools.partial(_experts_kernel, layer=l),
        grid_spec=grid_spec,
        out_shape=jax.ShapeDtypeStruct((n_slots * ROW_TILE, LANES), F32),
        compiler_params=pltpu.CompilerParams(
            dimension_semantics=("arbitrary",), vmem_limit_bytes=VMEM_LIMIT),
        name=f"experts_l{l}",
    )(block_expert, n_valid, n_used, next_expert, x_sorted, w_up,
      b_up.reshape(DEPTH, N_EXPERTS, 1, 2 * D_EXPERT), w_down,
      b_down.reshape(DEPTH, N_EXPERTS, 1, D_MODEL))


def _block_tables(counts, nb):
    nblk = (counts + BM - 1) // BM
    blk_end = jnp.cumsum(nblk)
    blk_start = blk_end - nblk
    bidx = jnp.arange(nb, dtype=I32)
    block_expert = jnp.minimum(
        jnp.sum((blk_end[None, :] <= bidx[:, None]).astype(I32), axis=1), N_EXPERTS - 1)
    n_valid = jnp.clip(counts[block_expert] - (bidx - blk_start[block_expert]) * BM, 0, BM)
    eid = jnp.arange(N_EXPERTS, dtype=I32)
    later = (eid[None, :] > eid[:, None]) & (nblk[None, :] > 0)
    next_expert = jnp.min(jnp.where(later, eid[None, :], N_EXPERTS), axis=1)
    next_expert = jnp.where(next_expert == N_EXPERTS, -1, next_expert)
    return ((blk_start * BM).astype(I32), block_expert.astype(I32), n_valid.astype(I32),
            blk_end[-1:].astype(I32), next_expert.astype(I32))


def _combine_kernel(slots_hbm, y_hbm, x1_ref, tw_ref, gt_ref, g_ref, o_ref, ybuf0, ybuf1, idx,
                    isem, rsem, *, rows, tok0, n_tok, gate_copies, final):
    i = pl.program_id(0)
    n = pl.num_programs(0)
    ybuf = (ybuf0, ybuf1)

    def fetch_indices(step, slot):
        _fetch_slots(slots_hbm, n_tok, tok0 + step * rows, rows, idx, slot * TOP_K * rows, isem.at[slot])

    def gather_rows(slot):
        def body(t, carry):
            for k in range(TOP_K):
                d = idx[(slot * TOP_K + k) * rows + t]
                pltpu.make_async_copy(_token_tile(y_hbm, d), _token_tile(ybuf[slot], k * rows + t),
                                      rsem.at[slot]).start(priority=k % 2)
            return carry
        jax.lax.fori_loop(0, rows, body, 0, unroll=4)

    @pl.when(i == 0)
    def _():
        fetch_indices(0, 0)
        gather_rows(0)

    for s in range(2):
        @pl.when((i % 2 == s) & (i + 1 < n))
        def _():
            fetch_indices(i + 1, 1 - s)
            gather_rows(1 - s)

    for s in range(2):
        @pl.when(i % 2 == s)
        def _():
            pltpu.make_async_copy(y_hbm.at[pl.ds(0, TOP_K * rows * ROW_TILE), :], ybuf[s], rsem.at[s]).wait()
            tw = tw_ref[...]
            moe = tw[:, 0:1] * _load_token_tiles(ybuf[s], 0, rows)
            for k in range(1, TOP_K):
                moe = moe + tw[:, k:k + 1] * _load_token_tiles(ybuf[s], k * rows, rows)
            gt = gt_ref[...]
            if gate_copies > 1:
                gt = jnp.concatenate([gt] * gate_copies, axis=0)
            x = x1_ref[...] + gt * moe
            o_ref[...] = _rms(x, g_ref[...]) if final else x


def _combine(slots, y_sorted, x1, tw, gate, norm_final, *, rows, tok0, gate_spec, gate_copies,
             final, name):
    n = x1.shape[0]
    row = lambda width: pl.BlockSpec((rows, width), lambda i: (i, 0))
    return pl.pallas_call(
        functools.partial(_combine_kernel, rows=rows, tok0=tok0, n_tok=slots.shape[0] // TOP_K,
                          gate_copies=gate_copies, final=final),
        grid=(n // rows,),
        in_specs=[
            pl.BlockSpec(memory_space=pl.ANY),
            pl.BlockSpec(memory_space=pl.ANY),
            row(D_MODEL), row(LANES), gate_spec,
            pl.BlockSpec((1, D_MODEL), lambda i: (0, 0)),
        ],
        out_specs=row(D_MODEL),
        scratch_shapes=[
            pltpu.VMEM((TOP_K * rows * ROW_TILE, LANES), F32),
            pltpu.VMEM((TOP_K * rows * ROW_TILE, LANES), F32),
            pltpu.SMEM((2 * TOP_K * rows,), I32),
            pltpu.SemaphoreType.DMA((2,)),
            pltpu.SemaphoreType.DMA((2,)),
        ],
        out_shape=jax.ShapeDtypeStruct((n, D_MODEL), F32),
        compiler_params=pltpu.CompilerParams(
            dimension_semantics=("arbitrary",), vmem_limit_bytes=VMEM_LIMIT),
        name=name,
    )(slots, y_sorted, x1, tw, gate, norm_final.reshape(1, D_MODEL))


def kernel(x_prompt, x_sample, state_conv, state_pool, c_prompt, c_sample, w_ada, b_ada, norm_mix, w_in, b_in, conv_w, conv_b, conv_ln_g, conv_ln_b, w_conv_out, w_pool, pool_scale, w_out, norm_ffn, w_router, b_router, w_up, b_up, w_down, b_down, norm_final):
    batch, seq, _ = x_prompt.shape
    dec_batch, dec_seq, _ = x_sample.shape
    n_p = batch * seq
    n_s = dec_batch * dec_seq
    n_blocks = dec_batch // SB
    n_assign = (n_p + n_s) * TOP_K
    nb = n_assign // BM + N_EXPERTS

    vec = lambda a: a.reshape(DEPTH, 1, a.shape[-1])
    weights = (
        vec(norm_mix), w_in.astype(BF16), vec(b_in), conv_w, vec(conv_b), vec(conv_ln_g),
        vec(conv_ln_b), w_conv_out.astype(BF16), w_pool.astype(BF16), vec(pool_scale),
        w_out.astype(BF16), vec(norm_ffn),
        jnp.pad(w_router, ((0, 0), (0, 0), (0, LANES - N_EXPERTS))).astype(BF16),
        vec(jnp.pad(b_router, ((0, 0), (0, LANES - N_EXPERTS)))),
    )

    mod = _adaln(jnp.concatenate([c_prompt, c_sample], axis=0), w_ada, b_ada)
    mod = mod.reshape(DEPTH, batch + dec_batch, N_MOD, D_MODEL)
    mod_p = mod[:, :batch]
    mod_s = jnp.transpose(mod[:, batch:], (0, 2, 1, 3))

    def to_sample_order(a, rows):
        return jnp.transpose(a.reshape(n_blocks, SB, rows, a.shape[-1]), (0, 2, 1, 3))

    def from_sample_order(a):
        return jnp.transpose(a, (0, 2, 1, 3)).reshape(dec_batch, a.shape[1], a.shape[-1])

    xp = x_prompt.reshape(n_p, D_MODEL)
    xs = to_sample_order(x_sample, dec_seq).reshape(n_s, D_MODEL)

    gate_p = pl.BlockSpec((None, 1, D_MODEL), lambda i: (i // (seq // TT), 0, 0))
    gate_s = pl.BlockSpec((SB, D_MODEL), lambda i: (i, 0))
    expert_ids = jnp.arange(N_EXPERTS, dtype=I32)

    new_conv_p, new_pool_p, new_conv_s, new_pool_s = [], [], [], []
    for l in range(DEPTH):
        x1p, h2p, twp, pkp, cntp, ncp, npp = _prompt_mixer(l, xp, mod_p[l], weights, batch, seq)
        x1s, h2s, tws, pks, cnt, ncs, nps = _sample_mixer(
            l, xs, mod_s[l], to_sample_order(state_conv[l], CONV_BUF),
            to_sample_order(state_pool[l], POOL_BUF), cntp, weights, dec_batch, dec_seq, PAST_LEN)
        new_conv_p.append(ncp)
        new_pool_p.append(npp)
        new_conv_s.append(from_sample_order(ncs))
        new_pool_s.append(from_sample_order(nps))

        counts = cnt[0, :N_EXPERTS].astype(I32)
        pstart, block_expert, n_valid, n_used, next_expert = _block_tables(counts, nb)
        pk = jnp.concatenate([pkp[:TOP_K], pks[:TOP_K]], axis=1)
        hit = (pk & (N_EXPERTS - 1))[None] == expert_ids[:, None, None]
        slots = (jnp.sum(jnp.where(hit, pstart[:, None, None], 0), axis=0) + (pk >> EXPERT_BITS)).reshape(-1)
        x_sorted = _dispatch(slots, h2p, h2s, nb * BM)
        y_sorted = _experts(l, x_sorted, block_expert, n_valid, n_used, next_expert, w_up, b_up, w_down,
                            b_down)

        final = l == DEPTH - 1
        xp = _combine(slots, y_sorted, x1p, twp, mod_p[l][:, 5:6, :], norm_final,
                      rows=TT, tok0=0, gate_spec=gate_p, gate_copies=1, final=final,
                      name=f"combine_prompt_l{l}")
        xs = _combine(slots, y_sorted, x1s, tws, mod_s[l][5], norm_final,
                      rows=dec_seq * SB, tok0=n_p, gate_spec=gate_s, gate_copies=dec_seq, final=final,
                      name=f"combine_sample_l{l}")

    y_prompt = xp.reshape(batch, seq, D_MODEL)
    y_sample = from_sample_order(xs.reshape(n_blocks, dec_seq, SB, D_MODEL))
    return (y_prompt, y_sample, jnp.stack(new_conv_p), jnp.stack(new_pool_p),
            jnp.stack(new_conv_s), jnp.stack(new_pool_s))
```

```python
import functools

import jax
import jax.numpy as jnp
from jax.experimental import pallas as pl
from jax.experimental.pallas import tpu as pltpu

F32 = jnp.float32
BF16 = jnp.bfloat16
I32 = jnp.int32

D_MODEL = 1024
DEPTH = 2
D_CONV = 512
CONV_WIDTH = 31
CONV_BUF = CONV_WIDTH - 1
D_POOL = 512
POOL_WINDOWS = (2, 4, 8, 16)
POOL_GROUP_IN = D_POOL // len(POOL_WINDOWS)
POOL_GROUP_OUT = D_MODEL // len(POOL_WINDOWS)
POOL_BUF = max(POOL_WINDOWS) - 1
IN_COLS = 2 * D_CONV + D_POOL + 2 * D_MODEL
N_EXPERTS = 32
TOP_K = 4
D_EXPERT = D_MODEL
SWIGLU_LIMIT = 7.0
SWIGLU_ALPHA = 1.702
N_MOD = 6
EPS = 1e-6
PAST_LEN = 16384

LANES = 128
SUBLANES = 8
VMEM_LIMIT = 52 * 1024 * 1024

TT = 256
NSEQ = 2
CONV_HALO = 32
POOL_HALO = 16
CONV_ROWS = 64
SB = 32
BM = 512
ADA_NT = 1536
CH = 512
ROW_TILE = D_MODEL // LANES
EXPERT_BITS = 5
CONV_SHIFT_ROWS = TT + CONV_HALO - SUBLANES
assert N_EXPERTS == 1 << EXPERT_BITS and ROW_TILE == SUBLANES


def _rms(x, g):
    return x * jax.lax.rsqrt(jnp.mean(x * x, axis=-1, keepdims=True) + EPS) * g


def _dot(a, b):
    return jnp.dot(a, b, preferred_element_type=F32)


def _store_token_tiles(ref, v):
    rows = v.shape[0]
    for s in range(ROW_TILE):
        ref[pl.ds(s, rows, stride=ROW_TILE), :] = v[:, s * LANES:(s + 1) * LANES]


def _load_token_tiles(ref, first_token, rows):
    return jnp.concatenate(
        [ref[pl.ds(first_token * ROW_TILE + s, rows, stride=ROW_TILE), :] for s in range(ROW_TILE)], axis=-1)


def _token_tile(ref, t):
    return ref.at[pl.ds(pl.multiple_of(t * ROW_TILE, ROW_TILE), ROW_TILE), :]


def _adaln_kernel(c_ref, w_ref, b_ref, o_ref):
    c = c_ref[...]
    a = (c * jax.nn.sigmoid(c)).astype(BF16)
    o_ref[...] = _dot(a, w_ref[...].astype(BF16)) + b_ref[...]


def _adaln(c_all, w_ada, b_ada):
    n = c_all.shape[0]
    cols = N_MOD * D_MODEL
    return pl.pallas_call(
        _adaln_kernel,
        grid=(DEPTH, cols // ADA_NT),
        in_specs=[
            pl.BlockSpec((n, D_MODEL), lambda l, j: (0, 0)),
            pl.BlockSpec((None, D_MODEL, ADA_NT), lambda l, j: (l, 0, j)),
            pl.BlockSpec((None, 1, ADA_NT), lambda l, j: (l, 0, j)),
        ],
        out_specs=pl.BlockSpec((None, n, ADA_NT), lambda l, j: (l, 0, j)),
        out_shape=jax.ShapeDtypeStruct((DEPTH, n, cols), F32),
        compiler_params=pltpu.CompilerParams(
            dimension_semantics=("arbitrary", "arbitrary"), vmem_limit_bytes=VMEM_LIMIT),
        name="adaln",
    )(c_all, w_ada, b_ada.reshape(DEPTH, 1, cols))


def _in_proj(x, sh1, sc1, nmix_ref, win_ref, bin_ref):
    h = (_rms(x, nmix_ref[...]) * (1.0 + sc1) + sh1).astype(BF16)
    c0, c1, c2 = 2 * D_CONV, 2 * D_CONV + D_POOL, IN_COLS
    zu = _dot(h, win_ref[:, 0:c0]) + bin_ref[:, 0:c0]
    u = zu[:, :D_CONV] * jax.nn.sigmoid(zu[:, D_CONV:])
    up = _dot(h, win_ref[:, c0:c1]) + bin_ref[:, c0:c1]
    zg = _dot(h, win_ref[:, c1:c2]) + bin_ref[:, c1:c2]
    return u, up, zg[:, :D_MODEL], zg[:, D_MODEL:]


def _conv_act(acc, cb_ref, lng_ref, lnb_ref):
    v = acc + cb_ref[...]
    mu = jnp.mean(v, axis=-1, keepdims=True)
    d = v - mu
    var = jnp.mean(d * d, axis=-1, keepdims=True)
    vn = d * jax.lax.rsqrt(var + EPS) * lng_ref[...] + lnb_ref[...]
    return vn * jax.nn.sigmoid(vn)


def _merge_and_route(x, v_bf, pooled, gc, gp, gt1, sh2, sc2, carry, w):
    rows = x.shape[0]
    y_conv = _dot(v_bf, w["wco"][...])
    y_pool = jnp.concatenate(
        [_dot(pooled[g].astype(BF16), w["wpool"][g]) for g in range(len(POOL_WINDOWS))], axis=-1)
    y_pool = y_pool * w["pscale"][...]
    m = jax.nn.sigmoid(gc) * y_conv + jax.nn.sigmoid(gp) * y_pool
    x1 = x + gt1 * _dot(m.astype(BF16), w["wout"][...])
    h2 = _rms(x1, w["nffn"][...]) * (1.0 + sc2) + sh2
    logits = _dot(h2.astype(BF16), w["wr"][...]) + w["br"][...]
    lane = jax.lax.broadcasted_iota(I32, (rows, LANES), 1)
    lane_f = lane.astype(F32)
    neg = jnp.full((rows, LANES), -jnp.inf, F32)
    l = jnp.where(lane < N_EXPERTS, logits, neg)
    vals, idxs, sels = [], [], []
    for _ in range(TOP_K):
        mx = jnp.max(l, axis=-1, keepdims=True)
        ix = jnp.min(jnp.where(l == mx, lane_f, float(LANES)), axis=-1, keepdims=True)
        sel = lane_f == ix
        l = jnp.where(sel, neg, l)
        vals.append(mx)
        idxs.append(ix)
        sels.append(sel)
    es = [jnp.exp(v - vals[0]) for v in vals]
    den = es[0] + es[1] + es[2] + es[3]

    onehot = jnp.where(sels[0] | sels[1] | sels[2] | sels[3], 1.0, 0.0)
    r_i = jax.lax.broadcasted_iota(I32, (rows, rows), 0)
    c_i = jax.lax.broadcasted_iota(I32, (rows, rows), 1)
    before = jnp.where(c_i < r_i, 1.0, 0.0).astype(BF16)
    base = _dot(before, onehot.astype(BF16)) + carry
    new_carry = carry + jnp.sum(onehot, axis=0, keepdims=True)

    tw = jnp.zeros((rows, LANES), F32)
    pk = jnp.zeros((rows, LANES), F32)
    for k in range(TOP_K):
        rank = jnp.sum(jnp.where(sels[k], base, 0.0), axis=-1, keepdims=True)
        tw = jnp.where(lane == k, es[k] / den, tw)
        pk = jnp.where(lane == k, rank * float(N_EXPERTS) + idxs[k], pk)
    pk8 = jnp.transpose(pk)[0:SUBLANES, :].astype(I32)
    return x1, h2, tw, pk8, new_carry


_WEIGHT_NAMES = ("nmix", "win", "bin", "cw", "cb", "lng", "lnb", "wco", "wpool", "pscale", "wout",
                 "nffn", "wr", "br")


def _weight_specs(l, n_grid):
    def spec(*shape):
        zeros = (0,) * len(shape)
        if n_grid == 2:
            return pl.BlockSpec((None,) + shape, lambda b, t: (l,) + zeros)
        return pl.BlockSpec((None,) + shape, lambda i: (l,) + zeros)
    return [
        spec(1, D_MODEL),
        spec(D_MODEL, IN_COLS),
        spec(1, IN_COLS),
        spec(CONV_WIDTH, D_CONV),
        spec(1, D_CONV),
        spec(1, D_CONV),
        spec(1, D_CONV),
        spec(D_CONV, D_MODEL),
        spec(len(POOL_WINDOWS), POOL_GROUP_IN, POOL_GROUP_OUT),
        spec(1, D_MODEL),
        spec(D_MODEL, D_MODEL),
        spec(1, D_MODEL),
        spec(D_MODEL, LANES),
        spec(1, LANES),
    ]


def _mixer_out_shapes(n):
    return (
        jax.ShapeDtypeStruct((n, D_MODEL), F32),
        jax.ShapeDtypeStruct((n * ROW_TILE, LANES), F32),
        jax.ShapeDtypeStruct((n, LANES), F32),
        jax.ShapeDtypeStruct((SUBLANES, n), I32),
        jax.ShapeDtypeStruct((SUBLANES, LANES), F32),
    )


def _prompt_mixer_kernel(x_ref, mod_ref, *refs):
    w = dict(zip(_WEIGHT_NAMES, refs[:len(_WEIGHT_NAMES)]))
    (x1_ref, h2_ref, tw_ref, pk_ref, cnt_ref, nconv_ref, npool_ref,
     uhist, ushift, phist, vbuf, carry) = refs[len(_WEIGHT_NAMES):]

    b = pl.program_id(0)
    t = pl.program_id(1)
    nt = pl.num_programs(1)

    @pl.when((b == 0) & (t == 0))
    def _():
        carry[...] = jnp.zeros((SUBLANES, LANES), F32)

    @pl.when(t == 0)
    def _():
        for q in range(NSEQ):
            uhist[q, 0:CONV_HALO, :] = jnp.zeros((CONV_HALO, D_CONV), F32)
            phist[q, 0:POOL_HALO, :] = jnp.zeros((POOL_HALO, D_POOL), F32)

    gates = []
    for q in range(NSEQ):
        u, up, gc, gp = _in_proj(x_ref[q], mod_ref[q, 0:1, :], mod_ref[q, 1:2, :],
                                 w["nmix"], w["win"], w["bin"])
        uhist[q, CONV_HALO:CONV_HALO + TT, :] = u
        phist[q, POOL_HALO:POOL_HALO + TT, :] = up
        gates.append((gc, gp))

    def conv_and_pool(q):
        for r in range(1, SUBLANES):
            ushift[q, r - 1] = uhist[q, r:r + CONV_SHIFT_ROWS, :]
        for c in range(TT // CONV_ROWS):
            acc = jnp.zeros((CONV_ROWS, D_CONV), F32)
            for k in range(CONV_WIDTH):
                qq, r = divmod(CONV_HALO - CONV_BUF + k, SUBLANES)
                start = qq * SUBLANES + c * CONV_ROWS
                if r == 0:
                    tap = uhist[q, start:start + CONV_ROWS, :]
                else:
                    tap = ushift[q, r - 1, start:start + CONV_ROWS, :]
                acc = acc + w["cw"][k:k + 1, :] * tap
            s = _conv_act(acc, w["cb"], w["lng"], w["lnb"])
            vbuf[q, c * CONV_ROWS:(c + 1) * CONV_ROWS, :] = s.astype(BF16)

        pos = t * TT + jax.lax.broadcasted_iota(I32, (TT, 1), 0)
        pooled = []
        for g, win in enumerate(POOL_WINDOWS):
            lo, hi = g * POOL_GROUP_IN, (g + 1) * POOL_GROUP_IN
            cur = phist[q, POOL_HALO:POOL_HALO + TT, lo:hi]
            ssum = cur
            for i in range(1, win):
                ssum = ssum + phist[q, POOL_HALO - i:POOL_HALO - i + TT, lo:hi]
            cnt = jnp.minimum(pos + 1, win).astype(F32)
            pooled.append(ssum / cnt - cur)
        return pooled

    cur_carry = carry[0:1, :]
    for q in range(NSEQ):
        pooled = conv_and_pool(q)
        gc, gp = gates[q]
        x1, h2, tw, pk8, cur_carry = _merge_and_route(
            x_ref[q], vbuf[q], pooled, gc, gp, mod_ref[q, 2:3, :], mod_ref[q, 3:4, :],
            mod_ref[q, 4:5, :], cur_carry, w)
        x1_ref[q] = x1
        _store_token_tiles(h2_ref.at[q], h2)
        tw_ref[q] = tw
        pk_ref[q] = pk8

        uhist[q, 0:CONV_HALO, :] = uhist[q, TT:TT + CONV_HALO, :]
        phist[q, 0:POOL_HALO, :] = phist[q, TT:TT + POOL_HALO, :]

    carry[...] = jnp.broadcast_to(cur_carry, (SUBLANES, LANES))
    cnt_ref[...] = jnp.broadcast_to(cur_carry, (SUBLANES, LANES))

    @pl.when(t == nt - 1)
    def _():
        for q in range(NSEQ):
            nconv_ref[q] = uhist[q, CONV_HALO + TT - CONV_BUF:CONV_HALO + TT, :]
            npool_ref[q] = phist[q, POOL_HALO + TT - POOL_BUF:POOL_HALO + TT, :]


def _prompt_mixer(l, x, mod, weights, batch, seq):
    nt = seq // TT
    n_tok = batch * seq
    half = n_tok // NSEQ
    row = lambda rows, width: pl.BlockSpec((NSEQ, rows, width), lambda b, t: (0, b * nt + t, 0))
    in_specs = [row(TT, D_MODEL), pl.BlockSpec((NSEQ, None, N_MOD, D_MODEL), lambda b, t: (0, b, 0, 0))]
    in_specs += _weight_specs(l, 2)
    out_shape = (
        jax.ShapeDtypeStruct((NSEQ, half, D_MODEL), F32),
        jax.ShapeDtypeStruct((NSEQ, half * ROW_TILE, LANES), F32),
        jax.ShapeDtypeStruct((NSEQ, half, LANES), F32),
        jax.ShapeDtypeStruct((NSEQ, SUBLANES, half), I32),
        jax.ShapeDtypeStruct((SUBLANES, LANES), F32),
        jax.ShapeDtypeStruct((NSEQ, batch // NSEQ, CONV_BUF, D_CONV), F32),
        jax.ShapeDtypeStruct((NSEQ, batch // NSEQ, POOL_BUF, D_POOL), F32),
    )
    out_specs = (
        row(TT, D_MODEL), row(TT * ROW_TILE, LANES), row(TT, LANES),
        pl.BlockSpec((NSEQ, SUBLANES, TT), lambda b, t: (0, 0, b * nt + t)),
        pl.BlockSpec((SUBLANES, LANES), lambda b, t: (0, 0)),
        pl.BlockSpec((NSEQ, None, CONV_BUF, D_CONV), lambda b, t: (0, b, 0, 0)),
        pl.BlockSpec((NSEQ, None, POOL_BUF, D_POOL), lambda b, t: (0, b, 0, 0)),
    )
    x1, h2, tw, pk8, cnt, nconv, npool = pl.pallas_call(
        _prompt_mixer_kernel,
        grid=(batch // NSEQ, nt),
        in_specs=in_specs,
        out_specs=out_specs,
        out_shape=out_shape,
        scratch_shapes=[
            pltpu.VMEM((NSEQ, CONV_HALO + TT, D_CONV), F32),
            pltpu.VMEM((NSEQ, SUBLANES - 1, CONV_SHIFT_ROWS, D_CONV), F32),
            pltpu.VMEM((NSEQ, POOL_HALO + TT, D_POOL), F32),
            pltpu.VMEM((NSEQ, TT, D_CONV), BF16),
            pltpu.VMEM((SUBLANES, LANES), F32),
        ],
        compiler_params=pltpu.CompilerParams(
            dimension_semantics=("arbitrary", "arbitrary"), vmem_limit_bytes=VMEM_LIMIT),
        name=f"prompt_mixer_l{l}",
    )(x.reshape(NSEQ, half, D_MODEL), mod.reshape(NSEQ, batch // NSEQ, N_MOD, D_MODEL), *weights)
    return (x1.reshape(n_tok, D_MODEL), h2.reshape(n_tok * ROW_TILE, LANES), tw.reshape(n_tok, LANES),
            jnp.transpose(pk8, (1, 0, 2)).reshape(SUBLANES, n_tok), cnt,
            nconv.reshape(batch, CONV_BUF, D_CONV), npool.reshape(batch, POOL_BUF, D_POOL))


def _sample_mixer_kernel(x_ref, mod_ref, cs_ref, ps_ref, cnt0_ref, *refs, dec_seq, pos0):
    w = dict(zip(_WEIGHT_NAMES, refs[:len(_WEIGHT_NAMES)]))
    (x1_ref, h2_ref, tw_ref, pk_ref, cnt_ref, nconv_ref, npool_ref,
     ufull, pfull, vbuf, carry) = refs[len(_WEIGHT_NAMES):]

    def per_row(v):
        return jnp.concatenate([v] * dec_seq, axis=0)

    @pl.when(pl.program_id(0) == 0)
    def _():
        carry[...] = cnt0_ref[...]

    x = x_ref[...]
    sh1, sc1, gt1 = per_row(mod_ref[0]), per_row(mod_ref[1]), per_row(mod_ref[2])
    sh2, sc2 = per_row(mod_ref[3]), per_row(mod_ref[4])

    u, up, gc, gp = _in_proj(x, sh1, sc1, w["nmix"], w["win"], w["bin"])
    ufull[0:CONV_BUF] = cs_ref[...]
    pfull[0:POOL_BUF] = ps_ref[...]
    for j in range(dec_seq):
        ufull[CONV_BUF + j] = u[j * SB:(j + 1) * SB, :]
        pfull[POOL_BUF + j] = up[j * SB:(j + 1) * SB, :]
    nconv_ref[...] = ufull[dec_seq:dec_seq + CONV_BUF]
    npool_ref[...] = pfull[dec_seq:dec_seq + POOL_BUF]

    for j in range(dec_seq):
        acc = jnp.zeros((SB, D_CONV), F32)
        for k in range(CONV_WIDTH):
            acc = acc + w["cw"][k:k + 1, :] * ufull[j + k]
        s = _conv_act(acc, w["cb"], w["lng"], w["lnb"])
        vbuf[j * SB:(j + 1) * SB, :] = s.astype(BF16)

    pooled = []
    for g, win in enumerate(POOL_WINDOWS):
        lo, hi = g * POOL_GROUP_IN, (g + 1) * POOL_GROUP_IN
        parts = []
        for j in range(dec_seq):
            cur = pfull[POOL_BUF + j, :, lo:hi]
            ssum = cur
            for i in range(1, win):
                ssum = ssum + pfull[POOL_BUF + j - i, :, lo:hi]
            cnt = float(min(pos0 + j + 1, win))
            parts.append(ssum / cnt - cur)
        pooled.append(jnp.concatenate(parts, axis=0))

    x1, h2, tw, pk8, new_carry = _merge_and_route(
        x, vbuf[...], pooled, gc, gp, gt1, sh2, sc2, carry[0:1, :], w)
    x1_ref[...] = x1
    _store_token_tiles(h2_ref, h2)
    tw_ref[...] = tw
    pk_ref[...] = pk8
    carry[...] = jnp.broadcast_to(new_carry, (SUBLANES, LANES))
    cnt_ref[...] = jnp.broadcast_to(new_carry, (SUBLANES, LANES))


def _sample_mixer(l, x, mod, conv_state, pool_state, cnt0, weights, dec_batch, dec_seq, pos0):
    rows = dec_seq * SB
    n_s = dec_batch * dec_seq
    row = lambda width: pl.BlockSpec((rows, width), lambda i: (i, 0))
    conv_spec = pl.BlockSpec((None, CONV_BUF, SB, D_CONV), lambda i: (i, 0, 0, 0))
    pool_spec = pl.BlockSpec((None, POOL_BUF, SB, D_POOL), lambda i: (i, 0, 0, 0))
    cnt_spec = pl.BlockSpec((SUBLANES, LANES), lambda i: (0, 0))
    in_specs = [row(D_MODEL), pl.BlockSpec((N_MOD, SB, D_MODEL), lambda i: (0, i, 0)),
                conv_spec, pool_spec, cnt_spec]
    in_specs += _weight_specs(l, 1)
    out_shape = _mixer_out_shapes(n_s) + (
        jax.ShapeDtypeStruct(conv_state.shape, F32),
        jax.ShapeDtypeStruct(pool_state.shape, F32),
    )
    out_specs = (row(D_MODEL), pl.BlockSpec((rows * ROW_TILE, LANES), lambda i: (i, 0)), row(LANES),
                 pl.BlockSpec((SUBLANES, rows), lambda i: (0, i)), cnt_spec, conv_spec, pool_spec)
    return pl.pallas_call(
        functools.partial(_sample_mixer_kernel, dec_seq=dec_seq, pos0=pos0),
        grid=(dec_batch // SB,),
        in_specs=in_specs,
        out_specs=out_specs,
        out_shape=out_shape,
        scratch_shapes=[
            pltpu.VMEM((CONV_BUF + dec_seq, SB, D_CONV), F32),
            pltpu.VMEM((POOL_BUF + dec_seq, SB, D_POOL), F32),
            pltpu.VMEM((rows, D_CONV), BF16),
            pltpu.VMEM((SUBLANES, LANES), F32),
        ],
        compiler_params=pltpu.CompilerParams(
            dimension_semantics=("arbitrary",), vmem_limit_bytes=VMEM_LIMIT),
        name=f"sample_mixer_l{l}",
    )(x, mod, conv_state, pool_state, cnt0, *weights)


def _slots_kernel(pstart_ref, pk_ref, o_ref):
    pk = pk_ref[...]
    expert = pk & (N_EXPERTS - 1)
    slot = pk >> EXPERT_BITS
    for e in range(N_EXPERTS):
        slot = slot + jnp.where(expert == e, pstart_ref[e], 0)
    o_ref[...] = slot


def _slots(pstart, pk):
    spec = pl.BlockSpec(pk.shape, lambda i, ps: (0, 0))
    return pl.pallas_call(
        _slots_kernel,
        grid_spec=pltpu.PrefetchScalarGridSpec(
            num_scalar_prefetch=1, grid=(1,), in_specs=[spec], out_specs=spec),
        out_shape=jax.ShapeDtypeStruct(pk.shape, I32),
        name="slots",
    )(pstart, pk)


def _fetch_slots(slots_hbm, first_token, count, idx, first_idx, sem):
    copies = [
        pltpu.make_async_copy(slots_hbm.at[k, pl.ds(first_token, count)],
                              idx.at[pl.ds(first_idx + k * count, count)], sem)
        for k in range(TOP_K)]
    for cp in copies:
        cp.start()
    for cp in copies:
        cp.wait()


def _dispatch_kernel(slots_hbm, h2p_ref, h2s_ref, xs_hbm, idx, isem, rsem, *, n_prompt_steps):
    c = pl.program_id(0)
    _fetch_slots(slots_hbm, c * CH, CH, idx, 0, isem)

    def scatter_rows(src_ref):
        def body(t, carry):
            for k in range(TOP_K):
                pltpu.make_async_copy(_token_tile(src_ref, t), _token_tile(xs_hbm, idx[k * CH + t]),
                                      rsem).start(priority=k % 2)
            return carry
        jax.lax.fori_loop(0, CH, body, 0, unroll=4)
        for _ in range(TOP_K):
            pltpu.make_async_copy(src_ref, xs_hbm.at[pl.ds(0, CH * ROW_TILE), :], rsem).wait()

    @pl.when(c < n_prompt_steps)
    def _():
        scatter_rows(h2p_ref)

    @pl.when(c >= n_prompt_steps)
    def _():
        scatter_rows(h2s_ref)


def _dispatch(slots, h2p, h2s, n_slots):
    n_p, n_s = h2p.shape[0] // ROW_TILE, h2s.shape[0] // ROW_TILE
    np_steps, ns_steps = n_p // CH, n_s // CH
    return pl.pallas_call(
        functools.partial(_dispatch_kernel, n_prompt_steps=np_steps),
        grid=(np_steps + ns_steps,),
        in_specs=[
            pl.BlockSpec(memory_space=pl.ANY),
            pl.BlockSpec((CH * ROW_TILE, LANES), lambda c: (jnp.minimum(c, np_steps - 1), 0)),
            pl.BlockSpec((CH * ROW_TILE, LANES), lambda c: (jnp.maximum(c - np_steps, 0), 0)),
        ],
        out_specs=pl.BlockSpec(memory_space=pl.ANY),
        scratch_shapes=[
            pltpu.SMEM((TOP_K * CH,), I32),
            pltpu.SemaphoreType.DMA,
            pltpu.SemaphoreType.DMA,
        ],
        out_shape=jax.ShapeDtypeStruct((n_slots * ROW_TILE, LANES), F32),
        compiler_params=pltpu.CompilerParams(
            dimension_semantics=("arbitrary",), vmem_limit_bytes=VMEM_LIMIT),
        name="dispatch",
    )(slots, h2p, h2s)


def _experts_kernel(be_ref, nv_ref, nu_ref, nx_ref, x_ref, wup_hbm, bup_ref, wdn_hbm, bdn_ref, o_ref,
                    wup_f32, wdn_f32, wup_bf, wdn_bf, wsem, *, layer):
    b = pl.program_id(0)

    def weight_copies(e):
        return (pltpu.make_async_copy(wup_hbm.at[layer, e], wup_f32, wsem.at[0]),
                pltpu.make_async_copy(wdn_hbm.at[layer, e], wdn_f32, wsem.at[1]))

    @pl.when(b < nu_ref[0])
    def _():
        e = be_ref[b]
        prev = be_ref[jnp.maximum(b - 1, 0)]

        @pl.when((b == 0) | (e != prev))
        def _():
            @pl.when(b == 0)
            def _():
                for cp in weight_copies(e):
                    cp.start()
            for cp in weight_copies(e):
                cp.wait()
            for r in range(0, D_MODEL, LANES):
                wup_bf[r:r + LANES, :] = wup_f32[r:r + LANES, :].astype(BF16)
                wdn_bf[r:r + LANES, :] = wdn_f32[r:r + LANES, :].astype(BF16)
            nxt = nx_ref[e]

            @pl.when(nxt >= 0)
            def _():
                for cp in weight_copies(nxt):
                    cp.start()

        n_valid = nv_ref[b]

        def ffn(rows):
            rid = jax.lax.broadcasted_iota(I32, (rows, D_MODEL), 0)
            x = jnp.where(rid < n_valid, _load_token_tiles(x_ref, 0, rows), 0.0)
            a = _dot(x.astype(BF16), wup_bf[...]) + bup_ref[...]
            a_glu = jnp.minimum(a[:, :D_EXPERT], SWIGLU_LIMIT)
            a_lin = jnp.clip(a[:, D_EXPERT:], -SWIGLU_LIMIT, SWIGLU_LIMIT)
            o = a_glu * jax.nn.sigmoid(SWIGLU_ALPHA * a_glu) * (a_lin + 1.0)
            _store_token_tiles(o_ref, _dot(o.astype(BF16), wdn_bf[...]) + bdn_ref[...])
            if rows < BM:
                o_ref[rows * ROW_TILE:, :] = jnp.zeros(((BM - rows) * ROW_TILE, LANES), F32)

        @pl.when(n_valid > BM // 2)
        def _():
            ffn(BM)

        @pl.when(n_valid <= BM // 2)
        def _():
            ffn(BM // 2)

    @pl.when(b >= nu_ref[0])
    def _():
        o_ref[...] = jnp.zeros((BM * ROW_TILE, LANES), F32)


def _experts(l, x_sorted, block_expert, n_valid, n_used, next_expert, w_up, b_up, w_down, b_down):
    n_slots = x_sorted.shape[0] // ROW_TILE
    nb = n_slots // BM

    def used_map(b, be, nv, nu, nx):
        return (jnp.minimum(b, nu[0] - 1), 0)

    bmap = lambda b, be, nv, nu, nx: (l, be[b], 0, 0)
    grid_spec = pltpu.PrefetchScalarGridSpec(
        num_scalar_prefetch=4,
        grid=(nb,),
        in_specs=[
            pl.BlockSpec((BM * ROW_TILE, LANES), used_map),
            pl.BlockSpec(memory_space=pl.ANY),
            pl.BlockSpec((None, None, 1, 2 * D_EXPERT), bmap),
            pl.BlockSpec(memory_space=pl.ANY),
            pl.BlockSpec((None, None, 1, D_MODEL), bmap),
        ],
        out_specs=pl.BlockSpec((BM * ROW_TILE, LANES), lambda b, be, nv, nu, nx: (b, 0)),
        scratch_shapes=[
            pltpu.VMEM((D_MODEL, 2 * D_EXPERT), F32),
            pltpu.VMEM((D_EXPERT, D_MODEL), F32),
            pltpu.VMEM((D_MODEL, 2 * D_EXPERT), BF16),
            pltpu.VMEM((D_EXPERT, D_MODEL), BF16),
            pltpu.SemaphoreType.DMA((2,)),
        ],
    )
    return pl.pallas_call(
        functools.partial(_experts_kernel, layer=l),
        grid_spec=grid_spec,
        out_shape=jax.ShapeDtypeStruct((n_slots * ROW_TILE, LANES), F32),
        compiler_params=pltpu.CompilerParams(
            dimension_semantics=("arbitrary",), vmem_limit_bytes=VMEM_LIMIT),
        name=f"experts_l{l}",
    )(block_expert, n_valid, n_used, next_expert, x_sorted, w_up,
      b_up.reshape(DEPTH, N_EXPERTS, 1, 2 * D_EXPERT), w_down,
      b_down.reshape(DEPTH, N_EXPERTS, 1, D_MODEL))


def _block_tables(counts, nb):
    nblk = (counts + BM - 1) // BM
    blk_end = jnp.cumsum(nblk)
    blk_start = blk_end - nblk
    bidx = jnp.arange(nb, dtype=I32)
    block_expert = jnp.minimum(
        jnp.sum((blk_end[None, :] <= bidx[:, None]).astype(I32), axis=1), N_EXPERTS - 1)
    n_valid = jnp.clip(counts[block_expert] - (bidx - blk_start[block_expert]) * BM, 0, BM)
    eid = jnp.arange(N_EXPERTS, dtype=I32)
    later = (eid[None, :] > eid[:, None]) & (nblk[None, :] > 0)
    next_expert = jnp.min(jnp.where(later, eid[None, :], N_EXPERTS), axis=1)
    next_expert = jnp.where(next_expert == N_EXPERTS, -1, next_expert)
    return ((blk_start * BM).astype(I32), block_expert.astype(I32), n_valid.astype(I32),
            blk_end[-1:].astype(I32), next_expert.astype(I32))


def _combine_kernel(slots_hbm, y_hbm, x1_ref, tw_ref, gt_ref, g_ref, o_ref, ybuf0, ybuf1, idx,
                    isem, rsem, *, rows, tok0, gate_copies, final):
    i = pl.program_id(0)
    n = pl.num_programs(0)
    ybuf = (ybuf0, ybuf1)

    def fetch_indices(step, slot):
        _fetch_slots(slots_hbm, tok0 + step * rows, rows, idx, slot * TOP_K * rows, isem.at[slot])

    def gather_rows(slot):
        def body(t, carry):
            for k in range(TOP_K):
                d = idx[(slot * TOP_K + k) * rows + t]
                pltpu.make_async_copy(_token_tile(y_hbm, d), _token_tile(ybuf[slot], k * rows + t),
                                      rsem.at[slot]).start(priority=k % 2)
            return carry
        jax.lax.fori_loop(0, rows, body, 0, unroll=4)

    @pl.when(i == 0)
    def _():
        fetch_indices(0, 0)
        gather_rows(0)

    for s in range(2):
        @pl.when((i % 2 == s) & (i + 1 < n))
        def _():
            fetch_indices(i + 1, 1 - s)
            gather_rows(1 - s)

    for s in range(2):
        @pl.when(i % 2 == s)
        def _():
            pltpu.make_async_copy(y_hbm.at[pl.ds(0, TOP_K * rows * ROW_TILE), :], ybuf[s], rsem.at[s]).wait()
            tw = tw_ref[...]
            moe = tw[:, 0:1] * _load_token_tiles(ybuf[s], 0, rows)
            for k in range(1, TOP_K):
                moe = moe + tw[:, k:k + 1] * _load_token_tiles(ybuf[s], k * rows, rows)
            gt = gt_ref[...]
            if gate_copies > 1:
                gt = jnp.concatenate([gt] * gate_copies, axis=0)
            x = x1_ref[...] + gt * moe
            o_ref[...] = _rms(x, g_ref[...]) if final else x


def _combine(slots, y_sorted, x1, tw, gate, norm_final, *, rows, tok0, gate_spec, gate_copies,
             final, name):
    n = x1.shape[0]
    row = lambda width: pl.BlockSpec((rows, width), lambda i: (i, 0))
    return pl.pallas_call(
        functools.partial(_combine_kernel, rows=rows, tok0=tok0, gate_copies=gate_copies, final=final),
        grid=(n // rows,),
        in_specs=[
            pl.BlockSpec(memory_space=pl.ANY),
            pl.BlockSpec(memory_space=pl.ANY),
            row(D_MODEL), row(LANES), gate_spec,
            pl.BlockSpec((1, D_MODEL), lambda i: (0, 0)),
        ],
        out_specs=row(D_MODEL),
        scratch_shapes=[
            pltpu.VMEM((TOP_K * rows * ROW_TILE, LANES), F32),
            pltpu.VMEM((TOP_K * rows * ROW_TILE, LANES), F32),
            pltpu.SMEM((2 * TOP_K * rows,), I32),
            pltpu.SemaphoreType.DMA((2,)),
            pltpu.SemaphoreType.DMA((2,)),
        ],
        out_shape=jax.ShapeDtypeStruct((n, D_MODEL), F32),
        compiler_params=pltpu.CompilerParams(
            dimension_semantics=("arbitrary",), vmem_limit_bytes=VMEM_LIMIT),
        name=name,
    )(slots, y_sorted, x1, tw, gate, norm_final.reshape(1, D_MODEL))


def kernel(x_prompt, x_sample, state_conv, state_pool, c_prompt, c_sample, w_ada, b_ada, norm_mix, w_in, b_in, conv_w, conv_b, conv_ln_g, conv_ln_b, w_conv_out, w_pool, pool_scale, w_out, norm_ffn, w_router, b_router, w_up, b_up, w_down, b_down, norm_final):
    batch, seq, _ = x_prompt.shape
    dec_batch, dec_seq, _ = x_sample.shape
    n_p = batch * seq
    n_s = dec_batch * dec_seq
    n_blocks = dec_batch // SB
    n_assign = (n_p + n_s) * TOP_K
    nb = n_assign // BM + N_EXPERTS

    vec = lambda a: a.reshape(DEPTH, 1, a.shape[-1])
    weights = (
        vec(norm_mix), w_in.astype(BF16), vec(b_in), conv_w, vec(conv_b), vec(conv_ln_g),
        vec(conv_ln_b), w_conv_out.astype(BF16), w_pool.astype(BF16), vec(pool_scale),
        w_out.astype(BF16), vec(norm_ffn),
        jnp.pad(w_router, ((0, 0), (0, 0), (0, LANES - N_EXPERTS))).astype(BF16),
        vec(jnp.pad(b_router, ((0, 0), (0, LANES - N_EXPERTS)))),
    )

    mod = _adaln(jnp.concatenate([c_prompt, c_sample], axis=0), w_ada, b_ada)
    mod = mod.reshape(DEPTH, batch + dec_batch, N_MOD, D_MODEL)
    mod_p = mod[:, :batch]
    mod_s = jnp.transpose(mod[:, batch:], (0, 2, 1, 3))

    def to_sample_order(a, rows):
        return jnp.transpose(a.reshape(n_blocks, SB, rows, a.shape[-1]), (0, 2, 1, 3))

    def from_sample_order(a):
        return jnp.transpose(a, (0, 2, 1, 3)).reshape(dec_batch, a.shape[1], a.shape[-1])

    xp = x_prompt.reshape(n_p, D_MODEL)
    xs = to_sample_order(x_sample, dec_seq).reshape(n_s, D_MODEL)

    gate_p = pl.BlockSpec((None, 1, D_MODEL), lambda i: (i // (seq // TT), 0, 0))
    gate_s = pl.BlockSpec((SB, D_MODEL), lambda i: (i, 0))

    new_conv_p, new_pool_p, new_conv_s, new_pool_s = [], [], [], []
    for l in range(DEPTH):
        x1p, h2p, twp, pkp, cntp, ncp, npp = _prompt_mixer(l, xp, mod_p[l], weights, batch, seq)
        x1s, h2s, tws, pks, cnt, ncs, nps = _sample_mixer(
            l, xs, mod_s[l], to_sample_order(state_conv[l], CONV_BUF),
            to_sample_order(state_pool[l], POOL_BUF), cntp, weights, dec_batch, dec_seq, PAST_LEN)
        new_conv_p.append(ncp)
        new_pool_p.append(npp)
        new_conv_s.append(from_sample_order(ncs))
        new_pool_s.append(from_sample_order(nps))

        counts = cnt[0, :N_EXPERTS].astype(I32)
        pstart, block_expert, n_valid, n_used, next_expert = _block_tables(counts, nb)
        slots = _slots(pstart, jnp.concatenate([pkp, pks], axis=1))
        x_sorted = _dispatch(slots, h2p, h2s, nb * BM)
        y_sorted = _experts(l, x_sorted, block_expert, n_valid, n_used, next_expert, w_up, b_up, w_down,
                            b_down)

        final = l == DEPTH - 1
        xp = _combine(slots, y_sorted, x1p, twp, mod_p[l][:, 5:6, :], norm_final,
                      rows=TT, tok0=0, gate_spec=gate_p, gate_copies=1, final=final,
                      name=f"combine_prompt_l{l}")
        xs = _combine(slots, y_sorted, x1s, tws, mod_s[l][5], norm_final,
                      rows=dec_seq * SB, tok0=n_p, gate_spec=gate_s, gate_copies=dec_seq, final=final,
                      name=f"combine_sample_l{l}")

    y_prompt = xp.reshape(batch, seq, D_MODEL)
    y_sample = from_sample_order(xs.reshape(n_blocks, dec_seq, SB, D_MODEL))
    return (y_prompt, y_sample, jnp.stack(new_conv_p), jnp.stack(new_pool_p),
            jnp.stack(new_conv_s), jnp.stack(new_pool_s))
```

```python
import functools

import jax
import jax.numpy as jnp
from jax.experimental import pallas as pl
from jax.experimental.pallas import tpu as pltpu

F32 = jnp.float32
BF16 = jnp.bfloat16
I32 = jnp.int32

D_MODEL = 1024
DEPTH = 2
D_CONV = 512
CONV_WIDTH = 31
CONV_BUF = CONV_WIDTH - 1
D_POOL = 512
POOL_WINDOWS = (2, 4, 8, 16)
POOL_GROUP_IN = D_POOL // len(POOL_WINDOWS)
POOL_GROUP_OUT = D_MODEL // len(POOL_WINDOWS)
POOL_BUF = max(POOL_WINDOWS) - 1
IN_COLS = 2 * D_CONV + D_POOL + 2 * D_MODEL
N_EXPERTS = 32
TOP_K = 4
D_EXPERT = D_MODEL
SWIGLU_LIMIT = 7.0
SWIGLU_ALPHA = 1.702
N_MOD = 6
EPS = 1e-6
PAST_LEN = 16384

LANES = 128
SUBLANES = 8
VMEM_LIMIT = 52 * 1024 * 1024

TT = 256
NSEQ = 2
CONV_HALO = 32
POOL_HALO = 16
CONV_ROWS = 64
SB = 32
BM = 512
CH = 512
ROW_TILE = D_MODEL // LANES
EXPERT_BITS = 5
CONV_SHIFT_ROWS = TT + CONV_HALO - SUBLANES
assert N_EXPERTS == 1 << EXPERT_BITS and ROW_TILE == SUBLANES


def _rms(x, g):
    return x * jax.lax.rsqrt(jnp.mean(x * x, axis=-1, keepdims=True) + EPS) * g


def _dot(a, b):
    return jnp.dot(a, b, preferred_element_type=F32)


def _store_token_tiles(ref, v):
    rows = v.shape[0]
    for s in range(ROW_TILE):
        ref[pl.ds(s, rows, stride=ROW_TILE), :] = v[:, s * LANES:(s + 1) * LANES]


def _load_token_tiles(ref, first_token, rows):
    return jnp.concatenate(
        [ref[pl.ds(first_token * ROW_TILE + s, rows, stride=ROW_TILE), :] for s in range(ROW_TILE)], axis=-1)


def _token_tile(ref, t):
    return ref.at[pl.ds(pl.multiple_of(t * ROW_TILE, ROW_TILE), ROW_TILE), :]


def _adaln_kernel(c_ref, w_ref, b_ref, o_ref):
    c = c_ref[...]
    a = (c * jax.nn.sigmoid(c)).astype(BF16)
    o_ref[...] = _dot(a, w_ref[...].astype(BF16)) + b_ref[...]


def _adaln(c_all, w_ada, b_ada):
    n = c_all.shape[0]
    cols = N_MOD * D_MODEL
    return pl.pallas_call(
        _adaln_kernel,
        grid=(DEPTH, N_MOD),
        in_specs=[
            pl.BlockSpec((n, D_MODEL), lambda l, j: (0, 0)),
            pl.BlockSpec((None, D_MODEL, D_MODEL), lambda l, j: (l, 0, j)),
            pl.BlockSpec((None, 1, D_MODEL), lambda l, j: (l, 0, j)),
        ],
        out_specs=pl.BlockSpec((None, None, n, D_MODEL), lambda l, j: (l, j, 0, 0)),
        out_shape=jax.ShapeDtypeStruct((DEPTH, N_MOD, n, D_MODEL), F32),
        compiler_params=pltpu.CompilerParams(
            dimension_semantics=("arbitrary", "arbitrary"), vmem_limit_bytes=VMEM_LIMIT),
        name="adaln",
    )(c_all, w_ada, b_ada.reshape(DEPTH, 1, cols))


def _in_proj(x, sh1, sc1, nmix_ref, win_ref, bin_ref):
    h = (_rms(x, nmix_ref[...]) * (1.0 + sc1) + sh1).astype(BF16)
    c0, c1, c2 = 2 * D_CONV, 2 * D_CONV + D_POOL, IN_COLS
    zu = _dot(h, win_ref[:, 0:c0]) + bin_ref[:, 0:c0]
    u = zu[:, :D_CONV] * jax.nn.sigmoid(zu[:, D_CONV:])
    up = _dot(h, win_ref[:, c0:c1]) + bin_ref[:, c0:c1]
    zg = _dot(h, win_ref[:, c1:c2]) + bin_ref[:, c1:c2]
    return u, up, zg[:, :D_MODEL], zg[:, D_MODEL:]


def _conv_act(acc, cb_ref, lng_ref, lnb_ref):
    v = acc + cb_ref[...]
    mu = jnp.mean(v, axis=-1, keepdims=True)
    d = v - mu
    var = jnp.mean(d * d, axis=-1, keepdims=True)
    vn = d * jax.lax.rsqrt(var + EPS) * lng_ref[...] + lnb_ref[...]
    return vn * jax.nn.sigmoid(vn)


def _merge_and_route(x, v_bf, pooled, gc, gp, gt1, sh2, sc2, carry, w):
    rows = x.shape[0]
    y_conv = _dot(v_bf, w["wco"][...])
    y_pool = jnp.concatenate(
        [_dot(pooled[g].astype(BF16), w["wpool"][g]) for g in range(len(POOL_WINDOWS))], axis=-1)
    y_pool = y_pool * w["pscale"][...]
    m = jax.nn.sigmoid(gc) * y_conv + jax.nn.sigmoid(gp) * y_pool
    x1 = x + gt1 * _dot(m.astype(BF16), w["wout"][...])
    h2 = _rms(x1, w["nffn"][...]) * (1.0 + sc2) + sh2
    logits = _dot(h2.astype(BF16), w["wr"][...]) + w["br"][...]
    lane = jax.lax.broadcasted_iota(I32, (rows, LANES), 1)
    lane_f = lane.astype(F32)
    neg = jnp.full((rows, LANES), -jnp.inf, F32)
    l = jnp.where(lane < N_EXPERTS, logits, neg)
    vals, idxs, sels = [], [], []
    for _ in range(TOP_K):
        mx = jnp.max(l, axis=-1, keepdims=True)
        ix = jnp.min(jnp.where(l == mx, lane_f, float(LANES)), axis=-1, keepdims=True)
        sel = lane_f == ix
        l = jnp.where(sel, neg, l)
        vals.append(mx)
        idxs.append(ix)
        sels.append(sel)
    es = [jnp.exp(v - vals[0]) for v in vals]
    den = es[0] + es[1] + es[2] + es[3]

    onehot = jnp.where(sels[0] | sels[1] | sels[2] | sels[3], 1.0, 0.0)
    r_i = jax.lax.broadcasted_iota(I32, (rows, rows), 0)
    c_i = jax.lax.broadcasted_iota(I32, (rows, rows), 1)
    before = jnp.where(c_i < r_i, 1.0, 0.0).astype(BF16)
    base = _dot(before, onehot.astype(BF16)) + carry
    new_carry = carry + jnp.sum(onehot, axis=0, keepdims=True)

    tw = jnp.zeros((rows, LANES), F32)
    pk = jnp.zeros((rows, LANES), F32)
    for k in range(TOP_K):
        rank = jnp.sum(jnp.where(sels[k], base, 0.0), axis=-1, keepdims=True)
        tw = jnp.where(lane == k, es[k] / den, tw)
        pk = jnp.where(lane == k, rank * float(N_EXPERTS) + idxs[k], pk)
    pk8 = jnp.transpose(pk)[0:SUBLANES, :].astype(I32)
    return x1, h2, tw, pk8, new_carry


_WEIGHT_NAMES = ("nmix", "win", "bin", "cw", "cb", "lng", "lnb", "wco", "wpool", "pscale", "wout",
                 "nffn", "wr", "br")


def _weight_specs(l, n_grid):
    def spec(*shape):
        zeros = (0,) * len(shape)
        if n_grid == 2:
            return pl.BlockSpec((None,) + shape, lambda b, t: (l,) + zeros)
        return pl.BlockSpec((None,) + shape, lambda i: (l,) + zeros)
    return [
        spec(1, D_MODEL),
        spec(D_MODEL, IN_COLS),
        spec(1, IN_COLS),
        spec(CONV_WIDTH, D_CONV),
        spec(1, D_CONV),
        spec(1, D_CONV),
        spec(1, D_CONV),
        spec(D_CONV, D_MODEL),
        spec(len(POOL_WINDOWS), POOL_GROUP_IN, POOL_GROUP_OUT),
        spec(1, D_MODEL),
        spec(D_MODEL, D_MODEL),
        spec(1, D_MODEL),
        spec(D_MODEL, LANES),
        spec(1, LANES),
    ]


def _mixer_out_shapes(n):
    return (
        jax.ShapeDtypeStruct((n, D_MODEL), F32),
        jax.ShapeDtypeStruct((n * ROW_TILE, LANES), F32),
        jax.ShapeDtypeStruct((n, LANES), F32),
        jax.ShapeDtypeStruct((SUBLANES, n), I32),
        jax.ShapeDtypeStruct((SUBLANES, LANES), F32),
    )


def _prompt_mixer_kernel(x_ref, mod_ref, *refs):
    w = dict(zip(_WEIGHT_NAMES, refs[:len(_WEIGHT_NAMES)]))
    (x1_ref, h2_ref, tw_ref, pk_ref, cnt_ref, nconv_ref, npool_ref,
     uhist, ushift, phist, vbuf, carry) = refs[len(_WEIGHT_NAMES):]

    b = pl.program_id(0)
    t = pl.program_id(1)
    nt = pl.num_programs(1)

    @pl.when((b == 0) & (t == 0))
    def _():
        carry[...] = jnp.zeros((SUBLANES, LANES), F32)

    @pl.when(t == 0)
    def _():
        for q in range(NSEQ):
            uhist[q, 0:CONV_HALO, :] = jnp.zeros((CONV_HALO, D_CONV), F32)
            phist[q, 0:POOL_HALO, :] = jnp.zeros((POOL_HALO, D_POOL), F32)

    gates = []
    for q in range(NSEQ):
        u, up, gc, gp = _in_proj(x_ref[q], mod_ref[0, q], mod_ref[1, q], w["nmix"], w["win"], w["bin"])
        uhist[q, CONV_HALO:CONV_HALO + TT, :] = u
        phist[q, POOL_HALO:POOL_HALO + TT, :] = up
        gates.append((gc, gp))

    def conv_and_pool(q):
        for r in range(1, SUBLANES):
            ushift[q, r - 1] = uhist[q, r:r + CONV_SHIFT_ROWS, :]
        for c in range(TT // CONV_ROWS):
            acc = jnp.zeros((CONV_ROWS, D_CONV), F32)
            for k in range(CONV_WIDTH):
                qq, r = divmod(CONV_HALO - CONV_BUF + k, SUBLANES)
                start = qq * SUBLANES + c * CONV_ROWS
                if r == 0:
                    tap = uhist[q, start:start + CONV_ROWS, :]
                else:
                    tap = ushift[q, r - 1, start:start + CONV_ROWS, :]
                acc = acc + w["cw"][k:k + 1, :] * tap
            s = _conv_act(acc, w["cb"], w["lng"], w["lnb"])
            vbuf[q, c * CONV_ROWS:(c + 1) * CONV_ROWS, :] = s.astype(BF16)

        pos = t * TT + jax.lax.broadcasted_iota(I32, (TT, 1), 0)
        pooled = []
        for g, win in enumerate(POOL_WINDOWS):
            lo, hi = g * POOL_GROUP_IN, (g + 1) * POOL_GROUP_IN
            cur = phist[q, POOL_HALO:POOL_HALO + TT, lo:hi]
            ssum = cur
            for i in range(1, win):
                ssum = ssum + phist[q, POOL_HALO - i:POOL_HALO - i + TT, lo:hi]
            cnt = jnp.minimum(pos + 1, win).astype(F32)
            pooled.append(ssum / cnt - cur)
        return pooled

    cur_carry = carry[0:1, :]
    for q in range(NSEQ):
        pooled = conv_and_pool(q)
        gc, gp = gates[q]
        x1, h2, tw, pk8, cur_carry = _merge_and_route(
            x_ref[q], vbuf[q], pooled, gc, gp, mod_ref[2, q], mod_ref[3, q], mod_ref[4, q], cur_carry, w)
        x1_ref[q] = x1
        _store_token_tiles(h2_ref.at[q], h2)
        tw_ref[q] = tw
        pk_ref[q] = pk8

        uhist[q, 0:CONV_HALO, :] = uhist[q, TT:TT + CONV_HALO, :]
        phist[q, 0:POOL_HALO, :] = phist[q, TT:TT + POOL_HALO, :]

    carry[...] = jnp.broadcast_to(cur_carry, (SUBLANES, LANES))
    cnt_ref[...] = jnp.broadcast_to(cur_carry, (SUBLANES, LANES))

    @pl.when(t == nt - 1)
    def _():
        for q in range(NSEQ):
            nconv_ref[q] = uhist[q, CONV_HALO + TT - CONV_BUF:CONV_HALO + TT, :]
            npool_ref[q] = phist[q, POOL_HALO + TT - POOL_BUF:POOL_HALO + TT, :]


def _prompt_mixer(l, x, mod, weights, batch, seq):
    nt = seq // TT
    n_tok = batch * seq
    half = n_tok // NSEQ
    row = lambda rows, width: pl.BlockSpec((NSEQ, rows, width), lambda b, t: (0, b * nt + t, 0))
    in_specs = [row(TT, D_MODEL),
                pl.BlockSpec((N_MOD, NSEQ, None, 1, D_MODEL), lambda b, t: (0, 0, b, 0, 0))]
    in_specs += _weight_specs(l, 2)
    out_shape = (
        jax.ShapeDtypeStruct((NSEQ, half, D_MODEL), F32),
        jax.ShapeDtypeStruct((NSEQ, half * ROW_TILE, LANES), F32),
        jax.ShapeDtypeStruct((NSEQ, half, LANES), F32),
        jax.ShapeDtypeStruct((NSEQ, SUBLANES, half), I32),
        jax.ShapeDtypeStruct((SUBLANES, LANES), F32),
        jax.ShapeDtypeStruct((NSEQ, batch // NSEQ, CONV_BUF, D_CONV), F32),
        jax.ShapeDtypeStruct((NSEQ, batch // NSEQ, POOL_BUF, D_POOL), F32),
    )
    out_specs = (
        row(TT, D_MODEL), row(TT * ROW_TILE, LANES), row(TT, LANES),
        pl.BlockSpec((NSEQ, SUBLANES, TT), lambda b, t: (0, 0, b * nt + t)),
        pl.BlockSpec((SUBLANES, LANES), lambda b, t: (0, 0)),
        pl.BlockSpec((NSEQ, None, CONV_BUF, D_CONV), lambda b, t: (0, b, 0, 0)),
        pl.BlockSpec((NSEQ, None, POOL_BUF, D_POOL), lambda b, t: (0, b, 0, 0)),
    )
    x1, h2, tw, pk8, cnt, nconv, npool = pl.pallas_call(
        _prompt_mixer_kernel,
        grid=(batch // NSEQ, nt),
        in_specs=in_specs,
        out_specs=out_specs,
        out_shape=out_shape,
        scratch_shapes=[
            pltpu.VMEM((NSEQ, CONV_HALO + TT, D_CONV), F32),
            pltpu.VMEM((NSEQ, SUBLANES - 1, CONV_SHIFT_ROWS, D_CONV), F32),
            pltpu.VMEM((NSEQ, POOL_HALO + TT, D_POOL), F32),
            pltpu.VMEM((NSEQ, TT, D_CONV), BF16),
            pltpu.VMEM((SUBLANES, LANES), F32),
        ],
        compiler_params=pltpu.CompilerParams(
            dimension_semantics=("arbitrary", "arbitrary"), vmem_limit_bytes=VMEM_LIMIT),
        name=f"prompt_mixer_l{l}",
    )(x.reshape(NSEQ, half, D_MODEL), mod.reshape(N_MOD, NSEQ, batch // NSEQ, 1, D_MODEL), *weights)
    return (x1.reshape(n_tok, D_MODEL), h2.reshape(n_tok * ROW_TILE, LANES), tw.reshape(n_tok, LANES),
            jnp.transpose(pk8, (1, 0, 2)).reshape(SUBLANES, n_tok), cnt,
            nconv.reshape(batch, CONV_BUF, D_CONV), npool.reshape(batch, POOL_BUF, D_POOL))


def _sample_mixer_kernel(x_ref, mod_ref, cs_ref, ps_ref, cnt0_ref, *refs, dec_seq, pos0):
    w = dict(zip(_WEIGHT_NAMES, refs[:len(_WEIGHT_NAMES)]))
    (x1_ref, h2_ref, tw_ref, pk_ref, cnt_ref, nconv_ref, npool_ref,
     ufull, pfull, vbuf, carry) = refs[len(_WEIGHT_NAMES):]

    def per_row(v):
        return jnp.concatenate([v] * dec_seq, axis=0)

    @pl.when(pl.program_id(0) == 0)
    def _():
        carry[...] = cnt0_ref[...]

    x = x_ref[...]
    sh1, sc1, gt1 = per_row(mod_ref[0]), per_row(mod_ref[1]), per_row(mod_ref[2])
    sh2, sc2 = per_row(mod_ref[3]), per_row(mod_ref[4])

    u, up, gc, gp = _in_proj(x, sh1, sc1, w["nmix"], w["win"], w["bin"])
    ufull[0:CONV_BUF] = cs_ref[...]
    pfull[0:POOL_BUF] = ps_ref[...]
    for j in range(dec_seq):
        ufull[CONV_BUF + j] = u[j * SB:(j + 1) * SB, :]
        pfull[POOL_BUF + j] = up[j * SB:(j + 1) * SB, :]
    nconv_ref[...] = ufull[dec_seq:dec_seq + CONV_BUF]
    npool_ref[...] = pfull[dec_seq:dec_seq + POOL_BUF]

    for j in range(dec_seq):
        acc = jnp.zeros((SB, D_CONV), F32)
        for k in range(CONV_WIDTH):
            acc = acc + w["cw"][k:k + 1, :] * ufull[j + k]
        s = _conv_act(acc, w["cb"], w["lng"], w["lnb"])
        vbuf[j * SB:(j + 1) * SB, :] = s.astype(BF16)

    pooled = []
    for g, win in enumerate(POOL_WINDOWS):
        lo, hi = g * POOL_GROUP_IN, (g + 1) * POOL_GROUP_IN
        parts = []
        for j in range(dec_seq):
            cur = pfull[POOL_BUF + j, :, lo:hi]
            ssum = cur
            for i in range(1, win):
                ssum = ssum + pfull[POOL_BUF + j - i, :, lo:hi]
            cnt = float(min(pos0 + j + 1, win))
            parts.append(ssum / cnt - cur)
        pooled.append(jnp.concatenate(parts, axis=0))

    x1, h2, tw, pk8, new_carry = _merge_and_route(
        x, vbuf[...], pooled, gc, gp, gt1, sh2, sc2, carry[0:1, :], w)
    x1_ref[...] = x1
    _store_token_tiles(h2_ref, h2)
    tw_ref[...] = tw
    pk_ref[...] = pk8
    carry[...] = jnp.broadcast_to(new_carry, (SUBLANES, LANES))
    cnt_ref[...] = jnp.broadcast_to(new_carry, (SUBLANES, LANES))


def _sample_mixer(l, x, mod, conv_state, pool_state, cnt0, weights, dec_batch, dec_seq, pos0):
    rows = dec_seq * SB
    n_s = dec_batch * dec_seq
    row = lambda width: pl.BlockSpec((rows, width), lambda i: (i, 0))
    conv_spec = pl.BlockSpec((None, CONV_BUF, SB, D_CONV), lambda i: (i, 0, 0, 0))
    pool_spec = pl.BlockSpec((None, POOL_BUF, SB, D_POOL), lambda i: (i, 0, 0, 0))
    cnt_spec = pl.BlockSpec((SUBLANES, LANES), lambda i: (0, 0))
    in_specs = [row(D_MODEL), pl.BlockSpec((N_MOD, SB, D_MODEL), lambda i: (0, i, 0)),
                conv_spec, pool_spec, cnt_spec]
    in_specs += _weight_specs(l, 1)
    out_shape = _mixer_out_shapes(n_s) + (
        jax.ShapeDtypeStruct(conv_state.shape, F32),
        jax.ShapeDtypeStruct(pool_state.shape, F32),
    )
    out_specs = (row(D_MODEL), pl.BlockSpec((rows * ROW_TILE, LANES), lambda i: (i, 0)), row(LANES),
                 pl.BlockSpec((SUBLANES, rows), lambda i: (0, i)), cnt_spec, conv_spec, pool_spec)
    return pl.pallas_call(
        functools.partial(_sample_mixer_kernel, dec_seq=dec_seq, pos0=pos0),
        grid=(dec_batch // SB,),
        in_specs=in_specs,
        out_specs=out_specs,
        out_shape=out_shape,
        scratch_shapes=[
            pltpu.VMEM((CONV_BUF + dec_seq, SB, D_CONV), F32),
            pltpu.VMEM((POOL_BUF + dec_seq, SB, D_POOL), F32),
            pltpu.VMEM((rows, D_CONV), BF16),
            pltpu.VMEM((SUBLANES, LANES), F32),
        ],
        compiler_params=pltpu.CompilerParams(
            dimension_semantics=("arbitrary",), vmem_limit_bytes=VMEM_LIMIT),
        name=f"sample_mixer_l{l}",
    )(x, mod, conv_state, pool_state, cnt0, *weights)


def _slots_kernel(pstart_ref, pk_ref, o_ref):
    pk = pk_ref[...]
    expert = pk & (N_EXPERTS - 1)
    slot = pk >> EXPERT_BITS
    for e in range(N_EXPERTS):
        slot = slot + jnp.where(expert == e, pstart_ref[e], 0)
    o_ref[...] = slot


def _slots(pstart, pk):
    spec = pl.BlockSpec(pk.shape, lambda i, ps: (0, 0))
    return pl.pallas_call(
        _slots_kernel,
        grid_spec=pltpu.PrefetchScalarGridSpec(
            num_scalar_prefetch=1, grid=(1,), in_specs=[spec], out_specs=spec),
        out_shape=jax.ShapeDtypeStruct(pk.shape, I32),
        name="slots",
    )(pstart, pk)


def _fetch_slots(slots_hbm, first_token, count, idx, first_idx, sem):
    copies = [
        pltpu.make_async_copy(slots_hbm.at[k, pl.ds(first_token, count)],
                              idx.at[pl.ds(first_idx + k * count, count)], sem)
        for k in range(TOP_K)]
    for cp in copies:
        cp.start()
    for cp in copies:
        cp.wait()


def _dispatch_kernel(slots_hbm, h2p_ref, h2s_ref, xs_hbm, idx, isem, rsem, *, n_prompt_steps):
    c = pl.program_id(0)
    _fetch_slots(slots_hbm, c * CH, CH, idx, 0, isem)

    def scatter_rows(src_ref):
        def body(t, carry):
            for k in range(TOP_K):
                pltpu.make_async_copy(_token_tile(src_ref, t), _token_tile(xs_hbm, idx[k * CH + t]),
                                      rsem).start(priority=k % 2)
            return carry
        jax.lax.fori_loop(0, CH, body, 0, unroll=4)
        for _ in range(TOP_K):
            pltpu.make_async_copy(src_ref, xs_hbm.at[pl.ds(0, CH * ROW_TILE), :], rsem).wait()

    @pl.when(c < n_prompt_steps)
    def _():
        scatter_rows(h2p_ref)

    @pl.when(c >= n_prompt_steps)
    def _():
        scatter_rows(h2s_ref)


def _dispatch(slots, h2p, h2s, n_slots):
    n_p, n_s = h2p.shape[0] // ROW_TILE, h2s.shape[0] // ROW_TILE
    np_steps, ns_steps = n_p // CH, n_s // CH
    return pl.pallas_call(
        functools.partial(_dispatch_kernel, n_prompt_steps=np_steps),
        grid=(np_steps + ns_steps,),
        in_specs=[
            pl.BlockSpec(memory_space=pl.ANY),
            pl.BlockSpec((CH * ROW_TILE, LANES), lambda c: (jnp.minimum(c, np_steps - 1), 0)),
            pl.BlockSpec((CH * ROW_TILE, LANES), lambda c: (jnp.maximum(c - np_steps, 0), 0)),
        ],
        out_specs=pl.BlockSpec(memory_space=pl.ANY),
        scratch_shapes=[
            pltpu.SMEM((TOP_K * CH,), I32),
            pltpu.SemaphoreType.DMA,
            pltpu.SemaphoreType.DMA,
        ],
        out_shape=jax.ShapeDtypeStruct((n_slots * ROW_TILE, LANES), F32),
        compiler_params=pltpu.CompilerParams(
            dimension_semantics=("arbitrary",), vmem_limit_bytes=VMEM_LIMIT),
        name="dispatch",
    )(slots, h2p, h2s)


def _experts_kernel(be_ref, nv_ref, nu_ref, nx_ref, x_ref, wup_hbm, bup_ref, wdn_hbm, bdn_ref, o_ref,
                    wup_f32, wdn_f32, wup_bf, wdn_bf, wsem, *, layer):
    b = pl.program_id(0)

    def weight_copies(e):
        return (pltpu.make_async_copy(wup_hbm.at[layer, e], wup_f32, wsem.at[0]),
                pltpu.make_async_copy(wdn_hbm.at[layer, e], wdn_f32, wsem.at[1]))

    @pl.when(b < nu_ref[0])
    def _():
        e = be_ref[b]
        prev = be_ref[jnp.maximum(b - 1, 0)]

        @pl.when((b == 0) | (e != prev))
        def _():
            @pl.when(b == 0)
            def _():
                for cp in weight_copies(e):
                    cp.start()
            for cp in weight_copies(e):
                cp.wait()
            for r in range(0, D_MODEL, LANES):
                wup_bf[r:r + LANES, :] = wup_f32[r:r + LANES, :].astype(BF16)
                wdn_bf[r:r + LANES, :] = wdn_f32[r:r + LANES, :].astype(BF16)
            nxt = nx_ref[e]

            @pl.when(nxt >= 0)
            def _():
                for cp in weight_copies(nxt):
                    cp.start()

        n_valid = nv_ref[b]

        def ffn(rows):
            rid = jax.lax.broadcasted_iota(I32, (rows, D_MODEL), 0)
            x = jnp.where(rid < n_valid, _load_token_tiles(x_ref, 0, rows), 0.0)
            a = _dot(x.astype(BF16), wup_bf[...]) + bup_ref[...]
            a_glu = jnp.minimum(a[:, :D_EXPERT], SWIGLU_LIMIT)
            a_lin = jnp.clip(a[:, D_EXPERT:], -SWIGLU_LIMIT, SWIGLU_LIMIT)
            o = a_glu * jax.nn.sigmoid(SWIGLU_ALPHA * a_glu) * (a_lin + 1.0)
            _store_token_tiles(o_ref, _dot(o.astype(BF16), wdn_bf[...]) + bdn_ref[...])
            if rows < BM:
                o_ref[rows * ROW_TILE:, :] = jnp.zeros(((BM - rows) * ROW_TILE, LANES), F32)

        @pl.when(n_valid > BM // 2)
        def _():
            ffn(BM)

        @pl.when(n_valid <= BM // 2)
        def _():
            ffn(BM // 2)

    @pl.when(b >= nu_ref[0])
    def _():
        o_ref[...] = jnp.zeros((BM * ROW_TILE, LANES), F32)


def _experts(l, x_sorted, block_expert, n_valid, n_used, next_expert, w_up, b_up, w_down, b_down):
    n_slots = x_sorted.shape[0] // ROW_TILE
    nb = n_slots // BM

    def used_map(b, be, nv, nu, nx):
        return (jnp.minimum(b, nu[0] - 1), 0)

    bmap = lambda b, be, nv, nu, nx: (l, be[b], 0, 0)
    grid_spec = pltpu.PrefetchScalarGridSpec(
        num_scalar_prefetch=4,
        grid=(nb,),
        in_specs=[
            pl.BlockSpec((BM * ROW_TILE, LANES), used_map),
            pl.BlockSpec(memory_space=pl.ANY),
            pl.BlockSpec((None, None, 1, 2 * D_EXPERT), bmap),
            pl.BlockSpec(memory_space=pl.ANY),
            pl.BlockSpec((None, None, 1, D_MODEL), bmap),
        ],
        out_specs=pl.BlockSpec((BM * ROW_TILE, LANES), lambda b, be, nv, nu, nx: (b, 0)),
        scratch_shapes=[
            pltpu.VMEM((D_MODEL, 2 * D_EXPERT), F32),
            pltpu.VMEM((D_EXPERT, D_MODEL), F32),
            pltpu.VMEM((D_MODEL, 2 * D_EXPERT), BF16),
            pltpu.VMEM((D_EXPERT, D_MODEL), BF16),
            pltpu.SemaphoreType.DMA((2,)),
        ],
    )
    return pl.pallas_call(
        functools.partial(_experts_kernel, layer=l),
        grid_spec=grid_spec,
        out_shape=jax.ShapeDtypeStruct((n_slots * ROW_TILE, LANES), F32),
        compiler_params=pltpu.CompilerParams(
            dimension_semantics=("arbitrary",), vmem_limit_bytes=VMEM_LIMIT),
        name=f"experts_l{l}",
    )(block_expert, n_valid, n_used, next_expert, x_sorted, w_up,
      b_up.reshape(DEPTH, N_EXPERTS, 1, 2 * D_EXPERT), w_down,
      b_down.reshape(DEPTH, N_EXPERTS, 1, D_MODEL))


def _tables_kernel(cnt_ref, pstart_ref, be_ref, nv_ref, nu_ref, nx_ref, *, nb):
    def clear(b, c):
        be_ref[b] = N_EXPERTS - 1
        nv_ref[b] = 0
        return c
    jax.lax.fori_loop(0, nb, clear, 0)

    def per_expert(e, carry):
        blk, last = carry
        cnt = cnt_ref[e]
        n = (cnt + (BM - 1)) // BM
        pstart_ref[e] = blk * BM
        nx_ref[e] = -1

        def per_block(j, c):
            be_ref[blk + j] = e
            nv_ref[blk + j] = jnp.minimum(cnt - j * BM, BM)
            return c
        jax.lax.fori_loop(0, n, per_block, 0)

        @pl.when((n > 0) & (last >= 0))
        def _():
            nx_ref[last] = e
        return blk + n, jnp.where(n > 0, e, last)

    blk, _ = jax.lax.fori_loop(0, N_EXPERTS, per_expert, (jnp.int32(0), jnp.int32(-1)))
    nu_ref[0] = blk


def _block_tables(counts, nb):
    smem = pl.BlockSpec(memory_space=pltpu.SMEM)
    return pl.pallas_call(
        functools.partial(_tables_kernel, nb=nb),
        in_specs=[smem],
        out_specs=(smem, smem, smem, smem, smem),
        out_shape=(
            jax.ShapeDtypeStruct((N_EXPERTS,), I32),
            jax.ShapeDtypeStruct((nb,), I32),
            jax.ShapeDtypeStruct((nb,), I32),
            jax.ShapeDtypeStruct((1,), I32),
            jax.ShapeDtypeStruct((N_EXPERTS,), I32),
        ),
        name="block_tables",
    )(counts)


def _combine_kernel(slots_hbm, y_hbm, x1_ref, tw_ref, gt_ref, g_ref, o_ref, ybuf0, ybuf1, idx,
                    isem, rsem, *, rows, tok0, gate_copies, final):
    i = pl.program_id(0)
    n = pl.num_programs(0)
    ybuf = (ybuf0, ybuf1)

    def fetch_indices(step, slot):
        _fetch_slots(slots_hbm, tok0 + step * rows, rows, idx, slot * TOP_K * rows, isem.at[slot])

    def gather_rows(slot):
        def body(t, carry):
            for k in range(TOP_K):
                d = idx[(slot * TOP_K + k) * rows + t]
                pltpu.make_async_copy(_token_tile(y_hbm, d), _token_tile(ybuf[slot], k * rows + t),
                                      rsem.at[slot]).start(priority=k % 2)
            return carry
        jax.lax.fori_loop(0, rows, body, 0, unroll=4)

    @pl.when(i == 0)
    def _():
        fetch_indices(0, 0)
        gather_rows(0)

    for s in range(2):
        @pl.when((i % 2 == s) & (i + 1 < n))
        def _():
            fetch_indices(i + 1, 1 - s)
            gather_rows(1 - s)

    for s in range(2):
        @pl.when(i % 2 == s)
        def _():
            pltpu.make_async_copy(y_hbm.at[pl.ds(0, TOP_K * rows * ROW_TILE), :], ybuf[s], rsem.at[s]).wait()
            tw = tw_ref[...]
            moe = tw[:, 0:1] * _load_token_tiles(ybuf[s], 0, rows)
            for k in range(1, TOP_K):
                moe = moe + tw[:, k:k + 1] * _load_token_tiles(ybuf[s], k * rows, rows)
            gt = gt_ref[...]
            if gate_copies > 1:
                gt = jnp.concatenate([gt] * gate_copies, axis=0)
            x = x1_ref[...] + gt * moe
            o_ref[...] = _rms(x, g_ref[...]) if final else x


def _combine(slots, y_sorted, x1, tw, gate, norm_final, *, rows, tok0, gate_spec, gate_copies,
             final, name):
    n = x1.shape[0]
    row = lambda width: pl.BlockSpec((rows, width), lambda i: (i, 0))
    return pl.pallas_call(
        functools.partial(_combine_kernel, rows=rows, tok0=tok0, gate_copies=gate_copies, final=final),
        grid=(n // rows,),
        in_specs=[
            pl.BlockSpec(memory_space=pl.ANY),
            pl.BlockSpec(memory_space=pl.ANY),
            row(D_MODEL), row(LANES), gate_spec,
            pl.BlockSpec((1, D_MODEL), lambda i: (0, 0)),
        ],
        out_specs=row(D_MODEL),
        scratch_shapes=[
            pltpu.VMEM((TOP_K * rows * ROW_TILE, LANES), F32),
            pltpu.VMEM((TOP_K * rows * ROW_TILE, LANES), F32),
            pltpu.SMEM((2 * TOP_K * rows,), I32),
            pltpu.SemaphoreType.DMA((2,)),
            pltpu.SemaphoreType.DMA((2,)),
        ],
        out_shape=jax.ShapeDtypeStruct((n, D_MODEL), F32),
        compiler_params=pltpu.CompilerParams(
            dimension_semantics=("arbitrary",), vmem_limit_bytes=VMEM_LIMIT),
        name=name,
    )(slots, y_sorted, x1, tw, gate, norm_final.reshape(1, D_MODEL))


def kernel(x_prompt, x_sample, state_conv, state_pool, c_prompt, c_sample, w_ada, b_ada, norm_mix, w_in, b_in, conv_w, conv_b, conv_ln_g, conv_ln_b, w_conv_out, w_pool, pool_scale, w_out, norm_ffn, w_router, b_router, w_up, b_up, w_down, b_down, norm_final):
    batch, seq, _ = x_prompt.shape
    dec_batch, dec_seq, _ = x_sample.shape
    n_p = batch * seq
    n_s = dec_batch * dec_seq
    n_blocks = dec_batch // SB
    n_assign = (n_p + n_s) * TOP_K
    nb = n_assign // BM + N_EXPERTS

    vec = lambda a: a.reshape(DEPTH, 1, a.shape[-1])
    weights = (
        vec(norm_mix), w_in.astype(BF16), vec(b_in), conv_w, vec(conv_b), vec(conv_ln_g),
        vec(conv_ln_b), w_conv_out.astype(BF16), w_pool.astype(BF16), vec(pool_scale),
        w_out.astype(BF16), vec(norm_ffn),
        jnp.pad(w_router, ((0, 0), (0, 0), (0, LANES - N_EXPERTS))).astype(BF16),
        vec(jnp.pad(b_router, ((0, 0), (0, LANES - N_EXPERTS)))),
    )

    mod = _adaln(jnp.concatenate([c_prompt, c_sample], axis=0), w_ada, b_ada)
    mod_p = mod[:, :, :batch]
    mod_s = mod[:, :, batch:]

    def to_sample_order(a, rows):
        return jnp.transpose(a.reshape(n_blocks, SB, rows, a.shape[-1]), (0, 2, 1, 3))

    def from_sample_order(a):
        return jnp.transpose(a, (0, 2, 1, 3)).reshape(dec_batch, a.shape[1], a.shape[-1])

    xp = x_prompt.reshape(n_p, D_MODEL)
    xs = to_sample_order(x_sample, dec_seq).reshape(n_s, D_MODEL)

    gate_p = pl.BlockSpec((None, 1, D_MODEL), lambda i: (i // (seq // TT), 0, 0))
    gate_s = pl.BlockSpec((SB, D_MODEL), lambda i: (i, 0))

    new_conv_p, new_pool_p, new_conv_s, new_pool_s = [], [], [], []
    for l in range(DEPTH):
        x1p, h2p, twp, pkp, cntp, ncp, npp = _prompt_mixer(l, xp, mod_p[l], weights, batch, seq)
        x1s, h2s, tws, pks, cnt, ncs, nps = _sample_mixer(
            l, xs, mod_s[l], to_sample_order(state_conv[l], CONV_BUF),
            to_sample_order(state_pool[l], POOL_BUF), cntp, weights, dec_batch, dec_seq, PAST_LEN)
        new_conv_p.append(ncp)
        new_pool_p.append(npp)
        new_conv_s.append(from_sample_order(ncs))
        new_pool_s.append(from_sample_order(nps))

        counts = cnt[0, :N_EXPERTS].astype(I32)
        pstart, block_expert, n_valid, n_used, next_expert = _block_tables(counts, nb)
        slots = _slots(pstart, jnp.concatenate([pkp, pks], axis=1))
        x_sorted = _dispatch(slots, h2p, h2s, nb * BM)
        y_sorted = _experts(l, x_sorted, block_expert, n_valid, n_used, next_expert, w_up, b_up, w_down,
                            b_down)

        final = l == DEPTH - 1
        xp = _combine(slots, y_sorted, x1p, twp, mod_p[l][5][:, None, :], norm_final,
                      rows=TT, tok0=0, gate_spec=gate_p, gate_copies=1, final=final,
                      name=f"combine_prompt_l{l}")
        xs = _combine(slots, y_sorted, x1s, tws, mod_s[l][5], norm_final,
                      rows=dec_seq * SB, tok0=n_p, gate_spec=gate_s, gate_copies=dec_seq, final=final,
                      name=f"combine_sample_l{l}")

    y_prompt = xp.reshape(batch, seq, D_MODEL)
    y_sample = from_sample_order(xs.reshape(n_blocks, dec_seq, SB, D_MODEL))
    return (y_prompt, y_sample, jnp.stack(new_conv_p), jnp.stack(new_pool_p),
            jnp.stack(new_conv_s), jnp.stack(new_pool_s))
```

```python
import functools

import jax
import jax.numpy as jnp
from jax.experimental import pallas as pl
from jax.experimental.pallas import tpu as pltpu

F32 = jnp.float32
BF16 = jnp.bfloat16
I32 = jnp.int32

D_MODEL = 1024
DEPTH = 2
D_CONV = 512
CONV_WIDTH = 31
CONV_BUF = CONV_WIDTH - 1
D_POOL = 512
POOL_WINDOWS = (2, 4, 8, 16)
POOL_GROUP_IN = D_POOL // len(POOL_WINDOWS)
POOL_GROUP_OUT = D_MODEL // len(POOL_WINDOWS)
POOL_BUF = max(POOL_WINDOWS) - 1
IN_COLS = 2 * D_CONV + D_POOL + 2 * D_MODEL
N_EXPERTS = 32
TOP_K = 4
D_EXPERT = D_MODEL
SWIGLU_LIMIT = 7.0
SWIGLU_ALPHA = 1.702
N_MOD = 6
EPS = 1e-6
PAST_LEN = 16384

LANES = 128
SUBLANES = 8
VMEM_LIMIT = 52 * 1024 * 1024

TT = 256
NSEQ = 2
CONV_HALO = 32
POOL_HALO = 16
CONV_ROWS = 64
SB = 64
BM = 512
ROW_TILE = D_MODEL // LANES
EXPERT_BITS = 5
CONV_SHIFT_ROWS = TT + CONV_HALO - SUBLANES
assert N_EXPERTS == 1 << EXPERT_BITS and ROW_TILE == SUBLANES


def _rms(x, g):
    return x * jax.lax.rsqrt(jnp.mean(x * x, axis=-1, keepdims=True) + EPS) * g


def _dot(a, b):
    return jnp.dot(a, b, preferred_element_type=F32)


def _store_token_tiles(ref, v):
    rows = v.shape[0]
    for s in range(ROW_TILE):
        ref[pl.ds(s, rows, stride=ROW_TILE), :] = v[:, s * LANES:(s + 1) * LANES]


def _load_token_tiles(ref, first_token, rows):
    return jnp.concatenate(
        [ref[pl.ds(first_token * ROW_TILE + s, rows, stride=ROW_TILE), :] for s in range(ROW_TILE)], axis=-1)


def _token_tile(ref, t):
    return ref.at[pl.ds(pl.multiple_of(t * ROW_TILE, ROW_TILE), ROW_TILE), :]


def _adaln_kernel(c_ref, w_ref, b_ref, o_ref):
    c = c_ref[...]
    a = (c * jax.nn.sigmoid(c)).astype(BF16)
    o_ref[...] = _dot(a, w_ref[...].astype(BF16)) + b_ref[...]


def _adaln(c_all, w_ada, b_ada):
    n = c_all.shape[0]
    cols = N_MOD * D_MODEL
    return pl.pallas_call(
        _adaln_kernel,
        grid=(DEPTH, N_MOD),
        in_specs=[
            pl.BlockSpec((n, D_MODEL), lambda l, j: (0, 0)),
            pl.BlockSpec((None, D_MODEL, D_MODEL), lambda l, j: (l, 0, j)),
            pl.BlockSpec((None, 1, D_MODEL), lambda l, j: (l, 0, j)),
        ],
        out_specs=pl.BlockSpec((None, None, n, D_MODEL), lambda l, j: (l, j, 0, 0)),
        out_shape=jax.ShapeDtypeStruct((DEPTH, N_MOD, n, D_MODEL), F32),
        compiler_params=pltpu.CompilerParams(
            dimension_semantics=("arbitrary", "arbitrary"), vmem_limit_bytes=VMEM_LIMIT),
        name="adaln",
    )(c_all, w_ada, b_ada.reshape(DEPTH, 1, cols))


def _in_proj(x, sh1, sc1, nmix_ref, win_ref, bin_ref):
    h = (_rms(x, nmix_ref[...]) * (1.0 + sc1) + sh1).astype(BF16)
    c0, c1, c2 = 2 * D_CONV, 2 * D_CONV + D_POOL, IN_COLS
    zu = _dot(h, win_ref[:, 0:c0]) + bin_ref[:, 0:c0]
    u = zu[:, :D_CONV] * jax.nn.sigmoid(zu[:, D_CONV:])
    up = _dot(h, win_ref[:, c0:c1]) + bin_ref[:, c0:c1]
    zg = _dot(h, win_ref[:, c1:c2]) + bin_ref[:, c1:c2]
    return u, up, zg[:, :D_MODEL], zg[:, D_MODEL:]


def _conv_act(acc, cb_ref, lng_ref, lnb_ref):
    v = acc + cb_ref[...]
    mu = jnp.mean(v, axis=-1, keepdims=True)
    d = v - mu
    var = jnp.mean(d * d, axis=-1, keepdims=True)
    vn = d * jax.lax.rsqrt(var + EPS) * lng_ref[...] + lnb_ref[...]
    return vn * jax.nn.sigmoid(vn)


def _merge_and_route(x, v_bf, pooled, gc, gp, gt1, sh2, sc2, carry, w):
    rows = x.shape[0]
    y_conv = _dot(v_bf, w["wco"][...])
    y_pool = jnp.concatenate(
        [_dot(pooled[g].astype(BF16), w["wpool"][g]) for g in range(len(POOL_WINDOWS))], axis=-1)
    y_pool = y_pool * w["pscale"][...]
    m = jax.nn.sigmoid(gc) * y_conv + jax.nn.sigmoid(gp) * y_pool
    x1 = x + gt1 * _dot(m.astype(BF16), w["wout"][...])
    h2 = _rms(x1, w["nffn"][...]) * (1.0 + sc2) + sh2
    logits = _dot(h2.astype(BF16), w["wr"][...]) + w["br"][...]
    lane = jax.lax.broadcasted_iota(I32, (rows, LANES), 1)
    lane_f = lane.astype(F32)
    neg = jnp.full((rows, LANES), -jnp.inf, F32)
    l = jnp.where(lane < N_EXPERTS, logits, neg)
    vals, idxs, sels = [], [], []
    for _ in range(TOP_K):
        mx = jnp.max(l, axis=-1, keepdims=True)
        ix = jnp.min(jnp.where(l == mx, lane_f, float(LANES)), axis=-1, keepdims=True)
        sel = lane_f == ix
        l = jnp.where(sel, neg, l)
        vals.append(mx)
        idxs.append(ix)
        sels.append(sel)
    es = [jnp.exp(v - vals[0]) for v in vals]
    den = es[0] + es[1] + es[2] + es[3]

    onehot = jnp.where(sels[0] | sels[1] | sels[2] | sels[3], 1.0, 0.0)
    r_i = jax.lax.broadcasted_iota(I32, (rows, rows), 0)
    c_i = jax.lax.broadcasted_iota(I32, (rows, rows), 1)
    before = jnp.where(c_i < r_i, 1.0, 0.0).astype(BF16)
    earlier = _dot(before, onehot.astype(BF16))
    count = jnp.sum(onehot, axis=0, keepdims=True)
    new_carry = carry + count
    e_i = jax.lax.broadcasted_iota(I32, (LANES, LANES), 0)
    e_j = jax.lax.broadcasted_iota(I32, (LANES, LANES), 1)
    lower_expert = jnp.where(e_i < e_j, 1.0, 0.0).astype(BF16)
    first = _dot(jnp.broadcast_to(count, (SUBLANES, LANES)).astype(BF16), lower_expert)[0:1, :]

    tw = jnp.zeros((rows, LANES), F32)
    pk = jnp.zeros((rows, LANES), F32)
    for k in range(TOP_K):
        in_tile = jnp.sum(jnp.where(sels[k], earlier, 0.0), axis=-1, keepdims=True)
        prior = jnp.sum(jnp.where(sels[k], carry, 0.0), axis=-1, keepdims=True)
        start = jnp.sum(jnp.where(sels[k], first, 0.0), axis=-1, keepdims=True)
        tw = jnp.where(lane == k, es[k] / den, tw)
        pk = jnp.where(lane == k, (prior + in_tile) * float(N_EXPERTS) + idxs[k], pk)
        pk = jnp.where(lane == TOP_K + k, start + in_tile, pk)
    pk8 = jnp.transpose(pk)[0:SUBLANES, :].astype(I32)
    runs = jnp.concatenate(
        [carry, count, first, jnp.zeros((SUBLANES - 3, LANES), F32)], axis=0).astype(I32)
    return x1, h2, tw, pk8, runs, new_carry


_WEIGHT_NAMES = ("nmix", "win", "bin", "cw", "cb", "lng", "lnb", "wco", "wpool", "pscale", "wout",
                 "nffn", "wr", "br")


def _weight_specs(l, n_grid):
    def spec(*shape):
        zeros = (0,) * len(shape)
        if n_grid == 2:
            return pl.BlockSpec((None,) + shape, lambda b, t: (l,) + zeros)
        return pl.BlockSpec((None,) + shape, lambda i: (l,) + zeros)
    return [
        spec(1, D_MODEL),
        spec(D_MODEL, IN_COLS),
        spec(1, IN_COLS),
        spec(CONV_WIDTH, D_CONV),
        spec(1, D_CONV),
        spec(1, D_CONV),
        spec(1, D_CONV),
        spec(D_CONV, D_MODEL),
        spec(len(POOL_WINDOWS), POOL_GROUP_IN, POOL_GROUP_OUT),
        spec(1, D_MODEL),
        spec(D_MODEL, D_MODEL),
        spec(1, D_MODEL),
        spec(D_MODEL, LANES),
        spec(1, LANES),
    ]


def _prompt_mixer_kernel(x_ref, mod_ref, *refs):
    w = dict(zip(_WEIGHT_NAMES, refs[:len(_WEIGHT_NAMES)]))
    (x1_ref, h2_ref, tw_ref, pk_ref, runs_ref, cnt_ref, nconv_ref, npool_ref,
     uhist, ushift, phist, vbuf, carry) = refs[len(_WEIGHT_NAMES):]

    b = pl.program_id(0)
    t = pl.program_id(1)
    nt = pl.num_programs(1)

    @pl.when((b == 0) & (t == 0))
    def _():
        carry[...] = jnp.zeros((SUBLANES, LANES), F32)

    @pl.when(t == 0)
    def _():
        for q in range(NSEQ):
            uhist[q, 0:CONV_HALO, :] = jnp.zeros((CONV_HALO, D_CONV), F32)
            phist[q, 0:POOL_HALO, :] = jnp.zeros((POOL_HALO, D_POOL), F32)

    gates = []
    for q in range(NSEQ):
        u, up, gc, gp = _in_proj(x_ref[q], mod_ref[0, q], mod_ref[1, q], w["nmix"], w["win"], w["bin"])
        uhist[q, CONV_HALO:CONV_HALO + TT, :] = u
        phist[q, POOL_HALO:POOL_HALO + TT, :] = up
        gates.append((gc, gp))

    def conv_and_pool(q):
        for r in range(1, SUBLANES):
            ushift[q, r - 1] = uhist[q, r:r + CONV_SHIFT_ROWS, :]
        for c in range(TT // CONV_ROWS):
            acc = jnp.zeros((CONV_ROWS, D_CONV), F32)
            for k in range(CONV_WIDTH):
                qq, r = divmod(CONV_HALO - CONV_BUF + k, SUBLANES)
                start = qq * SUBLANES + c * CONV_ROWS
                if r == 0:
                    tap = uhist[q, start:start + CONV_ROWS, :]
                else:
                    tap = ushift[q, r - 1, start:start + CONV_ROWS, :]
                acc = acc + w["cw"][k:k + 1, :] * tap
            s = _conv_act(acc, w["cb"], w["lng"], w["lnb"])
            vbuf[q, c * CONV_ROWS:(c + 1) * CONV_ROWS, :] = s.astype(BF16)

        pos = t * TT + jax.lax.broadcasted_iota(I32, (TT, 1), 0)
        pooled = []
        for g, win in enumerate(POOL_WINDOWS):
            lo, hi = g * POOL_GROUP_IN, (g + 1) * POOL_GROUP_IN
            cur = phist[q, POOL_HALO:POOL_HALO + TT, lo:hi]
            ssum = cur
            for i in range(1, win):
                ssum = ssum + phist[q, POOL_HALO - i:POOL_HALO - i + TT, lo:hi]
            cnt = jnp.minimum(pos + 1, win).astype(F32)
            pooled.append(ssum / cnt - cur)
        return pooled

    cur_carry = carry[0:1, :]
    for q in range(NSEQ):
        pooled = conv_and_pool(q)
        gc, gp = gates[q]
        x1, h2, tw, pk8, runs, cur_carry = _merge_and_route(
            x_ref[q], vbuf[q], pooled, gc, gp, mod_ref[2, q], mod_ref[3, q], mod_ref[4, q], cur_carry, w)
        x1_ref[q] = x1
        _store_token_tiles(h2_ref.at[q], h2)
        tw_ref[q] = tw
        pk_ref[q] = pk8
        runs_ref[q] = runs

        uhist[q, 0:CONV_HALO, :] = uhist[q, TT:TT + CONV_HALO, :]
        phist[q, 0:POOL_HALO, :] = phist[q, TT:TT + POOL_HALO, :]

    carry[...] = jnp.broadcast_to(cur_carry, (SUBLANES, LANES))
    cnt_ref[...] = jnp.broadcast_to(cur_carry, (SUBLANES, LANES))

    @pl.when(t == nt - 1)
    def _():
        for q in range(NSEQ):
            nconv_ref[q] = uhist[q, CONV_HALO + TT - CONV_BUF:CONV_HALO + TT, :]
            npool_ref[q] = phist[q, POOL_HALO + TT - POOL_BUF:POOL_HALO + TT, :]


def _prompt_mixer(l, x, mod, weights, batch, seq):
    nt = seq // TT
    n_tok = batch * seq
    half = n_tok // NSEQ
    row = lambda rows, width: pl.BlockSpec((NSEQ, rows, width), lambda b, t: (0, b * nt + t, 0))
    in_specs = [row(TT, D_MODEL),
                pl.BlockSpec((N_MOD, NSEQ, None, 1, D_MODEL), lambda b, t: (0, 0, b, 0, 0))]
    in_specs += _weight_specs(l, 2)
    out_shape = (
        jax.ShapeDtypeStruct((NSEQ, half, D_MODEL), F32),
        jax.ShapeDtypeStruct((NSEQ, half * ROW_TILE, LANES), F32),
        jax.ShapeDtypeStruct((NSEQ, half, LANES), F32),
        jax.ShapeDtypeStruct((NSEQ, SUBLANES, half), I32),
        jax.ShapeDtypeStruct((NSEQ, half // TT * SUBLANES, LANES), I32),
        jax.ShapeDtypeStruct((SUBLANES, LANES), F32),
        jax.ShapeDtypeStruct((NSEQ, batch // NSEQ, CONV_BUF, D_CONV), F32),
        jax.ShapeDtypeStruct((NSEQ, batch // NSEQ, POOL_BUF, D_POOL), F32),
    )
    out_specs = (
        row(TT, D_MODEL), row(TT * ROW_TILE, LANES), row(TT, LANES),
        pl.BlockSpec((NSEQ, SUBLANES, TT), lambda b, t: (0, 0, b * nt + t)),
        row(SUBLANES, LANES),
        pl.BlockSpec((SUBLANES, LANES), lambda b, t: (0, 0)),
        pl.BlockSpec((NSEQ, None, CONV_BUF, D_CONV), lambda b, t: (0, b, 0, 0)),
        pl.BlockSpec((NSEQ, None, POOL_BUF, D_POOL), lambda b, t: (0, b, 0, 0)),
    )
    x1, h2, tw, pk8, runs, cnt, nconv, npool = pl.pallas_call(
        _prompt_mixer_kernel,
        grid=(batch // NSEQ, nt),
        in_specs=in_specs,
        out_specs=out_specs,
        out_shape=out_shape,
        scratch_shapes=[
            pltpu.VMEM((NSEQ, CONV_HALO + TT, D_CONV), F32),
            pltpu.VMEM((NSEQ, SUBLANES - 1, CONV_SHIFT_ROWS, D_CONV), F32),
            pltpu.VMEM((NSEQ, POOL_HALO + TT, D_POOL), F32),
            pltpu.VMEM((NSEQ, TT, D_CONV), BF16),
            pltpu.VMEM((SUBLANES, LANES), F32),
        ],
        compiler_params=pltpu.CompilerParams(
            dimension_semantics=("arbitrary", "arbitrary"), vmem_limit_bytes=VMEM_LIMIT),
        name=f"prompt_mixer_l{l}",
    )(x.reshape(NSEQ, half, D_MODEL), mod.reshape(N_MOD, NSEQ, batch // NSEQ, 1, D_MODEL), *weights)
    return (x1.reshape(n_tok, D_MODEL), h2.reshape(n_tok * ROW_TILE, LANES), tw.reshape(n_tok, LANES),
            jnp.transpose(pk8, (1, 0, 2)).reshape(SUBLANES, n_tok),
            runs.reshape(n_tok // TT * SUBLANES, LANES), cnt,
            nconv.reshape(batch, CONV_BUF, D_CONV), npool.reshape(batch, POOL_BUF, D_POOL))


def _sample_mixer_kernel(x_ref, mod_ref, cs_ref, ps_ref, cnt0_ref, *refs, dec_seq, pos0):
    w = dict(zip(_WEIGHT_NAMES, refs[:len(_WEIGHT_NAMES)]))
    (x1_ref, h2_ref, tw_ref, pk_ref, runs_ref, cnt_ref, nconv_ref, npool_ref,
     ufull, pfull, vbuf, carry) = refs[len(_WEIGHT_NAMES):]

    def per_row(v):
        return jnp.concatenate([v] * dec_seq, axis=0)

    @pl.when(pl.program_id(0) == 0)
    def _():
        carry[...] = cnt0_ref[...]

    x = x_ref[...]
    sh1, sc1, gt1 = per_row(mod_ref[0]), per_row(mod_ref[1]), per_row(mod_ref[2])
    sh2, sc2 = per_row(mod_ref[3]), per_row(mod_ref[4])

    u, up, gc, gp = _in_proj(x, sh1, sc1, w["nmix"], w["win"], w["bin"])
    ufull[0:CONV_BUF] = cs_ref[...]
    pfull[0:POOL_BUF] = ps_ref[...]
    for j in range(dec_seq):
        ufull[CONV_BUF + j] = u[j * SB:(j + 1) * SB, :]
        pfull[POOL_BUF + j] = up[j * SB:(j + 1) * SB, :]
    nconv_ref[...] = ufull[dec_seq:dec_seq + CONV_BUF]
    npool_ref[...] = pfull[dec_seq:dec_seq + POOL_BUF]

    for j in range(dec_seq):
        acc = jnp.zeros((SB, D_CONV), F32)
        for k in range(CONV_WIDTH):
            acc = acc + w["cw"][k:k + 1, :] * ufull[j + k]
        s = _conv_act(acc, w["cb"], w["lng"], w["lnb"])
        vbuf[j * SB:(j + 1) * SB, :] = s.astype(BF16)

    pooled = []
    for g, win in enumerate(POOL_WINDOWS):
        lo, hi = g * POOL_GROUP_IN, (g + 1) * POOL_GROUP_IN
        parts = []
        for j in range(dec_seq):
            cur = pfull[POOL_BUF + j, :, lo:hi]
            ssum = cur
            for i in range(1, win):
                ssum = ssum + pfull[POOL_BUF + j - i, :, lo:hi]
            cnt = float(min(pos0 + j + 1, win))
            parts.append(ssum / cnt - cur)
        pooled.append(jnp.concatenate(parts, axis=0))

    x1, h2, tw, pk8, runs, new_carry = _merge_and_route(
        x, vbuf[...], pooled, gc, gp, gt1, sh2, sc2, carry[0:1, :], w)
    x1_ref[...] = x1
    _store_token_tiles(h2_ref, h2)
    tw_ref[...] = tw
    pk_ref[...] = pk8
    runs_ref[...] = runs
    carry[...] = jnp.broadcast_to(new_carry, (SUBLANES, LANES))
    cnt_ref[...] = jnp.broadcast_to(new_carry, (SUBLANES, LANES))


def _sample_mixer(l, x, mod, conv_state, pool_state, cnt0, weights, dec_batch, dec_seq, pos0):
    rows = dec_seq * SB
    n_s = dec_batch * dec_seq
    row = lambda width: pl.BlockSpec((rows, width), lambda i: (i, 0))
    conv_spec = pl.BlockSpec((None, CONV_BUF, SB, D_CONV), lambda i: (i, 0, 0, 0))
    pool_spec = pl.BlockSpec((None, POOL_BUF, SB, D_POOL), lambda i: (i, 0, 0, 0))
    cnt_spec = pl.BlockSpec((SUBLANES, LANES), lambda i: (0, 0))
    in_specs = [row(D_MODEL), pl.BlockSpec((N_MOD, SB, D_MODEL), lambda i: (0, i, 0)),
                conv_spec, pool_spec, cnt_spec]
    in_specs += _weight_specs(l, 1)
    out_shape = (
        jax.ShapeDtypeStruct((n_s, D_MODEL), F32),
        jax.ShapeDtypeStruct((n_s * ROW_TILE, LANES), F32),
        jax.ShapeDtypeStruct((n_s, LANES), F32),
        jax.ShapeDtypeStruct((SUBLANES, n_s), I32),
        jax.ShapeDtypeStruct((n_s // rows * SUBLANES, LANES), I32),
        jax.ShapeDtypeStruct((SUBLANES, LANES), F32),
        jax.ShapeDtypeStruct(conv_state.shape, F32),
        jax.ShapeDtypeStruct(pool_state.shape, F32),
    )
    out_specs = (row(D_MODEL), pl.BlockSpec((rows * ROW_TILE, LANES), lambda i: (i, 0)), row(LANES),
                 pl.BlockSpec((SUBLANES, rows), lambda i: (0, i)),
                 pl.BlockSpec((SUBLANES, LANES), lambda i: (i, 0)), cnt_spec, conv_spec, pool_spec)
    return pl.pallas_call(
        functools.partial(_sample_mixer_kernel, dec_seq=dec_seq, pos0=pos0),
        grid=(dec_batch // SB,),
        in_specs=in_specs,
        out_specs=out_specs,
        out_shape=out_shape,
        scratch_shapes=[
            pltpu.VMEM((CONV_BUF + dec_seq, SB, D_CONV), F32),
            pltpu.VMEM((POOL_BUF + dec_seq, SB, D_POOL), F32),
            pltpu.VMEM((rows, D_CONV), BF16),
            pltpu.VMEM((SUBLANES, LANES), F32),
        ],
        compiler_params=pltpu.CompilerParams(
            dimension_semantics=("arbitrary",), vmem_limit_bytes=VMEM_LIMIT),
        name=f"sample_mixer_l{l}",
    )(x, mod, conv_state, pool_state, cnt0, *weights)


RUN_ROWS = 3


def _table_copies(pk_hbm, runs_hbm, tile, pos, runs, slot, sem):
    copies = [
        pltpu.make_async_copy(pk_hbm.at[TOP_K + k, pl.ds(tile * TT, TT)],
                              pos.at[pl.ds((slot * TOP_K + k) * TT, TT)], sem)
        for k in range(TOP_K)]
    copies += [
        pltpu.make_async_copy(runs_hbm.at[tile * SUBLANES + r],
                              runs.at[pl.ds((slot * RUN_ROWS + r) * LANES, LANES)], sem)
        for r in range(RUN_ROWS)]
    return copies


def _run_copies(pstart_ref, runs, slot, sorted_hbm, run_ref, sem, to_hbm):
    def body(e, carry):
        count = runs[(slot * RUN_ROWS + 1) * LANES + e]

        @pl.when(count > 0)
        def _():
            glob = pl.multiple_of((pstart_ref[e] + runs[slot * RUN_ROWS * LANES + e]) * ROW_TILE, ROW_TILE)
            loc = pl.multiple_of(runs[(slot * RUN_ROWS + 2) * LANES + e] * ROW_TILE, ROW_TILE)
            hbm = sorted_hbm.at[pl.ds(glob, count * ROW_TILE), :]
            vmem = run_ref.at[pl.ds(loc, count * ROW_TILE), :]
            if to_hbm:
                pltpu.make_async_copy(vmem, hbm, sem).start()
            else:
                pltpu.make_async_copy(hbm, vmem, sem).start()
        return carry
    jax.lax.fori_loop(0, N_EXPERTS, body, 0)


def _dispatch_kernel(pstart_ref, pk_hbm, runs_hbm, h2p_ref, h2s_ref, xs_hbm, xrun0, xrun1, pos, runs,
                     isem, rsem, *, n_prompt_tiles):
    c = pl.program_id(0)
    n = pl.num_programs(0)
    xrun = (xrun0, xrun1)

    def tables(tile, slot):
        return _table_copies(pk_hbm, runs_hbm, tile, pos, runs, slot, isem.at[slot])

    def wait_runs(slot):
        pltpu.make_async_copy(xrun[slot], xs_hbm.at[pl.ds(0, TOP_K * TT * ROW_TILE), :], rsem.at[slot]).wait()

    @pl.when(c == 0)
    def _():
        for cp in tables(0, 0):
            cp.start()

    for s in range(2):
        @pl.when(c % 2 == s)
        def _():
            for cp in tables(c, s):
                cp.wait()

            @pl.when(c + 1 < n)
            def _():
                for cp in tables(c + 1, 1 - s):
                    cp.start()

            @pl.when(c >= 2)
            def _():
                wait_runs(s)

            def place_from(src_ref):
                def place(t, carry):
                    for k in range(TOP_K):
                        p = pos[(s * TOP_K + k) * TT + t]
                        _token_tile(xrun[s], p)[...] = _token_tile(src_ref, t)[...]
                    return carry
                jax.lax.fori_loop(0, TT, place, 0, unroll=4)

            @pl.when(c < n_prompt_tiles)
            def _():
                place_from(h2p_ref)

            @pl.when(c >= n_prompt_tiles)
            def _():
                place_from(h2s_ref)

            _run_copies(pstart_ref, runs, s, xs_hbm, xrun[s], rsem.at[s], to_hbm=True)

            @pl.when(c == n - 1)
            def _():
                @pl.when(c >= 1)
                def _():
                    wait_runs(1 - s)
                wait_runs(s)


def _dispatch(pstart, pk8, runs, h2p, h2s, n_slots):
    np_tiles = h2p.shape[0] // (TT * ROW_TILE)
    ns_tiles = h2s.shape[0] // (TT * ROW_TILE)
    tiles = pltpu.VMEM((TOP_K * TT * ROW_TILE, LANES), F32)
    grid_spec = pltpu.PrefetchScalarGridSpec(
        num_scalar_prefetch=1,
        grid=(np_tiles + ns_tiles,),
        in_specs=[
            pl.BlockSpec(memory_space=pl.ANY),
            pl.BlockSpec(memory_space=pl.ANY),
            pl.BlockSpec((TT * ROW_TILE, LANES), lambda c, ps: (jnp.minimum(c, np_tiles - 1), 0)),
            pl.BlockSpec((TT * ROW_TILE, LANES), lambda c, ps: (jnp.maximum(c - np_tiles, 0), 0)),
        ],
        out_specs=pl.BlockSpec(memory_space=pl.ANY),
        scratch_shapes=[
            tiles, tiles,
            pltpu.SMEM((2 * TOP_K * TT,), I32),
            pltpu.SMEM((2 * RUN_ROWS * LANES,), I32),
            pltpu.SemaphoreType.DMA((2,)),
            pltpu.SemaphoreType.DMA((2,)),
        ],
    )
    return pl.pallas_call(
        functools.partial(_dispatch_kernel, n_prompt_tiles=np_tiles),
        grid_spec=grid_spec,
        out_shape=jax.ShapeDtypeStruct((n_slots * ROW_TILE, LANES), F32),
        compiler_params=pltpu.CompilerParams(
            dimension_semantics=("arbitrary",), vmem_limit_bytes=VMEM_LIMIT),
        name="dispatch",
    )(pstart, pk8, runs, h2p, h2s)


def _experts_kernel(be_ref, nv_ref, nu_ref, nx_ref, x_ref, wup_hbm, bup_ref, wdn_hbm, bdn_ref, o_ref,
                    wup_f32, wdn_f32, wup_bf, wdn_bf, wsem, *, layer):
    b = pl.program_id(0)

    def weight_copies(e):
        return (pltpu.make_async_copy(wup_hbm.at[layer, e], wup_f32, wsem.at[0]),
                pltpu.make_async_copy(wdn_hbm.at[layer, e], wdn_f32, wsem.at[1]))

    @pl.when(b < nu_ref[0])
    def _():
        e = be_ref[b]
        prev = be_ref[jnp.maximum(b - 1, 0)]

        @pl.when((b == 0) | (e != prev))
        def _():
            @pl.when(b == 0)
            def _():
                for cp in weight_copies(e):
                    cp.start()
            for cp in weight_copies(e):
                cp.wait()
            for r in range(0, D_MODEL, LANES):
                wup_bf[r:r + LANES, :] = wup_f32[r:r + LANES, :].astype(BF16)
                wdn_bf[r:r + LANES, :] = wdn_f32[r:r + LANES, :].astype(BF16)
            nxt = nx_ref[e]

            @pl.when(nxt >= 0)
            def _():
                for cp in weight_copies(nxt):
                    cp.start()

        n_valid = nv_ref[b]

        def ffn(rows):
            rid = jax.lax.broadcasted_iota(I32, (rows, D_MODEL), 0)
            x = jnp.where(rid < n_valid, _load_token_tiles(x_ref, 0, rows), 0.0)
            a = _dot(x.astype(BF16), wup_bf[...]) + bup_ref[...]
            a_glu = jnp.minimum(a[:, :D_EXPERT], SWIGLU_LIMIT)
            a_lin = jnp.clip(a[:, D_EXPERT:], -SWIGLU_LIMIT, SWIGLU_LIMIT)
            o = a_glu * jax.nn.sigmoid(SWIGLU_ALPHA * a_glu) * (a_lin + 1.0)
            _store_token_tiles(o_ref, _dot(o.astype(BF16), wdn_bf[...]) + bdn_ref[...])
            if rows < BM:
                o_ref[rows * ROW_TILE:, :] = jnp.zeros(((BM - rows) * ROW_TILE, LANES), F32)

        @pl.when(n_valid > BM // 2)
        def _():
            ffn(BM)

        @pl.when(n_valid <= BM // 2)
        def _():
            ffn(BM // 2)

    @pl.when(b >= nu_ref[0])
    def _():
        o_ref[...] = jnp.zeros((BM * ROW_TILE, LANES), F32)


def _experts(l, x_sorted, block_expert, n_valid, n_used, next_expert, w_up, b_up, w_down, b_down):
    n_slots = x_sorted.shape[0] // ROW_TILE
    nb = n_slots // BM

    def used_map(b, be, nv, nu, nx):
        return (jnp.minimum(b, nu[0] - 1), 0)

    bmap = lambda b, be, nv, nu, nx: (l, be[b], 0, 0)
    grid_spec = pltpu.PrefetchScalarGridSpec(
        num_scalar_prefetch=4,
        grid=(nb,),
        in_specs=[
            pl.BlockSpec((BM * ROW_TILE, LANES), used_map),
            pl.BlockSpec(memory_space=pl.ANY),
            pl.BlockSpec((None, None, 1, 2 * D_EXPERT), bmap),
            pl.BlockSpec(memory_space=pl.ANY),
            pl.BlockSpec((None, None, 1, D_MODEL), bmap),
        ],
        out_specs=pl.BlockSpec((BM * ROW_TILE, LANES), lambda b, be, nv, nu, nx: (b, 0)),
        scratch_shapes=[
            pltpu.VMEM((D_MODEL, 2 * D_EXPERT), F32),
            pltpu.VMEM((D_EXPERT, D_MODEL), F32),
            pltpu.VMEM((D_MODEL, 2 * D_EXPERT), BF16),
            pltpu.VMEM((D_EXPERT, D_MODEL), BF16),
            pltpu.SemaphoreType.DMA((2,)),
        ],
    )
    return pl.pallas_call(
        functools.partial(_experts_kernel, layer=l),
        grid_spec=grid_spec,
        out_shape=jax.ShapeDtypeStruct((n_slots * ROW_TILE, LANES), F32),
        compiler_params=pltpu.CompilerParams(
            dimension_semantics=("arbitrary",), vmem_limit_bytes=VMEM_LIMIT),
        name=f"experts_l{l}",
    )(block_expert, n_valid, n_used, next_expert, x_sorted, w_up,
      b_up.reshape(DEPTH, N_EXPERTS, 1, 2 * D_EXPERT), w_down,
      b_down.reshape(DEPTH, N_EXPERTS, 1, D_MODEL))


def _tables_kernel(cnt_ref, pstart_ref, be_ref, nv_ref, nu_ref, nx_ref, *, nb):
    def clear(b, c):
        be_ref[b] = N_EXPERTS - 1
        nv_ref[b] = 0
        return c
    jax.lax.fori_loop(0, nb, clear, 0)

    def per_expert(e, carry):
        blk, last = carry
        cnt = cnt_ref[e]
        n = (cnt + (BM - 1)) // BM
        pstart_ref[e] = blk * BM
        nx_ref[e] = -1

        def per_block(j, c):
            be_ref[blk + j] = e
            nv_ref[blk + j] = jnp.minimum(cnt - j * BM, BM)
            return c
        jax.lax.fori_loop(0, n, per_block, 0)

        @pl.when((n > 0) & (last >= 0))
        def _():
            nx_ref[last] = e
        return blk + n, jnp.where(n > 0, e, last)

    blk, _ = jax.lax.fori_loop(0, N_EXPERTS, per_expert, (jnp.int32(0), jnp.int32(-1)))
    nu_ref[0] = blk


def _block_tables(counts, nb):
    smem = pl.BlockSpec(memory_space=pltpu.SMEM)
    return pl.pallas_call(
        functools.partial(_tables_kernel, nb=nb),
        in_specs=[smem],
        out_specs=(smem, smem, smem, smem, smem),
        out_shape=(
            jax.ShapeDtypeStruct((N_EXPERTS,), I32),
            jax.ShapeDtypeStruct((nb,), I32),
            jax.ShapeDtypeStruct((nb,), I32),
            jax.ShapeDtypeStruct((1,), I32),
            jax.ShapeDtypeStruct((N_EXPERTS,), I32),
        ),
        name="block_tables",
    )(counts)


def _combine_kernel(pstart_ref, pk_hbm, runs_hbm, y_hbm, x1_ref, tw_ref, gt_ref, g_ref, o_ref,
                    yrun0, yrun1, ybuf, pos, runs, isem, rsem, *, gate_copies, final):
    i = pl.program_id(0)
    n = pl.num_programs(0)
    yrun = (yrun0, yrun1)

    def tables(tile, slot):
        return _table_copies(pk_hbm, runs_hbm, tile, pos, runs, slot, isem.at[slot])

    @pl.when(i == 0)
    def _():
        for cp in tables(0, 0):
            cp.start()
        for cp in tables(0, 0):
            cp.wait()
        _run_copies(pstart_ref, runs, 0, y_hbm, yrun[0], rsem.at[0], to_hbm=False)

        @pl.when(n > 1)
        def _():
            for cp in tables(1, 1):
                cp.start()

    for s in range(2):
        @pl.when(i % 2 == s)
        def _():
            @pl.when(i + 1 < n)
            def _():
                for cp in tables(i + 1, 1 - s):
                    cp.wait()
                _run_copies(pstart_ref, runs, 1 - s, y_hbm, yrun[1 - s], rsem.at[1 - s], to_hbm=False)

            pltpu.make_async_copy(y_hbm.at[pl.ds(0, TOP_K * TT * ROW_TILE), :], yrun[s], rsem.at[s]).wait()

            def place(t, carry):
                for k in range(TOP_K):
                    p = pos[(s * TOP_K + k) * TT + t]
                    _token_tile(ybuf, k * TT + t)[...] = _token_tile(yrun[s], p)[...]
                return carry
            jax.lax.fori_loop(0, TT, place, 0, unroll=4)

            @pl.when(i + 2 < n)
            def _():
                for cp in tables(i + 2, s):
                    cp.start()

            tw = tw_ref[...]
            moe = tw[:, 0:1] * _load_token_tiles(ybuf, 0, TT)
            for k in range(1, TOP_K):
                moe = moe + tw[:, k:k + 1] * _load_token_tiles(ybuf, k * TT, TT)
            gt = gt_ref[...]
            if gate_copies > 1:
                gt = jnp.concatenate([gt] * gate_copies, axis=0)
            x = x1_ref[...] + gt * moe
            o_ref[...] = _rms(x, g_ref[...]) if final else x


def _combine(pstart, pk8, runs, y_sorted, x1, tw, gate, norm_final, *, gate_spec, gate_copies, final, name):
    n = x1.shape[0]
    row = lambda width: pl.BlockSpec((TT, width), lambda i, ps: (i, 0))
    tiles = pltpu.VMEM((TOP_K * TT * ROW_TILE, LANES), F32)
    grid_spec = pltpu.PrefetchScalarGridSpec(
        num_scalar_prefetch=1,
        grid=(n // TT,),
        in_specs=[
            pl.BlockSpec(memory_space=pl.ANY),
            pl.BlockSpec(memory_space=pl.ANY),
            pl.BlockSpec(memory_space=pl.ANY),
            row(D_MODEL), row(LANES), gate_spec,
            pl.BlockSpec((1, D_MODEL), lambda i, ps: (0, 0)),
        ],
        out_specs=row(D_MODEL),
        scratch_shapes=[
            tiles, tiles, tiles,
            pltpu.SMEM((2 * TOP_K * TT,), I32),
            pltpu.SMEM((2 * RUN_ROWS * LANES,), I32),
            pltpu.SemaphoreType.DMA((2,)),
            pltpu.SemaphoreType.DMA((2,)),
        ],
    )
    return pl.pallas_call(
        functools.partial(_combine_kernel, gate_copies=gate_copies, final=final),
        grid_spec=grid_spec,
        out_shape=jax.ShapeDtypeStruct((n, D_MODEL), F32),
        compiler_params=pltpu.CompilerParams(
            dimension_semantics=("arbitrary",), vmem_limit_bytes=VMEM_LIMIT),
        name=name,
    )(pstart, pk8, runs, y_sorted, x1, tw, gate, norm_final.reshape(1, D_MODEL))


def kernel(x_prompt, x_sample, state_conv, state_pool, c_prompt, c_sample, w_ada, b_ada, norm_mix, w_in, b_in, conv_w, conv_b, conv_ln_g, conv_ln_b, w_conv_out, w_pool, pool_scale, w_out, norm_ffn, w_router, b_router, w_up, b_up, w_down, b_down, norm_final):
    batch, seq, _ = x_prompt.shape
    dec_batch, dec_seq, _ = x_sample.shape
    n_p = batch * seq
    n_s = dec_batch * dec_seq
    n_blocks = dec_batch // SB
    assert dec_seq * SB == TT and seq % TT == 0 and dec_batch % SB == 0
    n_assign = (n_p + n_s) * TOP_K
    nb = n_assign // BM + N_EXPERTS

    vec = lambda a: a.reshape(DEPTH, 1, a.shape[-1])
    weights = (
        vec(norm_mix), w_in.astype(BF16), vec(b_in), conv_w, vec(conv_b), vec(conv_ln_g),
        vec(conv_ln_b), w_conv_out.astype(BF16), w_pool.astype(BF16), vec(pool_scale),
        w_out.astype(BF16), vec(norm_ffn),
        jnp.pad(w_router, ((0, 0), (0, 0), (0, LANES - N_EXPERTS))).astype(BF16),
        vec(jnp.pad(b_router, ((0, 0), (0, LANES - N_EXPERTS)))),
    )

    mod = _adaln(jnp.concatenate([c_prompt, c_sample], axis=0), w_ada, b_ada)
    mod_p = mod[:, :, :batch]
    mod_s = mod[:, :, batch:]

    def to_sample_order(a, rows):
        return jnp.transpose(a.reshape(n_blocks, SB, rows, a.shape[-1]), (0, 2, 1, 3))

    def from_sample_order(a):
        return jnp.transpose(a, (0, 2, 1, 3)).reshape(dec_batch, a.shape[1], a.shape[-1])

    xp = x_prompt.reshape(n_p, D_MODEL)
    xs = to_sample_order(x_sample, dec_seq).reshape(n_s, D_MODEL)

    gate_p = pl.BlockSpec((None, 1, D_MODEL), lambda i, ps: (i // (seq // TT), 0, 0))
    gate_s = pl.BlockSpec((SB, D_MODEL), lambda i, ps: (i, 0))

    new_conv_p, new_pool_p, new_conv_s, new_pool_s = [], [], [], []
    for l in range(DEPTH):
        x1p, h2p, twp, pkp, runsp, cntp, ncp, npp = _prompt_mixer(l, xp, mod_p[l], weights, batch, seq)
        x1s, h2s, tws, pks, runss, cnt, ncs, nps = _sample_mixer(
            l, xs, mod_s[l], to_sample_order(state_conv[l], CONV_BUF),
            to_sample_order(state_pool[l], POOL_BUF), cntp, weights, dec_batch, dec_seq, PAST_LEN)
        new_conv_p.append(ncp)
        new_pool_p.append(npp)
        new_conv_s.append(from_sample_order(ncs))
        new_pool_s.append(from_sample_order(nps))

        counts = cnt[0, :N_EXPERTS].astype(I32)
        pstart, block_expert, n_valid, n_used, next_expert = _block_tables(counts, nb)
        x_sorted = _dispatch(pstart, jnp.concatenate([pkp, pks], axis=1),
                             jnp.concatenate([runsp, runss], axis=0), h2p, h2s, nb * BM)
        y_sorted = _experts(l, x_sorted, block_expert, n_valid, n_used, next_expert, w_up, b_up, w_down,
                            b_down)

        final = l == DEPTH - 1
        xp = _combine(pstart, pkp, runsp, y_sorted, x1p, twp, mod_p[l][5][:, None, :], norm_final,
                      gate_spec=gate_p, gate_copies=1, final=final, name=f"combine_prompt_l{l}")
        xs = _combine(pstart, pks, runss, y_sorted, x1s, tws, mod_s[l][5], norm_final,
                      gate_spec=gate_s, gate_copies=dec_seq, final=final, name=f"combine_sample_l{l}")

    y_prompt = xp.reshape(batch, seq, D_MODEL)
    y_sample = from_sample_order(xs.reshape(n_blocks, dec_seq, SB, D_MODEL))
    return (y_prompt, y_sample, jnp.stack(new_conv_p), jnp.stack(new_pool_p),
            jnp.stack(new_conv_s), jnp.stack(new_pool_s))
```

```python
import functools

import jax
import jax.numpy as jnp
from jax.experimental import pallas as pl
from jax.experimental.pallas import tpu as pltpu

F32 = jnp.float32
BF16 = jnp.bfloat16
I32 = jnp.int32

D_MODEL = 1024
DEPTH = 2
D_CONV = 512
CONV_WIDTH = 31
CONV_BUF = CONV_WIDTH - 1
D_POOL = 512
POOL_WINDOWS = (2, 4, 8, 16)
POOL_GROUP_IN = D_POOL // len(POOL_WINDOWS)
POOL_GROUP_OUT = D_MODEL // len(POOL_WINDOWS)
POOL_BUF = max(POOL_WINDOWS) - 1
IN_COLS = 2 * D_CONV + D_POOL + 2 * D_MODEL
N_EXPERTS = 32
TOP_K = 4
D_EXPERT = D_MODEL
SWIGLU_LIMIT = 7.0
SWIGLU_ALPHA = 1.702
N_MOD = 6
EPS = 1e-6
PAST_LEN = 16384

LANES = 128
SUBLANES = 8
VMEM_LIMIT = 52 * 1024 * 1024

TT = 256
NSEQ = 2
CONV_HALO = 32
POOL_HALO = 16
CONV_ROWS = 64
SB = 64
BM = 512
ROW_TILE = D_MODEL // LANES
EXPERT_BITS = 5
CONV_SHIFT_ROWS = TT + CONV_HALO - SUBLANES
assert N_EXPERTS == 1 << EXPERT_BITS and ROW_TILE == SUBLANES


def _rms(x, g):
    return x * jax.lax.rsqrt(jnp.mean(x * x, axis=-1, keepdims=True) + EPS) * g


def _dot(a, b):
    return jnp.dot(a, b, preferred_element_type=F32)


def _store_token_tiles(ref, v):
    rows = v.shape[0]
    for s in range(ROW_TILE):
        ref[pl.ds(s, rows, stride=ROW_TILE), :] = v[:, s * LANES:(s + 1) * LANES]


def _load_token_tiles(ref, first_token, rows):
    return jnp.concatenate(
        [ref[pl.ds(first_token * ROW_TILE + s, rows, stride=ROW_TILE), :] for s in range(ROW_TILE)], axis=-1)


def _tile_at_row(ref, row):
    return ref.at[pl.ds(pl.multiple_of(row, ROW_TILE), ROW_TILE), :]


def _token_tile(ref, t):
    return _tile_at_row(ref, t * ROW_TILE)


def _adaln_kernel(c_ref, w_ref, b_ref, o_ref):
    c = c_ref[...]
    a = (c * jax.nn.sigmoid(c)).astype(BF16)
    o_ref[...] = _dot(a, w_ref[...].astype(BF16)) + b_ref[...]


def _adaln(c_all, w_ada, b_ada):
    n = c_all.shape[0]
    cols = N_MOD * D_MODEL
    return pl.pallas_call(
        _adaln_kernel,
        grid=(DEPTH, N_MOD),
        in_specs=[
            pl.BlockSpec((n, D_MODEL), lambda l, j: (0, 0)),
            pl.BlockSpec((None, D_MODEL, D_MODEL), lambda l, j: (l, 0, j)),
            pl.BlockSpec((None, 1, D_MODEL), lambda l, j: (l, 0, j)),
        ],
        out_specs=pl.BlockSpec((None, None, n, D_MODEL), lambda l, j: (l, j, 0, 0)),
        out_shape=jax.ShapeDtypeStruct((DEPTH, N_MOD, n, D_MODEL), F32),
        compiler_params=pltpu.CompilerParams(
            dimension_semantics=("arbitrary", "arbitrary"), vmem_limit_bytes=VMEM_LIMIT),
        name="adaln",
    )(c_all, w_ada, b_ada.reshape(DEPTH, 1, cols))


def _in_proj(x, sh1, sc1, nmix_ref, win_ref, bin_ref):
    h = (_rms(x, nmix_ref[...]) * (1.0 + sc1) + sh1).astype(BF16)
    c0, c1, c2 = 2 * D_CONV, 2 * D_CONV + D_POOL, IN_COLS
    zu = _dot(h, win_ref[:, 0:c0]) + bin_ref[:, 0:c0]
    u = zu[:, :D_CONV] * jax.nn.sigmoid(zu[:, D_CONV:])
    up = _dot(h, win_ref[:, c0:c1]) + bin_ref[:, c0:c1]
    zg = _dot(h, win_ref[:, c1:c2]) + bin_ref[:, c1:c2]
    return u, up, zg[:, :D_MODEL], zg[:, D_MODEL:]


def _conv_act(acc, cb_ref, lng_ref, lnb_ref):
    v = acc + cb_ref[...]
    mu = jnp.mean(v, axis=-1, keepdims=True)
    d = v - mu
    var = jnp.mean(d * d, axis=-1, keepdims=True)
    vn = d * jax.lax.rsqrt(var + EPS) * lng_ref[...] + lnb_ref[...]
    return vn * jax.nn.sigmoid(vn)


def _merge_and_route(x, v_bf, pooled, gc, gp, gt1, sh2, sc2, carry, w):
    rows = x.shape[0]
    y_conv = _dot(v_bf, w["wco"][...])
    y_pool = jnp.concatenate(
        [_dot(pooled[g].astype(BF16), w["wpool"][g]) for g in range(len(POOL_WINDOWS))], axis=-1)
    y_pool = y_pool * w["pscale"][...]
    m = jax.nn.sigmoid(gc) * y_conv + jax.nn.sigmoid(gp) * y_pool
    x1 = x + gt1 * _dot(m.astype(BF16), w["wout"][...])
    h2 = _rms(x1, w["nffn"][...]) * (1.0 + sc2) + sh2
    logits = _dot(h2.astype(BF16), w["wr"][...]) + w["br"][...]
    lane = jax.lax.broadcasted_iota(I32, (rows, LANES), 1)
    lane_f = lane.astype(F32)
    neg = jnp.full((rows, LANES), -jnp.inf, F32)
    l = jnp.where(lane < N_EXPERTS, logits, neg)
    vals, idxs, sels = [], [], []
    for _ in range(TOP_K):
        mx = jnp.max(l, axis=-1, keepdims=True)
        ix = jnp.min(jnp.where(l == mx, lane_f, float(LANES)), axis=-1, keepdims=True)
        sel = lane_f == ix
        l = jnp.where(sel, neg, l)
        vals.append(mx)
        idxs.append(ix)
        sels.append(sel)
    es = [jnp.exp(v - vals[0]) for v in vals]
    den = es[0] + es[1] + es[2] + es[3]

    onehot = jnp.where(sels[0] | sels[1] | sels[2] | sels[3], 1.0, 0.0)
    r_i = jax.lax.broadcasted_iota(I32, (rows, rows), 0)
    c_i = jax.lax.broadcasted_iota(I32, (rows, rows), 1)
    before = jnp.where(c_i < r_i, 1.0, 0.0).astype(BF16)
    earlier = _dot(before, onehot.astype(BF16))
    count = jnp.sum(onehot, axis=0, keepdims=True)
    new_carry = carry + count
    e_i = jax.lax.broadcasted_iota(I32, (LANES, LANES), 0)
    e_j = jax.lax.broadcasted_iota(I32, (LANES, LANES), 1)
    lower_expert = jnp.where(e_i < e_j, 1.0, 0.0).astype(BF16)
    first = _dot(jnp.broadcast_to(count, (SUBLANES, LANES)).astype(BF16), lower_expert)[0:1, :]

    tw = jnp.zeros((rows, LANES), F32)
    pk = jnp.zeros((rows, LANES), F32)
    for k in range(TOP_K):
        in_tile = jnp.sum(jnp.where(sels[k], earlier, 0.0), axis=-1, keepdims=True)
        prior = jnp.sum(jnp.where(sels[k], carry, 0.0), axis=-1, keepdims=True)
        start = jnp.sum(jnp.where(sels[k], first, 0.0), axis=-1, keepdims=True)
        tw = jnp.where(lane == k, es[k] / den, tw)
        pk = jnp.where(lane == k, (prior + in_tile) * float(N_EXPERTS) + idxs[k], pk)
        pk = jnp.where(lane == TOP_K + k, (start + in_tile) * float(ROW_TILE), pk)
    pk8 = jnp.transpose(pk)[0:SUBLANES, :].astype(I32)
    runs = jnp.concatenate(
        [carry, count, first, jnp.zeros((SUBLANES - 3, LANES), F32)], axis=0).astype(I32)
    return x1, h2, tw, pk8, runs, new_carry


_WEIGHT_NAMES = ("nmix", "win", "bin", "cw", "cb", "lng", "lnb", "wco", "wpool", "pscale", "wout",
                 "nffn", "wr", "br")


def _weight_specs(l, n_grid):
    def spec(*shape):
        zeros = (0,) * len(shape)
        if n_grid == 2:
            return pl.BlockSpec((None,) + shape, lambda b, t: (l,) + zeros)
        return pl.BlockSpec((None,) + shape, lambda i: (l,) + zeros)
    return [
        spec(1, D_MODEL),
        spec(D_MODEL, IN_COLS),
        spec(1, IN_COLS),
        spec(CONV_WIDTH, D_CONV),
        spec(1, D_CONV),
        spec(1, D_CONV),
        spec(1, D_CONV),
        spec(D_CONV, D_MODEL),
        spec(len(POOL_WINDOWS), POOL_GROUP_IN, POOL_GROUP_OUT),
        spec(1, D_MODEL),
        spec(D_MODEL, D_MODEL),
        spec(1, D_MODEL),
        spec(D_MODEL, LANES),
        spec(1, LANES),
    ]


def _prompt_mixer_kernel(x_ref, mod_ref, *refs):
    w = dict(zip(_WEIGHT_NAMES, refs[:len(_WEIGHT_NAMES)]))
    (x1_ref, h2_ref, tw_ref, pk_ref, runs_ref, cnt_ref, nconv_ref, npool_ref,
     uhist, ushift, phist, vbuf, carry) = refs[len(_WEIGHT_NAMES):]

    b = pl.program_id(0)
    t = pl.program_id(1)
    nt = pl.num_programs(1)

    @pl.when((b == 0) & (t == 0))
    def _():
        carry[...] = jnp.zeros((SUBLANES, LANES), F32)

    @pl.when(t == 0)
    def _():
        for q in range(NSEQ):
            uhist[q, 0:CONV_HALO, :] = jnp.zeros((CONV_HALO, D_CONV), F32)
            phist[q, 0:POOL_HALO, :] = jnp.zeros((POOL_HALO, D_POOL), F32)

    gates = []
    for q in range(NSEQ):
        u, up, gc, gp = _in_proj(x_ref[q], mod_ref[0, q], mod_ref[1, q], w["nmix"], w["win"], w["bin"])
        uhist[q, CONV_HALO:CONV_HALO + TT, :] = u
        phist[q, POOL_HALO:POOL_HALO + TT, :] = up
        gates.append((gc, gp))

    def conv_and_pool(q):
        for r in range(1, SUBLANES):
            ushift[q, r - 1] = uhist[q, r:r + CONV_SHIFT_ROWS, :]
        for c in range(TT // CONV_ROWS):
            acc = jnp.zeros((CONV_ROWS, D_CONV), F32)
            for k in range(CONV_WIDTH):
                qq, r = divmod(CONV_HALO - CONV_BUF + k, SUBLANES)
                start = qq * SUBLANES + c * CONV_ROWS
                if r == 0:
                    tap = uhist[q, start:start + CONV_ROWS, :]
                else:
                    tap = ushift[q, r - 1, start:start + CONV_ROWS, :]
                acc = acc + w["cw"][k:k + 1, :] * tap
            s = _conv_act(acc, w["cb"], w["lng"], w["lnb"])
            vbuf[q, c * CONV_ROWS:(c + 1) * CONV_ROWS, :] = s.astype(BF16)

        pos = t * TT + jax.lax.broadcasted_iota(I32, (TT, 1), 0)
        pooled = []
        for g, win in enumerate(POOL_WINDOWS):
            lo, hi = g * POOL_GROUP_IN, (g + 1) * POOL_GROUP_IN
            cur = phist[q, POOL_HALO:POOL_HALO + TT, lo:hi]
            ssum = cur
            for i in range(1, win):
                ssum = ssum + phist[q, POOL_HALO - i:POOL_HALO - i + TT, lo:hi]
            cnt = jnp.minimum(pos + 1, win).astype(F32)
            pooled.append(ssum / cnt - cur)
        return pooled

    cur_carry = carry[0:1, :]
    for q in range(NSEQ):
        pooled = conv_and_pool(q)
        gc, gp = gates[q]
        x1, h2, tw, pk8, runs, cur_carry = _merge_and_route(
            x_ref[q], vbuf[q], pooled, gc, gp, mod_ref[2, q], mod_ref[3, q], mod_ref[4, q], cur_carry, w)
        x1_ref[q] = x1
        _store_token_tiles(h2_ref.at[q], h2)
        tw_ref[q] = tw
        pk_ref[q] = pk8
        runs_ref[q] = runs

        uhist[q, 0:CONV_HALO, :] = uhist[q, TT:TT + CONV_HALO, :]
        phist[q, 0:POOL_HALO, :] = phist[q, TT:TT + POOL_HALO, :]

    carry[...] = jnp.broadcast_to(cur_carry, (SUBLANES, LANES))
    cnt_ref[...] = jnp.broadcast_to(cur_carry, (SUBLANES, LANES))

    @pl.when(t == nt - 1)
    def _():
        for q in range(NSEQ):
            nconv_ref[q] = uhist[q, CONV_HALO + TT - CONV_BUF:CONV_HALO + TT, :]
            npool_ref[q] = phist[q, POOL_HALO + TT - POOL_BUF:POOL_HALO + TT, :]


def _prompt_mixer(l, x, mod, weights, batch, seq):
    nt = seq // TT
    n_tok = batch * seq
    half = n_tok // NSEQ
    row = lambda rows, width: pl.BlockSpec((NSEQ, rows, width), lambda b, t: (0, b * nt + t, 0))
    in_specs = [row(TT, D_MODEL),
                pl.BlockSpec((N_MOD, NSEQ, None, 1, D_MODEL), lambda b, t: (0, 0, b, 0, 0))]
    in_specs += _weight_specs(l, 2)
    out_shape = (
        jax.ShapeDtypeStruct((NSEQ, half, D_MODEL), F32),
        jax.ShapeDtypeStruct((NSEQ, half * ROW_TILE, LANES), F32),
        jax.ShapeDtypeStruct((NSEQ, half, LANES), F32),
        jax.ShapeDtypeStruct((NSEQ, SUBLANES, half), I32),
        jax.ShapeDtypeStruct((NSEQ, half // TT * SUBLANES, LANES), I32),
        jax.ShapeDtypeStruct((SUBLANES, LANES), F32),
        jax.ShapeDtypeStruct((NSEQ, batch // NSEQ, CONV_BUF, D_CONV), F32),
        jax.ShapeDtypeStruct((NSEQ, batch // NSEQ, POOL_BUF, D_POOL), F32),
    )
    out_specs = (
        row(TT, D_MODEL), row(TT * ROW_TILE, LANES), row(TT, LANES),
        pl.BlockSpec((NSEQ, SUBLANES, TT), lambda b, t: (0, 0, b * nt + t)),
        row(SUBLANES, LANES),
        pl.BlockSpec((SUBLANES, LANES), lambda b, t: (0, 0)),
        pl.BlockSpec((NSEQ, None, CONV_BUF, D_CONV), lambda b, t: (0, b, 0, 0)),
        pl.BlockSpec((NSEQ, None, POOL_BUF, D_POOL), lambda b, t: (0, b, 0, 0)),
    )
    x1, h2, tw, pk8, runs, cnt, nconv, npool = pl.pallas_call(
        _prompt_mixer_kernel,
        grid=(batch // NSEQ, nt),
        in_specs=in_specs,
        out_specs=out_specs,
        out_shape=out_shape,
        scratch_shapes=[
            pltpu.VMEM((NSEQ, CONV_HALO + TT, D_CONV), F32),
            pltpu.VMEM((NSEQ, SUBLANES - 1, CONV_SHIFT_ROWS, D_CONV), F32),
            pltpu.VMEM((NSEQ, POOL_HALO + TT, D_POOL), F32),
            pltpu.VMEM((NSEQ, TT, D_CONV), BF16),
            pltpu.VMEM((SUBLANES, LANES), F32),
        ],
        compiler_params=pltpu.CompilerParams(
            dimension_semantics=("arbitrary", "arbitrary"), vmem_limit_bytes=VMEM_LIMIT),
        name=f"prompt_mixer_l{l}",
    )(x.reshape(NSEQ, half, D_MODEL), mod.reshape(N_MOD, NSEQ, batch // NSEQ, 1, D_MODEL), *weights)
    return (x1.reshape(n_tok, D_MODEL), h2.reshape(n_tok * ROW_TILE, LANES), tw.reshape(n_tok, LANES),
            jnp.transpose(pk8, (1, 0, 2)).reshape(SUBLANES, n_tok),
            runs.reshape(n_tok // TT * SUBLANES, LANES), cnt,
            nconv.reshape(batch, CONV_BUF, D_CONV), npool.reshape(batch, POOL_BUF, D_POOL))


def _sample_mixer_kernel(x_ref, mod_ref, cs_ref, ps_ref, cnt0_ref, *refs, dec_seq, pos0):
    w = dict(zip(_WEIGHT_NAMES, refs[:len(_WEIGHT_NAMES)]))
    (x1_ref, h2_ref, tw_ref, pk_ref, runs_ref, cnt_ref, nconv_ref, npool_ref,
     ufull, pfull, vbuf, carry) = refs[len(_WEIGHT_NAMES):]

    def per_row(v):
        return jnp.concatenate([v] * dec_seq, axis=0)

    @pl.when(pl.program_id(0) == 0)
    def _():
        carry[...] = cnt0_ref[...]

    x = x_ref[...]
    sh1, sc1, gt1 = per_row(mod_ref[0]), per_row(mod_ref[1]), per_row(mod_ref[2])
    sh2, sc2 = per_row(mod_ref[3]), per_row(mod_ref[4])

    u, up, gc, gp = _in_proj(x, sh1, sc1, w["nmix"], w["win"], w["bin"])
    ufull[0:CONV_BUF] = cs_ref[...]
    pfull[0:POOL_BUF] = ps_ref[...]
    for j in range(dec_seq):
        ufull[CONV_BUF + j] = u[j * SB:(j + 1) * SB, :]
        pfull[POOL_BUF + j] = up[j * SB:(j + 1) * SB, :]
    nconv_ref[...] = ufull[dec_seq:dec_seq + CONV_BUF]
    npool_ref[...] = pfull[dec_seq:dec_seq + POOL_BUF]

    for j in range(dec_seq):
        acc = jnp.zeros((SB, D_CONV), F32)
        for k in range(CONV_WIDTH):
            acc = acc + w["cw"][k:k + 1, :] * ufull[j + k]
        s = _conv_act(acc, w["cb"], w["lng"], w["lnb"])
        vbuf[j * SB:(j + 1) * SB, :] = s.astype(BF16)

    pooled = []
    for g, win in enumerate(POOL_WINDOWS):
        lo, hi = g * POOL_GROUP_IN, (g + 1) * POOL_GROUP_IN
        parts = []
        for j in range(dec_seq):
            cur = pfull[POOL_BUF + j, :, lo:hi]
            ssum = cur
            for i in range(1, win):
                ssum = ssum + pfull[POOL_BUF + j - i, :, lo:hi]
            cnt = float(min(pos0 + j + 1, win))
            parts.append(ssum / cnt - cur)
        pooled.append(jnp.concatenate(parts, axis=0))

    x1, h2, tw, pk8, runs, new_carry = _merge_and_route(
        x, vbuf[...], pooled, gc, gp, gt1, sh2, sc2, carry[0:1, :], w)
    x1_ref[...] = x1
    _store_token_tiles(h2_ref, h2)
    tw_ref[...] = tw
    pk_ref[...] = pk8
    runs_ref[...] = runs
    carry[...] = jnp.broadcast_to(new_carry, (SUBLANES, LANES))
    cnt_ref[...] = jnp.broadcast_to(new_carry, (SUBLANES, LANES))


def _sample_mixer(l, x, mod, conv_state, pool_state, cnt0, weights, dec_batch, dec_seq, pos0):
    rows = dec_seq * SB
    n_s = dec_batch * dec_seq
    row = lambda width: pl.BlockSpec((rows, width), lambda i: (i, 0))
    conv_spec = pl.BlockSpec((None, CONV_BUF, SB, D_CONV), lambda i: (i, 0, 0, 0))
    pool_spec = pl.BlockSpec((None, POOL_BUF, SB, D_POOL), lambda i: (i, 0, 0, 0))
    cnt_spec = pl.BlockSpec((SUBLANES, LANES), lambda i: (0, 0))
    in_specs = [row(D_MODEL), pl.BlockSpec((N_MOD, SB, D_MODEL), lambda i: (0, i, 0)),
                conv_spec, pool_spec, cnt_spec]
    in_specs += _weight_specs(l, 1)
    out_shape = (
        jax.ShapeDtypeStruct((n_s, D_MODEL), F32),
        jax.ShapeDtypeStruct((n_s * ROW_TILE, LANES), F32),
        jax.ShapeDtypeStruct((n_s, LANES), F32),
        jax.ShapeDtypeStruct((SUBLANES, n_s), I32),
        jax.ShapeDtypeStruct((n_s // rows * SUBLANES, LANES), I32),
        jax.ShapeDtypeStruct((SUBLANES, LANES), F32),
        jax.ShapeDtypeStruct(conv_state.shape, F32),
        jax.ShapeDtypeStruct(pool_state.shape, F32),
    )
    out_specs = (row(D_MODEL), pl.BlockSpec((rows * ROW_TILE, LANES), lambda i: (i, 0)), row(LANES),
                 pl.BlockSpec((SUBLANES, rows), lambda i: (0, i)),
                 pl.BlockSpec((SUBLANES, LANES), lambda i: (i, 0)), cnt_spec, conv_spec, pool_spec)
    return pl.pallas_call(
        functools.partial(_sample_mixer_kernel, dec_seq=dec_seq, pos0=pos0),
        grid=(dec_batch // SB,),
        in_specs=in_specs,
        out_specs=out_specs,
        out_shape=out_shape,
        scratch_shapes=[
            pltpu.VMEM((CONV_BUF + dec_seq, SB, D_CONV), F32),
            pltpu.VMEM((POOL_BUF + dec_seq, SB, D_POOL), F32),
            pltpu.VMEM((rows, D_CONV), BF16),
            pltpu.VMEM((SUBLANES, LANES), F32),
        ],
        compiler_params=pltpu.CompilerParams(
            dimension_semantics=("arbitrary",), vmem_limit_bytes=VMEM_LIMIT),
        name=f"sample_mixer_l{l}",
    )(x, mod, conv_state, pool_state, cnt0, *weights)


RUN_ROWS = 3


def _table_copies(pk_hbm, runs_hbm, tile, pos, runs, slot, sem):
    copies = [
        pltpu.make_async_copy(pk_hbm.at[TOP_K + k, pl.ds(tile * TT, TT)],
                              pos.at[pl.ds((slot * TOP_K + k) * TT, TT)], sem)
        for k in range(TOP_K)]
    copies += [
        pltpu.make_async_copy(runs_hbm.at[tile * SUBLANES + r],
                              runs.at[pl.ds((slot * RUN_ROWS + r) * LANES, LANES)], sem)
        for r in range(RUN_ROWS)]
    return copies


def _run_copies(pstart_ref, runs, slot, sorted_hbm, run_ref, sem, to_hbm):
    def body(e, carry):
        count = runs[(slot * RUN_ROWS + 1) * LANES + e]

        @pl.when(count > 0)
        def _():
            glob = pl.multiple_of((pstart_ref[e] + runs[slot * RUN_ROWS * LANES + e]) * ROW_TILE, ROW_TILE)
            loc = pl.multiple_of(runs[(slot * RUN_ROWS + 2) * LANES + e] * ROW_TILE, ROW_TILE)
            hbm = sorted_hbm.at[pl.ds(glob, count * ROW_TILE), :]
            vmem = run_ref.at[pl.ds(loc, count * ROW_TILE), :]
            if to_hbm:
                pltpu.make_async_copy(vmem, hbm, sem).start()
            else:
                pltpu.make_async_copy(hbm, vmem, sem).start()
        return carry
    jax.lax.fori_loop(0, N_EXPERTS, body, 0, unroll=2)


def _dispatch_kernel(pstart_ref, pk_hbm, runs_hbm, h2p_ref, h2s_ref, xs_hbm, xrun0, xrun1, pos, runs,
                     isem, rsem, *, n_prompt_tiles):
    c = pl.program_id(0)
    n = pl.num_programs(0)
    xrun = (xrun0, xrun1)

    def tables(tile, slot):
        return _table_copies(pk_hbm, runs_hbm, tile, pos, runs, slot, isem.at[slot])

    def wait_runs(slot):
        pltpu.make_async_copy(xrun[slot], xs_hbm.at[pl.ds(0, TOP_K * TT * ROW_TILE), :], rsem.at[slot]).wait()

    @pl.when(c == 0)
    def _():
        for cp in tables(0, 0):
            cp.start()

    for s in range(2):
        @pl.when(c % 2 == s)
        def _():
            for cp in tables(c, s):
                cp.wait()

            @pl.when(c + 1 < n)
            def _():
                for cp in tables(c + 1, 1 - s):
                    cp.start()

            @pl.when(c >= 2)
            def _():
                wait_runs(s)

            def place_from(src_ref):
                def place(t, carry):
                    for k in range(TOP_K):
                        p = pos[(s * TOP_K + k) * TT + t]
                        _tile_at_row(xrun[s], p)[...] = _token_tile(src_ref, t)[...]
                    return carry
                jax.lax.fori_loop(0, TT, place, 0, unroll=8)

            @pl.when(c < n_prompt_tiles)
            def _():
                place_from(h2p_ref)

            @pl.when(c >= n_prompt_tiles)
            def _():
                place_from(h2s_ref)

            _run_copies(pstart_ref, runs, s, xs_hbm, xrun[s], rsem.at[s], to_hbm=True)

            @pl.when(c == n - 1)
            def _():
                @pl.when(c >= 1)
                def _():
                    wait_runs(1 - s)
                wait_runs(s)


def _dispatch(pstart, pk8, runs, h2p, h2s, n_slots):
    np_tiles = h2p.shape[0] // (TT * ROW_TILE)
    ns_tiles = h2s.shape[0] // (TT * ROW_TILE)
    tiles = pltpu.VMEM((TOP_K * TT * ROW_TILE, LANES), F32)
    grid_spec = pltpu.PrefetchScalarGridSpec(
        num_scalar_prefetch=1,
        grid=(np_tiles + ns_tiles,),
        in_specs=[
            pl.BlockSpec(memory_space=pl.ANY),
            pl.BlockSpec(memory_space=pl.ANY),
            pl.BlockSpec((TT * ROW_TILE, LANES), lambda c, ps: (jnp.minimum(c, np_tiles - 1), 0)),
            pl.BlockSpec((TT * ROW_TILE, LANES), lambda c, ps: (jnp.maximum(c - np_tiles, 0), 0)),
        ],
        out_specs=pl.BlockSpec(memory_space=pl.ANY),
        scratch_shapes=[
            tiles, tiles,
            pltpu.SMEM((2 * TOP_K * TT,), I32),
            pltpu.SMEM((2 * RUN_ROWS * LANES,), I32),
            pltpu.SemaphoreType.DMA((2,)),
            pltpu.SemaphoreType.DMA((2,)),
        ],
    )
    return pl.pallas_call(
        functools.partial(_dispatch_kernel, n_prompt_tiles=np_tiles),
        grid_spec=grid_spec,
        out_shape=jax.ShapeDtypeStruct((n_slots * ROW_TILE, LANES), F32),
        compiler_params=pltpu.CompilerParams(
            dimension_semantics=("arbitrary",), vmem_limit_bytes=VMEM_LIMIT),
        name="dispatch",
    )(pstart, pk8, runs, h2p, h2s)


def _experts_kernel(be_ref, nv_ref, nu_ref, nx_ref, x_ref, wup_hbm, bup_ref, wdn_hbm, bdn_ref, o_ref,
                    wup_f32, wdn_f32, wup_bf, wdn_bf, wsem, *, layer):
    b = pl.program_id(0)

    def weight_copies(e):
        return (pltpu.make_async_copy(wup_hbm.at[layer, e], wup_f32, wsem.at[0]),
                pltpu.make_async_copy(wdn_hbm.at[layer, e], wdn_f32, wsem.at[1]))

    @pl.when(b < nu_ref[0])
    def _():
        e = be_ref[b]
        prev = be_ref[jnp.maximum(b - 1, 0)]

        @pl.when((b == 0) | (e != prev))
        def _():
            @pl.when(b == 0)
            def _():
                for cp in weight_copies(e):
                    cp.start()
            for cp in weight_copies(e):
                cp.wait()
            for r in range(0, D_MODEL, LANES):
                wup_bf[r:r + LANES, :] = wup_f32[r:r + LANES, :].astype(BF16)
                wdn_bf[r:r + LANES, :] = wdn_f32[r:r + LANES, :].astype(BF16)
            nxt = nx_ref[e]

            @pl.when(nxt >= 0)
            def _():
                for cp in weight_copies(nxt):
                    cp.start()

        n_valid = nv_ref[b]

        def ffn(rows):
            rid = jax.lax.broadcasted_iota(I32, (rows, D_MODEL), 0)
            x = jnp.where(rid < n_valid, _load_token_tiles(x_ref, 0, rows), 0.0)
            a = _dot(x.astype(BF16), wup_bf[...]) + bup_ref[...]
            a_glu = jnp.minimum(a[:, :D_EXPERT], SWIGLU_LIMIT)
            a_lin = jnp.clip(a[:, D_EXPERT:], -SWIGLU_LIMIT, SWIGLU_LIMIT)
            o = a_glu * jax.nn.sigmoid(SWIGLU_ALPHA * a_glu) * (a_lin + 1.0)
            _store_token_tiles(o_ref, _dot(o.astype(BF16), wdn_bf[...]) + bdn_ref[...])
            if rows < BM:
                o_ref[rows * ROW_TILE:, :] = jnp.zeros(((BM - rows) * ROW_TILE, LANES), F32)

        @pl.when(n_valid > BM // 2)
        def _():
            ffn(BM)

        @pl.when(n_valid <= BM // 2)
        def _():
            ffn(BM // 2)

    @pl.when(b >= nu_ref[0])
    def _():
        o_ref[...] = jnp.zeros((BM * ROW_TILE, LANES), F32)


def _experts(l, x_sorted, block_expert, n_valid, n_used, next_expert, w_up, b_up, w_down, b_down):
    n_slots = x_sorted.shape[0] // ROW_TILE
    nb = n_slots // BM

    def used_map(b, be, nv, nu, nx):
        return (jnp.minimum(b, nu[0] - 1), 0)

    bmap = lambda b, be, nv, nu, nx: (l, be[b], 0, 0)
    grid_spec = pltpu.PrefetchScalarGridSpec(
        num_scalar_prefetch=4,
        grid=(nb,),
        in_specs=[
            pl.BlockSpec((BM * ROW_TILE, LANES), used_map),
            pl.BlockSpec(memory_space=pl.ANY),
            pl.BlockSpec((None, None, 1, 2 * D_EXPERT), bmap),
            pl.BlockSpec(memory_space=pl.ANY),
            pl.BlockSpec((None, None, 1, D_MODEL), bmap),
        ],
        out_specs=pl.BlockSpec((BM * ROW_TILE, LANES), lambda b, be, nv, nu, nx: (b, 0)),
        scratch_shapes=[
            pltpu.VMEM((D_MODEL, 2 * D_EXPERT), F32),
            pltpu.VMEM((D_EXPERT, D_MODEL), F32),
            pltpu.VMEM((D_MODEL, 2 * D_EXPERT), BF16),
            pltpu.VMEM((D_EXPERT, D_MODEL), BF16),
            pltpu.SemaphoreType.DMA((2,)),
        ],
    )
    return pl.pallas_call(
        functools.partial(_experts_kernel, layer=l),
        grid_spec=grid_spec,
        out_shape=jax.ShapeDtypeStruct((n_slots * ROW_TILE, LANES), F32),
        compiler_params=pltpu.CompilerParams(
            dimension_semantics=("arbitrary",), vmem_limit_bytes=VMEM_LIMIT),
        name=f"experts_l{l}",
    )(block_expert, n_valid, n_used, next_expert, x_sorted, w_up,
      b_up.reshape(DEPTH, N_EXPERTS, 1, 2 * D_EXPERT), w_down,
      b_down.reshape(DEPTH, N_EXPERTS, 1, D_MODEL))


def _tables_kernel(cnt_ref, pstart_ref, be_ref, nv_ref, nu_ref, nx_ref, *, nb):
    def clear(b, c):
        be_ref[b] = N_EXPERTS - 1
        nv_ref[b] = 0
        return c
    jax.lax.fori_loop(0, nb, clear, 0)

    def per_expert(e, carry):
        blk, last = carry
        cnt = cnt_ref[e]
        n = (cnt + (BM - 1)) // BM
        pstart_ref[e] = blk * BM
        nx_ref[e] = -1

        def per_block(j, c):
            be_ref[blk + j] = e
            nv_ref[blk + j] = jnp.minimum(cnt - j * BM, BM)
            return c
        jax.lax.fori_loop(0, n, per_block, 0)

        @pl.when((n > 0) & (last >= 0))
        def _():
            nx_ref[last] = e
        return blk + n, jnp.where(n > 0, e, last)

    blk, _ = jax.lax.fori_loop(0, N_EXPERTS, per_expert, (jnp.int32(0), jnp.int32(-1)))
    nu_ref[0] = blk


def _block_tables(counts, nb):
    smem = pl.BlockSpec(memory_space=pltpu.SMEM)
    return pl.pallas_call(
        functools.partial(_tables_kernel, nb=nb),
        in_specs=[smem],
        out_specs=(smem, smem, smem, smem, smem),
        out_shape=(
            jax.ShapeDtypeStruct((N_EXPERTS,), I32),
            jax.ShapeDtypeStruct((nb,), I32),
            jax.ShapeDtypeStruct((nb,), I32),
            jax.ShapeDtypeStruct((1,), I32),
            jax.ShapeDtypeStruct((N_EXPERTS,), I32),
        ),
        name="block_tables",
    )(counts)


def _combine_kernel(pstart_ref, pk_hbm, runs_hbm, y_hbm, x1_ref, tw_ref, gt_ref, g_ref, o_ref,
                    yrun0, yrun1, ybuf, pos, runs, isem, rsem, *, gate_copies, final):
    i = pl.program_id(0)
    n = pl.num_programs(0)
    yrun = (yrun0, yrun1)

    def tables(tile, slot):
        return _table_copies(pk_hbm, runs_hbm, tile, pos, runs, slot, isem.at[slot])

    @pl.when(i == 0)
    def _():
        for cp in tables(0, 0):
            cp.start()
        for cp in tables(0, 0):
            cp.wait()
        _run_copies(pstart_ref, runs, 0, y_hbm, yrun[0], rsem.at[0], to_hbm=False)

        @pl.when(n > 1)
        def _():
            for cp in tables(1, 1):
                cp.start()

    for s in range(2):
        @pl.when(i % 2 == s)
        def _():
            @pl.when(i + 1 < n)
            def _():
                for cp in tables(i + 1, 1 - s):
                    cp.wait()
                _run_copies(pstart_ref, runs, 1 - s, y_hbm, yrun[1 - s], rsem.at[1 - s], to_hbm=False)

            pltpu.make_async_copy(y_hbm.at[pl.ds(0, TOP_K * TT * ROW_TILE), :], yrun[s], rsem.at[s]).wait()

            def place(t, carry):
                for k in range(TOP_K):
                    p = pos[(s * TOP_K + k) * TT + t]
                    _token_tile(ybuf, k * TT + t)[...] = _tile_at_row(yrun[s], p)[...]
                return carry
            jax.lax.fori_loop(0, TT, place, 0, unroll=8)

            @pl.when(i + 2 < n)
            def _():
                for cp in tables(i + 2, s):
                    cp.start()

            tw = tw_ref[...]
            moe = tw[:, 0:1] * _load_token_tiles(ybuf, 0, TT)
            for k in range(1, TOP_K):
                moe = moe + tw[:, k:k + 1] * _load_token_tiles(ybuf, k * TT, TT)
            gt = gt_ref[...]
            if gate_copies > 1:
                gt = jnp.concatenate([gt] * gate_copies, axis=0)
            x = x1_ref[...] + gt * moe
            o_ref[...] = _rms(x, g_ref[...]) if final else x


def _combine(pstart, pk8, runs, y_sorted, x1, tw, gate, norm_final, *, gate_spec, gate_copies, final, name):
    n = x1.shape[0]
    row = lambda width: pl.BlockSpec((TT, width), lambda i, ps: (i, 0))
    tiles = pltpu.VMEM((TOP_K * TT * ROW_TILE, LANES), F32)
    grid_spec = pltpu.PrefetchScalarGridSpec(
        num_scalar_prefetch=1,
        grid=(n // TT,),
        in_specs=[
            pl.BlockSpec(memory_space=pl.ANY),
            pl.BlockSpec(memory_space=pl.ANY),
            pl.BlockSpec(memory_space=pl.ANY),
            row(D_MODEL), row(LANES), gate_spec,
            pl.BlockSpec((1, D_MODEL), lambda i, ps: (0, 0)),
        ],
        out_specs=row(D_MODEL),
        scratch_shapes=[
            tiles, tiles, tiles,
            pltpu.SMEM((2 * TOP_K * TT,), I32),
            pltpu.SMEM((2 * RUN_ROWS * LANES,), I32),
            pltpu.SemaphoreType.DMA((2,)),
            pltpu.SemaphoreType.DMA((2,)),
        ],
    )
    return pl.pallas_call(
        functools.partial(_combine_kernel, gate_copies=gate_copies, final=final),
        grid_spec=grid_spec,
        out_shape=jax.ShapeDtypeStruct((n, D_MODEL), F32),
        compiler_params=pltpu.CompilerParams(
            dimension_semantics=("arbitrary",), vmem_limit_bytes=VMEM_LIMIT),
        name=name,
    )(pstart, pk8, runs, y_sorted, x1, tw, gate, norm_final.reshape(1, D_MODEL))


def kernel(x_prompt, x_sample, state_conv, state_pool, c_prompt, c_sample, w_ada, b_ada, norm_mix, w_in, b_in, conv_w, conv_b, conv_ln_g, conv_ln_b, w_conv_out, w_pool, pool_scale, w_out, norm_ffn, w_router, b_router, w_up, b_up, w_down, b_down, norm_final):
    batch, seq, _ = x_prompt.shape
    dec_batch, dec_seq, _ = x_sample.shape
    n_p = batch * seq
    n_s = dec_batch * dec_seq
    n_blocks = dec_batch // SB
    assert dec_seq * SB == TT and seq % TT == 0 and dec_batch % SB == 0
    n_assign = (n_p + n_s) * TOP_K
    nb = n_assign // BM + N_EXPERTS

    vec = lambda a: a.reshape(DEPTH, 1, a.shape[-1])
    weights = (
        vec(norm_mix), w_in.astype(BF16), vec(b_in), conv_w, vec(conv_b), vec(conv_ln_g),
        vec(conv_ln_b), w_conv_out.astype(BF16), w_pool.astype(BF16), vec(pool_scale),
        w_out.astype(BF16), vec(norm_ffn),
        jnp.pad(w_router, ((0, 0), (0, 0), (0, LANES - N_EXPERTS))).astype(BF16),
        vec(jnp.pad(b_router, ((0, 0), (0, LANES - N_EXPERTS)))),
    )

    mod = _adaln(jnp.concatenate([c_prompt, c_sample], axis=0), w_ada, b_ada)
    mod_p = mod[:, :, :batch]
    mod_s = mod[:, :, batch:]

    def to_sample_order(a):
        lead = a.shape[:-3]
        a = a.reshape(lead + (n_blocks, SB) + a.shape[-2:])
        return jnp.swapaxes(a, -3, -2)

    def from_sample_order(a):
        a = jnp.swapaxes(a, -3, -2)
        return a.reshape(a.shape[:-4] + (dec_batch,) + a.shape[-2:])

    xp = x_prompt.reshape(n_p, D_MODEL)
    xs = to_sample_order(x_sample).reshape(n_s, D_MODEL)
    conv_state = to_sample_order(state_conv)
    pool_state = to_sample_order(state_pool)

    gate_p = pl.BlockSpec((None, 1, D_MODEL), lambda i, ps: (i // (seq // TT), 0, 0))
    gate_s = pl.BlockSpec((SB, D_MODEL), lambda i, ps: (i, 0))

    new_conv_p, new_pool_p, new_conv_s, new_pool_s = [], [], [], []
    for l in range(DEPTH):
        x1p, h2p, twp, pkp, runsp, cntp, ncp, npp = _prompt_mixer(l, xp, mod_p[l], weights, batch, seq)
        x1s, h2s, tws, pks, runss, cnt, ncs, nps = _sample_mixer(
            l, xs, mod_s[l], conv_state[l], pool_state[l], cntp, weights, dec_batch, dec_seq, PAST_LEN)
        new_conv_p.append(ncp)
        new_pool_p.append(npp)
        new_conv_s.append(ncs)
        new_pool_s.append(nps)

        counts = cnt[0, :N_EXPERTS].astype(I32)
        pstart, block_expert, n_valid, n_used, next_expert = _block_tables(counts, nb)
        x_sorted = _dispatch(pstart, jnp.concatenate([pkp, pks], axis=1),
                             jnp.concatenate([runsp, runss], axis=0), h2p, h2s, nb * BM)
        y_sorted = _experts(l, x_sorted, block_expert, n_valid, n_used, next_expert, w_up, b_up, w_down,
                            b_down)

        final = l == DEPTH - 1
        xp = _combine(pstart, pkp, runsp, y_sorted, x1p, twp, mod_p[l][5][:, None, :], norm_final,
                      gate_spec=gate_p, gate_copies=1, final=final, name=f"combine_prompt_l{l}")
        xs = _combine(pstart, pks, runss, y_sorted, x1s, tws, mod_s[l][5], norm_final,
                      gate_spec=gate_s, gate_copies=dec_seq, final=final, name=f"combine_sample_l{l}")

    y_prompt = xp.reshape(batch, seq, D_MODEL)
    y_sample = from_sample_order(xs.reshape(n_blocks, dec_seq, SB, D_MODEL))
    return (y_prompt, y_sample, jnp.stack(new_conv_p), jnp.stack(new_pool_p),
            from_sample_order(jnp.stack(new_conv_s)), from_sample_order(jnp.stack(new_pool_s)))
```

```python
import functools

import jax
import jax.numpy as jnp
from jax.experimental import pallas as pl
from jax.experimental.pallas import tpu as pltpu

F32 = jnp.float32
BF16 = jnp.bfloat16
I32 = jnp.int32

D_MODEL = 1024
DEPTH = 2
D_CONV = 512
CONV_WIDTH = 31
CONV_BUF = CONV_WIDTH - 1
D_POOL = 512
POOL_WINDOWS = (2, 4, 8, 16)
POOL_GROUP_IN = D_POOL // len(POOL_WINDOWS)
POOL_GROUP_OUT = D_MODEL // len(POOL_WINDOWS)
POOL_BUF = max(POOL_WINDOWS) - 1
IN_COLS = 2 * D_CONV + D_POOL + 2 * D_MODEL
N_EXPERTS = 32
TOP_K = 4
D_EXPERT = D_MODEL
SWIGLU_LIMIT = 7.0
SWIGLU_ALPHA = 1.702
N_MOD = 6
EPS = 1e-6
PAST_LEN = 16384

LANES = 128
SUBLANES = 8
VMEM_LIMIT = 52 * 1024 * 1024

TT = 256
NSEQ = 2
CONV_HALO = 32
POOL_HALO = 16
CONV_ROWS = 64
SB = 64
BM = 512
BM_PARTS = 4
ROW_TILE = D_MODEL // LANES
EXPERT_BITS = 5
CONV_SHIFT_ROWS = TT + CONV_HALO - SUBLANES
assert N_EXPERTS == 1 << EXPERT_BITS and ROW_TILE == SUBLANES


def _rms(x, g):
    return x * jax.lax.rsqrt(jnp.mean(x * x, axis=-1, keepdims=True) + EPS) * g


def _dot(a, b):
    return jnp.dot(a, b, preferred_element_type=F32)


def _store_token_tiles(ref, v):
    rows = v.shape[0]
    for s in range(ROW_TILE):
        ref[pl.ds(s, rows, stride=ROW_TILE), :] = v[:, s * LANES:(s + 1) * LANES]


def _load_token_tiles(ref, first_token, rows):
    return jnp.concatenate(
        [ref[pl.ds(first_token * ROW_TILE + s, rows, stride=ROW_TILE), :] for s in range(ROW_TILE)], axis=-1)


def _tile_at_row(ref, row):
    return ref.at[pl.ds(pl.multiple_of(row, ROW_TILE), ROW_TILE), :]


def _token_tile(ref, t):
    return _tile_at_row(ref, t * ROW_TILE)


def _adaln_kernel(c_ref, w_ref, b_ref, o_ref):
    c = c_ref[...]
    a = (c * jax.nn.sigmoid(c)).astype(BF16)
    o_ref[...] = _dot(a, w_ref[...].astype(BF16)) + b_ref[...]


def _adaln(c_all, w_ada, b_ada):
    n = c_all.shape[0]
    cols = N_MOD * D_MODEL
    return pl.pallas_call(
        _adaln_kernel,
        grid=(DEPTH, N_MOD),
        in_specs=[
            pl.BlockSpec((n, D_MODEL), lambda l, j: (0, 0)),
            pl.BlockSpec((None, D_MODEL, D_MODEL), lambda l, j: (l, 0, j)),
            pl.BlockSpec((None, 1, D_MODEL), lambda l, j: (l, 0, j)),
        ],
        out_specs=pl.BlockSpec((None, None, n, D_MODEL), lambda l, j: (l, j, 0, 0)),
        out_shape=jax.ShapeDtypeStruct((DEPTH, N_MOD, n, D_MODEL), F32),
        compiler_params=pltpu.CompilerParams(
            dimension_semantics=("arbitrary", "arbitrary"), vmem_limit_bytes=VMEM_LIMIT),
        name="adaln",
    )(c_all, w_ada, b_ada.reshape(DEPTH, 1, cols))


def _in_proj(x, sh1, sc1, nmix_ref, win_ref, bin_ref):
    h = (_rms(x, nmix_ref[...]) * (1.0 + sc1) + sh1).astype(BF16)
    c0, c1, c2 = 2 * D_CONV, 2 * D_CONV + D_POOL, IN_COLS
    zu = _dot(h, win_ref[:, 0:c0]) + bin_ref[:, 0:c0]
    u = zu[:, :D_CONV] * jax.nn.sigmoid(zu[:, D_CONV:])
    up = _dot(h, win_ref[:, c0:c1]) + bin_ref[:, c0:c1]
    zg = _dot(h, win_ref[:, c1:c2]) + bin_ref[:, c1:c2]
    return u, up, zg[:, :D_MODEL], zg[:, D_MODEL:]


def _conv_act(acc, cb_ref, lng_ref, lnb_ref):
    v = acc + cb_ref[...]
    mu = jnp.mean(v, axis=-1, keepdims=True)
    d = v - mu
    var = jnp.mean(d * d, axis=-1, keepdims=True)
    vn = d * jax.lax.rsqrt(var + EPS) * lng_ref[...] + lnb_ref[...]
    return vn * jax.nn.sigmoid(vn)


def _merge_and_route(x, v_bf, pooled, gc, gp, gt1, sh2, sc2, carry, w):
    rows = x.shape[0]
    y_conv = _dot(v_bf, w["wco"][...])
    y_pool = jnp.concatenate(
        [_dot(pooled[g].astype(BF16), w["wpool"][g]) for g in range(len(POOL_WINDOWS))], axis=-1)
    y_pool = y_pool * w["pscale"][...]
    m = jax.nn.sigmoid(gc) * y_conv + jax.nn.sigmoid(gp) * y_pool
    x1 = x + gt1 * _dot(m.astype(BF16), w["wout"][...])
    h2 = _rms(x1, w["nffn"][...]) * (1.0 + sc2) + sh2
    logits = _dot(h2.astype(BF16), w["wr"][...]) + w["br"][...]
    lane = jax.lax.broadcasted_iota(I32, (rows, LANES), 1)
    lane_f = lane.astype(F32)
    neg = jnp.full((rows, LANES), -jnp.inf, F32)
    l = jnp.where(lane < N_EXPERTS, logits, neg)
    vals, idxs, sels = [], [], []
    for _ in range(TOP_K):
        mx = jnp.max(l, axis=-1, keepdims=True)
        ix = jnp.min(jnp.where(l == mx, lane_f, float(LANES)), axis=-1, keepdims=True)
        sel = lane_f == ix
        l = jnp.where(sel, neg, l)
        vals.append(mx)
        idxs.append(ix)
        sels.append(sel)
    es = [jnp.exp(v - vals[0]) for v in vals]
    den = es[0] + es[1] + es[2] + es[3]

    onehot = jnp.where(sels[0] | sels[1] | sels[2] | sels[3], 1.0, 0.0)
    r_i = jax.lax.broadcasted_iota(I32, (rows, rows), 0)
    c_i = jax.lax.broadcasted_iota(I32, (rows, rows), 1)
    before = jnp.where(c_i < r_i, 1.0, 0.0).astype(BF16)
    earlier = _dot(before, onehot.astype(BF16))
    count = jnp.sum(onehot, axis=0, keepdims=True)
    new_carry = carry + count
    e_i = jax.lax.broadcasted_iota(I32, (LANES, LANES), 0)
    e_j = jax.lax.broadcasted_iota(I32, (LANES, LANES), 1)
    lower_expert = jnp.where(e_i < e_j, 1.0, 0.0).astype(BF16)
    first = _dot(jnp.broadcast_to(count, (SUBLANES, LANES)).astype(BF16), lower_expert)[0:1, :]

    tw = jnp.zeros((rows, LANES), F32)
    pk = jnp.zeros((rows, LANES), F32)
    for k in range(TOP_K):
        in_tile = jnp.sum(jnp.where(sels[k], earlier, 0.0), axis=-1, keepdims=True)
        prior = jnp.sum(jnp.where(sels[k], carry, 0.0), axis=-1, keepdims=True)
        start = jnp.sum(jnp.where(sels[k], first, 0.0), axis=-1, keepdims=True)
        tw = jnp.where(lane == k, es[k] / den, tw)
        pk = jnp.where(lane == k, (prior + in_tile) * float(N_EXPERTS) + idxs[k], pk)
        pk = jnp.where(lane == TOP_K + k, (start + in_tile) * float(ROW_TILE), pk)
    pk8 = jnp.transpose(pk)[0:SUBLANES, :].astype(I32)
    runs = jnp.concatenate(
        [carry, count, first, jnp.zeros((SUBLANES - 3, LANES), F32)], axis=0).astype(I32)
    return x1, h2, tw, pk8, runs, new_carry


_WEIGHT_NAMES = ("nmix", "win", "bin", "cw", "cb", "lng", "lnb", "wco", "wpool", "pscale", "wout",
                 "nffn", "wr", "br")


def _weight_specs(l, n_grid):
    def spec(*shape):
        zeros = (0,) * len(shape)
        if n_grid == 2:
            return pl.BlockSpec((None,) + shape, lambda b, t: (l,) + zeros)
        return pl.BlockSpec((None,) + shape, lambda i: (l,) + zeros)
    return [
        spec(1, D_MODEL),
        spec(D_MODEL, IN_COLS),
        spec(1, IN_COLS),
        spec(CONV_WIDTH, D_CONV),
        spec(1, D_CONV),
        spec(1, D_CONV),
        spec(1, D_CONV),
        spec(D_CONV, D_MODEL),
        spec(len(POOL_WINDOWS), POOL_GROUP_IN, POOL_GROUP_OUT),
        spec(1, D_MODEL),
        spec(D_MODEL, D_MODEL),
        spec(1, D_MODEL),
        spec(D_MODEL, LANES),
        spec(1, LANES),
    ]


def _prompt_mixer_kernel(x_ref, mod_ref, *refs):
    w = dict(zip(_WEIGHT_NAMES, refs[:len(_WEIGHT_NAMES)]))
    (x1_ref, h2_ref, tw_ref, pk_ref, runs_ref, cnt_ref, nconv_ref, npool_ref,
     uhist, ushift, phist, vbuf, carry) = refs[len(_WEIGHT_NAMES):]

    b = pl.program_id(0)
    t = pl.program_id(1)
    nt = pl.num_programs(1)

    @pl.when((b == 0) & (t == 0))
    def _():
        carry[...] = jnp.zeros((SUBLANES, LANES), F32)

    @pl.when(t == 0)
    def _():
        for q in range(NSEQ):
            uhist[q, 0:CONV_HALO, :] = jnp.zeros((CONV_HALO, D_CONV), F32)
            phist[q, 0:POOL_HALO, :] = jnp.zeros((POOL_HALO, D_POOL), F32)

    gates = []
    for q in range(NSEQ):
        u, up, gc, gp = _in_proj(x_ref[q], mod_ref[0, q], mod_ref[1, q], w["nmix"], w["win"], w["bin"])
        uhist[q, CONV_HALO:CONV_HALO + TT, :] = u
        phist[q, POOL_HALO:POOL_HALO + TT, :] = up
        gates.append((gc, gp))

    def conv_and_pool(q):
        for r in range(1, SUBLANES):
            ushift[q, r - 1] = uhist[q, r:r + CONV_SHIFT_ROWS, :]
        for c in range(TT // CONV_ROWS):
            acc = jnp.zeros((CONV_ROWS, D_CONV), F32)
            for k in range(CONV_WIDTH):
                qq, r = divmod(CONV_HALO - CONV_BUF + k, SUBLANES)
                start = qq * SUBLANES + c * CONV_ROWS
                if r == 0:
                    tap = uhist[q, start:start + CONV_ROWS, :]
                else:
                    tap = ushift[q, r - 1, start:start + CONV_ROWS, :]
                acc = acc + w["cw"][k:k + 1, :] * tap
            s = _conv_act(acc, w["cb"], w["lng"], w["lnb"])
            vbuf[q, c * CONV_ROWS:(c + 1) * CONV_ROWS, :] = s.astype(BF16)

        pos = t * TT + jax.lax.broadcasted_iota(I32, (TT, 1), 0)
        pooled = []
        for g, win in enumerate(POOL_WINDOWS):
            lo, hi = g * POOL_GROUP_IN, (g + 1) * POOL_GROUP_IN
            cur = phist[q, POOL_HALO:POOL_HALO + TT, lo:hi]
            ssum = cur
            for i in range(1, win):
                ssum = ssum + phist[q, POOL_HALO - i:POOL_HALO - i + TT, lo:hi]
            cnt = jnp.minimum(pos + 1, win).astype(F32)
            pooled.append(ssum / cnt - cur)
        return pooled

    cur_carry = carry[0:1, :]
    for q in range(NSEQ):
        pooled = conv_and_pool(q)
        gc, gp = gates[q]
        x1, h2, tw, pk8, runs, cur_carry = _merge_and_route(
            x_ref[q], vbuf[q], pooled, gc, gp, mod_ref[2, q], mod_ref[3, q], mod_ref[4, q], cur_carry, w)
        x1_ref[q] = x1
        _store_token_tiles(h2_ref.at[q], h2)
        tw_ref[q] = tw
        pk_ref[q] = pk8
        runs_ref[q] = runs

        uhist[q, 0:CONV_HALO, :] = uhist[q, TT:TT + CONV_HALO, :]
        phist[q, 0:POOL_HALO, :] = phist[q, TT:TT + POOL_HALO, :]

    carry[...] = jnp.broadcast_to(cur_carry, (SUBLANES, LANES))
    cnt_ref[...] = jnp.broadcast_to(cur_carry, (SUBLANES, LANES))

    @pl.when(t == nt - 1)
    def _():
        for q in range(NSEQ):
            nconv_ref[q] = uhist[q, CONV_HALO + TT - CONV_BUF:CONV_HALO + TT, :]
            npool_ref[q] = phist[q, POOL_HALO + TT - POOL_BUF:POOL_HALO + TT, :]


def _prompt_mixer(l, x, mod, weights, batch, seq):
    nt = seq // TT
    n_tok = batch * seq
    half = n_tok // NSEQ
    row = lambda rows, width: pl.BlockSpec((NSEQ, rows, width), lambda b, t: (0, b * nt + t, 0))
    in_specs = [row(TT, D_MODEL),
                pl.BlockSpec((N_MOD, NSEQ, None, 1, D_MODEL), lambda b, t: (0, 0, b, 0, 0))]
    in_specs += _weight_specs(l, 2)
    out_shape = (
        jax.ShapeDtypeStruct((NSEQ, half, D_MODEL), F32),
        jax.ShapeDtypeStruct((NSEQ, half * ROW_TILE, LANES), F32),
        jax.ShapeDtypeStruct((NSEQ, half, LANES), F32),
        jax.ShapeDtypeStruct((NSEQ, SUBLANES, half), I32),
        jax.ShapeDtypeStruct((NSEQ, half // TT * SUBLANES, LANES), I32),
        jax.ShapeDtypeStruct((SUBLANES, LANES), F32),
        jax.ShapeDtypeStruct((NSEQ, batch // NSEQ, CONV_BUF, D_CONV), F32),
        jax.ShapeDtypeStruct((NSEQ, batch // NSEQ, POOL_BUF, D_POOL), F32),
    )
    out_specs = (
        row(TT, D_MODEL), row(TT * ROW_TILE, LANES), row(TT, LANES),
        pl.BlockSpec((NSEQ, SUBLANES, TT), lambda b, t: (0, 0, b * nt + t)),
        row(SUBLANES, LANES),
        pl.BlockSpec((SUBLANES, LANES), lambda b, t: (0, 0)),
        pl.BlockSpec((NSEQ, None, CONV_BUF, D_CONV), lambda b, t: (0, b, 0, 0)),
        pl.BlockSpec((NSEQ, None, POOL_BUF, D_POOL), lambda b, t: (0, b, 0, 0)),
    )
    x1, h2, tw, pk8, runs, cnt, nconv, npool = pl.pallas_call(
        _prompt_mixer_kernel,
        grid=(batch // NSEQ, nt),
        in_specs=in_specs,
        out_specs=out_specs,
        out_shape=out_shape,
        scratch_shapes=[
            pltpu.VMEM((NSEQ, CONV_HALO + TT, D_CONV), F32),
            pltpu.VMEM((NSEQ, SUBLANES - 1, CONV_SHIFT_ROWS, D_CONV), F32),
            pltpu.VMEM((NSEQ, POOL_HALO + TT, D_POOL), F32),
            pltpu.VMEM((NSEQ, TT, D_CONV), BF16),
            pltpu.VMEM((SUBLANES, LANES), F32),
        ],
        compiler_params=pltpu.CompilerParams(
            dimension_semantics=("arbitrary", "arbitrary"), vmem_limit_bytes=VMEM_LIMIT),
        name=f"prompt_mixer_l{l}",
    )(x.reshape(NSEQ, half, D_MODEL), mod.reshape(N_MOD, NSEQ, batch // NSEQ, 1, D_MODEL), *weights)
    return (x1.reshape(n_tok, D_MODEL), h2.reshape(n_tok * ROW_TILE, LANES), tw.reshape(n_tok, LANES),
            jnp.transpose(pk8, (1, 0, 2)).reshape(SUBLANES, n_tok),
            runs.reshape(n_tok // TT * SUBLANES, LANES), cnt,
            nconv.reshape(batch, CONV_BUF, D_CONV), npool.reshape(batch, POOL_BUF, D_POOL))


def _sample_mixer_kernel(x_ref, mod_ref, cs_ref, ps_ref, cnt0_ref, *refs, dec_seq, pos0):
    w = dict(zip(_WEIGHT_NAMES, refs[:len(_WEIGHT_NAMES)]))
    (x1_ref, h2_ref, tw_ref, pk_ref, runs_ref, cnt_ref, nconv_ref, npool_ref,
     ufull, pfull, vbuf, carry) = refs[len(_WEIGHT_NAMES):]

    def per_row(v):
        return jnp.concatenate([v] * dec_seq, axis=0)

    @pl.when(pl.program_id(0) == 0)
    def _():
        carry[...] = cnt0_ref[...]

    x = x_ref[...]
    sh1, sc1, gt1 = per_row(mod_ref[0]), per_row(mod_ref[1]), per_row(mod_ref[2])
    sh2, sc2 = per_row(mod_ref[3]), per_row(mod_ref[4])

    u, up, gc, gp = _in_proj(x, sh1, sc1, w["nmix"], w["win"], w["bin"])
    ufull[0:CONV_BUF] = cs_ref[...]
    pfull[0:POOL_BUF] = ps_ref[...]
    for j in range(dec_seq):
        ufull[CONV_BUF + j] = u[j * SB:(j + 1) * SB, :]
        pfull[POOL_BUF + j] = up[j * SB:(j + 1) * SB, :]
    nconv_ref[...] = ufull[dec_seq:dec_seq + CONV_BUF]
    npool_ref[...] = pfull[dec_seq:dec_seq + POOL_BUF]

    for j in range(dec_seq):
        acc = jnp.zeros((SB, D_CONV), F32)
        for k in range(CONV_WIDTH):
            acc = acc + w["cw"][k:k + 1, :] * ufull[j + k]
        s = _conv_act(acc, w["cb"], w["lng"], w["lnb"])
        vbuf[j * SB:(j + 1) * SB, :] = s.astype(BF16)

    pooled = []
    for g, win in enumerate(POOL_WINDOWS):
        lo, hi = g * POOL_GROUP_IN, (g + 1) * POOL_GROUP_IN
        parts = []
        for j in range(dec_seq):
            cur = pfull[POOL_BUF + j, :, lo:hi]
            ssum = cur
            for i in range(1, win):
                ssum = ssum + pfull[POOL_BUF + j - i, :, lo:hi]
            cnt = float(min(pos0 + j + 1, win))
            parts.append(ssum / cnt - cur)
        pooled.append(jnp.concatenate(parts, axis=0))

    x1, h2, tw, pk8, runs, new_carry = _merge_and_route(
        x, vbuf[...], pooled, gc, gp, gt1, sh2, sc2, carry[0:1, :], w)
    x1_ref[...] = x1
    _store_token_tiles(h2_ref, h2)
    tw_ref[...] = tw
    pk_ref[...] = pk8
    runs_ref[...] = runs
    carry[...] = jnp.broadcast_to(new_carry, (SUBLANES, LANES))
    cnt_ref[...] = jnp.broadcast_to(new_carry, (SUBLANES, LANES))


def _sample_mixer(l, x, mod, conv_state, pool_state, cnt0, weights, dec_batch, dec_seq, pos0):
    rows = dec_seq * SB
    n_s = dec_batch * dec_seq
    row = lambda width: pl.BlockSpec((rows, width), lambda i: (i, 0))
    conv_spec = pl.BlockSpec((None, CONV_BUF, SB, D_CONV), lambda i: (i, 0, 0, 0))
    pool_spec = pl.BlockSpec((None, POOL_BUF, SB, D_POOL), lambda i: (i, 0, 0, 0))
    cnt_spec = pl.BlockSpec((SUBLANES, LANES), lambda i: (0, 0))
    in_specs = [row(D_MODEL), pl.BlockSpec((N_MOD, SB, D_MODEL), lambda i: (0, i, 0)),
                conv_spec, pool_spec, cnt_spec]
    in_specs += _weight_specs(l, 1)
    out_shape = (
        jax.ShapeDtypeStruct((n_s, D_MODEL), F32),
        jax.ShapeDtypeStruct((n_s * ROW_TILE, LANES), F32),
        jax.ShapeDtypeStruct((n_s, LANES), F32),
        jax.ShapeDtypeStruct((SUBLANES, n_s), I32),
        jax.ShapeDtypeStruct((n_s // rows * SUBLANES, LANES), I32),
        jax.ShapeDtypeStruct((SUBLANES, LANES), F32),
        jax.ShapeDtypeStruct(conv_state.shape, F32),
        jax.ShapeDtypeStruct(pool_state.shape, F32),
    )
    out_specs = (row(D_MODEL), pl.BlockSpec((rows * ROW_TILE, LANES), lambda i: (i, 0)), row(LANES),
                 pl.BlockSpec((SUBLANES, rows), lambda i: (0, i)),
                 pl.BlockSpec((SUBLANES, LANES), lambda i: (i, 0)), cnt_spec, conv_spec, pool_spec)
    return pl.pallas_call(
        functools.partial(_sample_mixer_kernel, dec_seq=dec_seq, pos0=pos0),
        grid=(dec_batch // SB,),
        in_specs=in_specs,
        out_specs=out_specs,
        out_shape=out_shape,
        scratch_shapes=[
            pltpu.VMEM((CONV_BUF + dec_seq, SB, D_CONV), F32),
            pltpu.VMEM((POOL_BUF + dec_seq, SB, D_POOL), F32),
            pltpu.VMEM((rows, D_CONV), BF16),
            pltpu.VMEM((SUBLANES, LANES), F32),
        ],
        compiler_params=pltpu.CompilerParams(
            dimension_semantics=("arbitrary",), vmem_limit_bytes=VMEM_LIMIT),
        name=f"sample_mixer_l{l}",
    )(x, mod, conv_state, pool_state, cnt0, *weights)


RUN_ROWS = 3


def _table_copies(pk_hbm, runs_hbm, tile, pos, runs, slot, sem):
    copies = [
        pltpu.make_async_copy(pk_hbm.at[TOP_K + k, pl.ds(tile * TT, TT)],
                              pos.at[pl.ds((slot * TOP_K + k) * TT, TT)], sem)
        for k in range(TOP_K)]
    copies += [
        pltpu.make_async_copy(runs_hbm.at[tile * SUBLANES + r],
                              runs.at[pl.ds((slot * RUN_ROWS + r) * LANES, LANES)], sem)
        for r in range(RUN_ROWS)]
    return copies


def _run_copies(pstart_ref, runs, slot, sorted_hbm, run_ref, sem, to_hbm):
    def body(e, carry):
        count = runs[(slot * RUN_ROWS + 1) * LANES + e]

        @pl.when(count > 0)
        def _():
            glob = pl.multiple_of((pstart_ref[e] + runs[slot * RUN_ROWS * LANES + e]) * ROW_TILE, ROW_TILE)
            loc = pl.multiple_of(runs[(slot * RUN_ROWS + 2) * LANES + e] * ROW_TILE, ROW_TILE)
            hbm = sorted_hbm.at[pl.ds(glob, count * ROW_TILE), :]
            vmem = run_ref.at[pl.ds(loc, count * ROW_TILE), :]
            if to_hbm:
                pltpu.make_async_copy(vmem, hbm, sem).start()
            else:
                pltpu.make_async_copy(hbm, vmem, sem).start()
        return carry
    jax.lax.fori_loop(0, N_EXPERTS, body, 0, unroll=2)


def _dispatch_kernel(pstart_ref, pk_hbm, runs_hbm, h2p_ref, h2s_ref, xs_hbm, xrun0, xrun1, pos, runs,
                     isem, rsem, *, n_prompt_tiles):
    c = pl.program_id(0)
    n = pl.num_programs(0)
    xrun = (xrun0, xrun1)

    def tables(tile, slot):
        return _table_copies(pk_hbm, runs_hbm, tile, pos, runs, slot, isem.at[slot])

    def wait_runs(slot):
        pltpu.make_async_copy(xrun[slot], xs_hbm.at[pl.ds(0, TOP_K * TT * ROW_TILE), :], rsem.at[slot]).wait()

    @pl.when(c == 0)
    def _():
        for cp in tables(0, 0):
            cp.start()

    for s in range(2):
        @pl.when(c % 2 == s)
        def _():
            for cp in tables(c, s):
                cp.wait()

            @pl.when(c + 1 < n)
            def _():
                for cp in tables(c + 1, 1 - s):
                    cp.start()

            @pl.when(c >= 2)
            def _():
                wait_runs(s)

            def place_from(src_ref):
                def place(t, carry):
                    for k in range(TOP_K):
                        p = pos[(s * TOP_K + k) * TT + t]
                        _tile_at_row(xrun[s], p)[...] = _token_tile(src_ref, t)[...]
                    return carry
                jax.lax.fori_loop(0, TT, place, 0, unroll=8)

            @pl.when(c < n_prompt_tiles)
            def _():
                place_from(h2p_ref)

            @pl.when(c >= n_prompt_tiles)
            def _():
                place_from(h2s_ref)

            _run_copies(pstart_ref, runs, s, xs_hbm, xrun[s], rsem.at[s], to_hbm=True)

            @pl.when(c == n - 1)
            def _():
                @pl.when(c >= 1)
                def _():
                    wait_runs(1 - s)
                wait_runs(s)


def _dispatch(pstart, pk8, runs, h2p, h2s, n_slots):
    np_tiles = h2p.shape[0] // (TT * ROW_TILE)
    ns_tiles = h2s.shape[0] // (TT * ROW_TILE)
    tiles = pltpu.VMEM((TOP_K * TT * ROW_TILE, LANES), F32)
    grid_spec = pltpu.PrefetchScalarGridSpec(
        num_scalar_prefetch=1,
        grid=(np_tiles + ns_tiles,),
        in_specs=[
            pl.BlockSpec(memory_space=pl.ANY),
            pl.BlockSpec(memory_space=pl.ANY),
            pl.BlockSpec((TT * ROW_TILE, LANES), lambda c, ps: (jnp.minimum(c, np_tiles - 1), 0)),
            pl.BlockSpec((TT * ROW_TILE, LANES), lambda c, ps: (jnp.maximum(c - np_tiles, 0), 0)),
        ],
        out_specs=pl.BlockSpec(memory_space=pl.ANY),
        scratch_shapes=[
            tiles, tiles,
            pltpu.SMEM((2 * TOP_K * TT,), I32),
            pltpu.SMEM((2 * RUN_ROWS * LANES,), I32),
            pltpu.SemaphoreType.DMA((2,)),
            pltpu.SemaphoreType.DMA((2,)),
        ],
    )
    return pl.pallas_call(
        functools.partial(_dispatch_kernel, n_prompt_tiles=np_tiles),
        grid_spec=grid_spec,
        out_shape=jax.ShapeDtypeStruct((n_slots * ROW_TILE, LANES), F32),
        compiler_params=pltpu.CompilerParams(
            dimension_semantics=("arbitrary",), vmem_limit_bytes=VMEM_LIMIT),
        name="dispatch",
    )(pstart, pk8, runs, h2p, h2s)


def _experts_kernel(be_ref, nv_ref, nu_ref, nx_ref, x_ref, wup_hbm, bup_ref, wdn_hbm, bdn_ref, o_ref,
                    wup_f32, wdn_f32, wup_bf, wdn_bf, wsem, *, layer):
    b = pl.program_id(0)

    def weight_copies(e):
        return (pltpu.make_async_copy(wup_hbm.at[layer, e], wup_f32, wsem.at[0]),
                pltpu.make_async_copy(wdn_hbm.at[layer, e], wdn_f32, wsem.at[1]))

    @pl.when(b < nu_ref[0])
    def _():
        e = be_ref[b]
        prev = be_ref[jnp.maximum(b - 1, 0)]

        @pl.when((b == 0) | (e != prev))
        def _():
            @pl.when(b == 0)
            def _():
                for cp in weight_copies(e):
                    cp.start()
            for cp in weight_copies(e):
                cp.wait()
            for r in range(0, D_MODEL, LANES):
                wup_bf[r:r + LANES, :] = wup_f32[r:r + LANES, :].astype(BF16)
                wdn_bf[r:r + LANES, :] = wdn_f32[r:r + LANES, :].astype(BF16)
            nxt = nx_ref[e]

            @pl.when(nxt >= 0)
            def _():
                for cp in weight_copies(nxt):
                    cp.start()

        n_valid = nv_ref[b]

        def ffn(rows):
            rid = jax.lax.broadcasted_iota(I32, (rows, D_MODEL), 0)
            x = jnp.where(rid < n_valid, _load_token_tiles(x_ref, 0, rows), 0.0)
            a = _dot(x.astype(BF16), wup_bf[...]) + bup_ref[...]
            a_glu = jnp.minimum(a[:, :D_EXPERT], SWIGLU_LIMIT)
            a_lin = jnp.clip(a[:, D_EXPERT:], -SWIGLU_LIMIT, SWIGLU_LIMIT)
            o = a_glu * jax.nn.sigmoid(SWIGLU_ALPHA * a_glu) * (a_lin + 1.0)
            _store_token_tiles(o_ref, _dot(o.astype(BF16), wdn_bf[...]) + bdn_ref[...])
            if rows < BM:
                o_ref[rows * ROW_TILE:, :] = jnp.zeros(((BM - rows) * ROW_TILE, LANES), F32)

        for part in range(1, BM_PARTS + 1):
            rows = part * (BM // BM_PARTS)

            @pl.when((n_valid > rows - BM // BM_PARTS) & (n_valid <= rows))
            def _():
                ffn(rows)

    @pl.when(b >= nu_ref[0])
    def _():
        o_ref[...] = jnp.zeros((BM * ROW_TILE, LANES), F32)


def _experts(l, x_sorted, block_expert, n_valid, n_used, next_expert, w_up, b_up, w_down, b_down):
    n_slots = x_sorted.shape[0] // ROW_TILE
    nb = n_slots // BM

    def used_map(b, be, nv, nu, nx):
        return (jnp.minimum(b, nu[0] - 1), 0)

    bmap = lambda b, be, nv, nu, nx: (l, be[b], 0, 0)
    grid_spec = pltpu.PrefetchScalarGridSpec(
        num_scalar_prefetch=4,
        grid=(nb,),
        in_specs=[
            pl.BlockSpec((BM * ROW_TILE, LANES), used_map),
            pl.BlockSpec(memory_space=pl.ANY),
            pl.BlockSpec((None, None, 1, 2 * D_EXPERT), bmap),
            pl.BlockSpec(memory_space=pl.ANY),
            pl.BlockSpec((None, None, 1, D_MODEL), bmap),
        ],
        out_specs=pl.BlockSpec((BM * ROW_TILE, LANES), lambda b, be, nv, nu, nx: (b, 0)),
        scratch_shapes=[
            pltpu.VMEM((D_MODEL, 2 * D_EXPERT), F32),
            pltpu.VMEM((D_EXPERT, D_MODEL), F32),
            pltpu.VMEM((D_MODEL, 2 * D_EXPERT), BF16),
            pltpu.VMEM((D_EXPERT, D_MODEL), BF16),
            pltpu.SemaphoreType.DMA((2,)),
        ],
    )
    return pl.pallas_call(
        functools.partial(_experts_kernel, layer=l),
        grid_spec=grid_spec,
        out_shape=jax.ShapeDtypeStruct((n_slots * ROW_TILE, LANES), F32),
        compiler_params=pltpu.CompilerParams(
            dimension_semantics=("arbitrary",), vmem_limit_bytes=VMEM_LIMIT),
        name=f"experts_l{l}",
    )(block_expert, n_valid, n_used, next_expert, x_sorted, w_up,
      b_up.reshape(DEPTH, N_EXPERTS, 1, 2 * D_EXPERT), w_down,
      b_down.reshape(DEPTH, N_EXPERTS, 1, D_MODEL))


def _tables_kernel(cnt_ref, pstart_ref, be_ref, nv_ref, nu_ref, nx_ref, *, nb):
    def clear(b, c):
        be_ref[b] = N_EXPERTS - 1
        nv_ref[b] = 0
        return c
    jax.lax.fori_loop(0, nb, clear, 0)

    def per_expert(e, carry):
        blk, last = carry
        cnt = cnt_ref[e]
        n = (cnt + (BM - 1)) // BM
        pstart_ref[e] = blk * BM
        nx_ref[e] = -1

        def per_block(j, c):
            be_ref[blk + j] = e
            nv_ref[blk + j] = jnp.minimum(cnt - j * BM, BM)
            return c
        jax.lax.fori_loop(0, n, per_block, 0)

        @pl.when((n > 0) & (last >= 0))
        def _():
            nx_ref[last] = e
        return blk + n, jnp.where(n > 0, e, last)

    blk, _ = jax.lax.fori_loop(0, N_EXPERTS, per_expert, (jnp.int32(0), jnp.int32(-1)))
    nu_ref[0] = blk


def _block_tables(counts, nb):
    smem = pl.BlockSpec(memory_space=pltpu.SMEM)
    return pl.pallas_call(
        functools.partial(_tables_kernel, nb=nb),
        in_specs=[smem],
        out_specs=(smem, smem, smem, smem, smem),
        out_shape=(
            jax.ShapeDtypeStruct((N_EXPERTS,), I32),
            jax.ShapeDtypeStruct((nb,), I32),
            jax.ShapeDtypeStruct((nb,), I32),
            jax.ShapeDtypeStruct((1,), I32),
            jax.ShapeDtypeStruct((N_EXPERTS,), I32),
        ),
        name="block_tables",
    )(counts)


def _combine_kernel(pstart_ref, pk_hbm, runs_hbm, y_hbm, x1_ref, tw_ref, gt_ref, g_ref, o_ref,
                    yrun0, yrun1, ybuf, pos, runs, isem, rsem, *, gate_copies, final):
    i = pl.program_id(0)
    n = pl.num_programs(0)
    yrun = (yrun0, yrun1)

    def tables(tile, slot):
        return _table_copies(pk_hbm, runs_hbm, tile, pos, runs, slot, isem.at[slot])

    @pl.when(i == 0)
    def _():
        for cp in tables(0, 0):
            cp.start()
        for cp in tables(0, 0):
            cp.wait()
        _run_copies(pstart_ref, runs, 0, y_hbm, yrun[0], rsem.at[0], to_hbm=False)

        @pl.when(n > 1)
        def _():
            for cp in tables(1, 1):
                cp.start()

    for s in range(2):
        @pl.when(i % 2 == s)
        def _():
            @pl.when(i + 1 < n)
            def _():
                for cp in tables(i + 1, 1 - s):
                    cp.wait()
                _run_copies(pstart_ref, runs, 1 - s, y_hbm, yrun[1 - s], rsem.at[1 - s], to_hbm=False)

            pltpu.make_async_copy(y_hbm.at[pl.ds(0, TOP_K * TT * ROW_TILE), :], yrun[s], rsem.at[s]).wait()

            def place(t, carry):
                for k in range(TOP_K):
                    p = pos[(s * TOP_K + k) * TT + t]
                    _token_tile(ybuf, k * TT + t)[...] = _tile_at_row(yrun[s], p)[...]
                return carry
            jax.lax.fori_loop(0, TT, place, 0, unroll=8)

            @pl.when(i + 2 < n)
            def _():
                for cp in tables(i + 2, s):
                    cp.start()

            tw = tw_ref[...]
            moe = tw[:, 0:1] * _load_token_tiles(ybuf, 0, TT)
            for k in range(1, TOP_K):
                moe = moe + tw[:, k:k + 1] * _load_token_tiles(ybuf, k * TT, TT)
            gt = gt_ref[...]
            if gate_copies > 1:
                gt = jnp.concatenate([gt] * gate_copies, axis=0)
            x = x1_ref[...] + gt * moe
            o_ref[...] = _rms(x, g_ref[...]) if final else x


def _combine(pstart, pk8, runs, y_sorted, x1, tw, gate, norm_final, *, gate_spec, gate_copies, final, name):
    n = x1.shape[0]
    row = lambda width: pl.BlockSpec((TT, width), lambda i, ps: (i, 0))
    tiles = pltpu.VMEM((TOP_K * TT * ROW_TILE, LANES), F32)
    grid_spec = pltpu.PrefetchScalarGridSpec(
        num_scalar_prefetch=1,
        grid=(n // TT,),
        in_specs=[
            pl.BlockSpec(memory_space=pl.ANY),
            pl.BlockSpec(memory_space=pl.ANY),
            pl.BlockSpec(memory_space=pl.ANY),
            row(D_MODEL), row(LANES), gate_spec,
            pl.BlockSpec((1, D_MODEL), lambda i, ps: (0, 0)),
        ],
        out_specs=row(D_MODEL),
        scratch_shapes=[
            tiles, tiles, tiles,
            pltpu.SMEM((2 * TOP_K * TT,), I32),
            pltpu.SMEM((2 * RUN_ROWS * LANES,), I32),
            pltpu.SemaphoreType.DMA((2,)),
            pltpu.SemaphoreType.DMA((2,)),
        ],
    )
    return pl.pallas_call(
        functools.partial(_combine_kernel, gate_copies=gate_copies, final=final),
        grid_spec=grid_spec,
        out_shape=jax.ShapeDtypeStruct((n, D_MODEL), F32),
        compiler_params=pltpu.CompilerParams(
            dimension_semantics=("arbitrary",), vmem_limit_bytes=VMEM_LIMIT),
        name=name,
    )(pstart, pk8, runs, y_sorted, x1, tw, gate, norm_final.reshape(1, D_MODEL))


def kernel(x_prompt, x_sample, state_conv, state_pool, c_prompt, c_sample, w_ada, b_ada, norm_mix, w_in, b_in, conv_w, conv_b, conv_ln_g, conv_ln_b, w_conv_out, w_pool, pool_scale, w_out, norm_ffn, w_router, b_router, w_up, b_up, w_down, b_down, norm_final):
    batch, seq, _ = x_prompt.shape
    dec_batch, dec_seq, _ = x_sample.shape
    n_p = batch * seq
    n_s = dec_batch * dec_seq
    n_blocks = dec_batch // SB
    assert dec_seq * SB == TT and seq % TT == 0 and dec_batch % SB == 0
    n_assign = (n_p + n_s) * TOP_K
    nb = n_assign // BM + N_EXPERTS

    vec = lambda a: a.reshape(DEPTH, 1, a.shape[-1])
    weights = (
        vec(norm_mix), w_in.astype(BF16), vec(b_in), conv_w, vec(conv_b), vec(conv_ln_g),
        vec(conv_ln_b), w_conv_out.astype(BF16), w_pool.astype(BF16), vec(pool_scale),
        w_out.astype(BF16), vec(norm_ffn),
        jnp.pad(w_router, ((0, 0), (0, 0), (0, LANES - N_EXPERTS))).astype(BF16),
        vec(jnp.pad(b_router, ((0, 0), (0, LANES - N_EXPERTS)))),
    )

    mod = _adaln(jnp.concatenate([c_prompt, c_sample], axis=0), w_ada, b_ada)
    mod_p = mod[:, :, :batch]
    mod_s = mod[:, :, batch:]

    def to_sample_order(a):
        lead = a.shape[:-3]
        a = a.reshape(lead + (n_blocks, SB) + a.shape[-2:])
        return jnp.swapaxes(a, -3, -2)

    def from_sample_order(a):
        a = jnp.swapaxes(a, -3, -2)
        return a.reshape(a.shape[:-4] + (dec_batch,) + a.shape[-2:])

    xp = x_prompt.reshape(n_p, D_MODEL)
    xs = to_sample_order(x_sample).reshape(n_s, D_MODEL)
    conv_state = to_sample_order(state_conv)
    pool_state = to_sample_order(state_pool)

    gate_p = pl.BlockSpec((None, 1, D_MODEL), lambda i, ps: (i // (seq // TT), 0, 0))
    gate_s = pl.BlockSpec((SB, D_MODEL), lambda i, ps: (i, 0))

    new_conv_p, new_pool_p, new_conv_s, new_pool_s = [], [], [], []
    for l in range(DEPTH):
        x1p, h2p, twp, pkp, runsp, cntp, ncp, npp = _prompt_mixer(l, xp, mod_p[l], weights, batch, seq)
        x1s, h2s, tws, pks, runss, cnt, ncs, nps = _sample_mixer(
            l, xs, mod_s[l], conv_state[l], pool_state[l], cntp, weights, dec_batch, dec_seq, PAST_LEN)
        new_conv_p.append(ncp)
        new_pool_p.append(npp)
        new_conv_s.append(ncs)
        new_pool_s.append(nps)

        counts = cnt[0, :N_EXPERTS].astype(I32)
        pstart, block_expert, n_valid, n_used, next_expert = _block_tables(counts, nb)
        x_sorted = _dispatch(pstart, jnp.concatenate([pkp, pks], axis=1),
                             jnp.concatenate([runsp, runss], axis=0), h2p, h2s, nb * BM)
        y_sorted = _experts(l, x_sorted, block_expert, n_valid, n_used, next_expert, w_up, b_up, w_down,
                            b_down)

        final = l == DEPTH - 1
        xp = _combine(pstart, pkp, runsp, y_sorted, x1p, twp, mod_p[l][5][:, None, :], norm_final,
                      gate_spec=gate_p, gate_copies=1, final=final, name=f"combine_prompt_l{l}")
        xs = _combine(pstart, pks, runss, y_sorted, x1s, tws, mod_s[l][5], norm_final,
                      gate_spec=gate_s, gate_copies=dec_seq, final=final, name=f"combine_sample_l{l}")

    y_prompt = xp.reshape(batch, seq, D_MODEL)
    y_sample = from_sample_order(xs.reshape(n_blocks, dec_seq, SB, D_MODEL))
    return (y_prompt, y_sample, jnp.stack(new_conv_p), jnp.stack(new_pool_p),
            from_sample_order(jnp.stack(new_conv_s)), from_sample_order(jnp.stack(new_pool_s)))
```

```python
import functools

import jax
import jax.numpy as jnp
from jax.experimental import pallas as pl
from jax.experimental.pallas import tpu as pltpu

F32 = jnp.float32
BF16 = jnp.bfloat16
I32 = jnp.int32

D_MODEL = 1024
DEPTH = 2
D_CONV = 512
CONV_WIDTH = 31
CONV_BUF = CONV_WIDTH - 1
D_POOL = 512
POOL_WINDOWS = (2, 4, 8, 16)
POOL_GROUP_IN = D_POOL // len(POOL_WINDOWS)
POOL_GROUP_OUT = D_MODEL // len(POOL_WINDOWS)
POOL_BUF = max(POOL_WINDOWS) - 1
IN_COLS = 2 * D_CONV + D_POOL + 2 * D_MODEL
N_EXPERTS = 32
TOP_K = 4
D_EXPERT = D_MODEL
SWIGLU_LIMIT = 7.0
SWIGLU_ALPHA = 1.702
N_MOD = 6
EPS = 1e-6
PAST_LEN = 16384

LANES = 128
SUBLANES = 8
VMEM_LIMIT = 52 * 1024 * 1024

TT = 256
NSEQ = 2
CONV_HALO = 32
POOL_HALO = 16
CONV_ROWS = 64
SB = 64
BM = 1024
BM_PASS = 512
BM_PARTS = 4
ROW_TILE = D_MODEL // LANES
EXPERT_BITS = 5
CONV_SHIFT_ROWS = TT + CONV_HALO - SUBLANES
assert N_EXPERTS == 1 << EXPERT_BITS and ROW_TILE == SUBLANES


def _rms(x, g):
    return x * jax.lax.rsqrt(jnp.mean(x * x, axis=-1, keepdims=True) + EPS) * g


def _dot(a, b):
    return jnp.dot(a, b, preferred_element_type=F32)


def _store_token_tiles(ref, v):
    rows = v.shape[0]
    for s in range(ROW_TILE):
        ref[pl.ds(s, rows, stride=ROW_TILE), :] = v[:, s * LANES:(s + 1) * LANES]


def _load_token_tiles(ref, first_token, rows):
    return jnp.concatenate(
        [ref[pl.ds(first_token * ROW_TILE + s, rows, stride=ROW_TILE), :] for s in range(ROW_TILE)], axis=-1)


def _tile_at_row(ref, row):
    return ref.at[pl.ds(pl.multiple_of(row, ROW_TILE), ROW_TILE), :]


def _token_tile(ref, t):
    return _tile_at_row(ref, t * ROW_TILE)


def _adaln_kernel(c_ref, w_ref, b_ref, o_ref):
    c = c_ref[...]
    a = (c * jax.nn.sigmoid(c)).astype(BF16)
    o_ref[...] = _dot(a, w_ref[...].astype(BF16)) + b_ref[...]


def _adaln(c_all, w_ada, b_ada):
    n = c_all.shape[0]
    cols = N_MOD * D_MODEL
    return pl.pallas_call(
        _adaln_kernel,
        grid=(DEPTH, N_MOD),
        in_specs=[
            pl.BlockSpec((n, D_MODEL), lambda l, j: (0, 0)),
            pl.BlockSpec((None, D_MODEL, D_MODEL), lambda l, j: (l, 0, j)),
            pl.BlockSpec((None, 1, D_MODEL), lambda l, j: (l, 0, j)),
        ],
        out_specs=pl.BlockSpec((None, None, n, D_MODEL), lambda l, j: (l, j, 0, 0)),
        out_shape=jax.ShapeDtypeStruct((DEPTH, N_MOD, n, D_MODEL), F32),
        compiler_params=pltpu.CompilerParams(
            dimension_semantics=("arbitrary", "arbitrary"), vmem_limit_bytes=VMEM_LIMIT),
        name="adaln",
    )(c_all, w_ada, b_ada.reshape(DEPTH, 1, cols))


def _in_proj(x, sh1, sc1, nmix_ref, win_ref, bin_ref):
    h = (_rms(x, nmix_ref[...]) * (1.0 + sc1) + sh1).astype(BF16)
    c0, c1, c2 = 2 * D_CONV, 2 * D_CONV + D_POOL, IN_COLS
    zu = _dot(h, win_ref[:, 0:c0]) + bin_ref[:, 0:c0]
    u = zu[:, :D_CONV] * jax.nn.sigmoid(zu[:, D_CONV:])
    up = _dot(h, win_ref[:, c0:c1]) + bin_ref[:, c0:c1]
    zg = _dot(h, win_ref[:, c1:c2]) + bin_ref[:, c1:c2]
    return u, up, zg[:, :D_MODEL], zg[:, D_MODEL:]


def _conv_act(acc, cb_ref, lng_ref, lnb_ref):
    v = acc + cb_ref[...]
    mu = jnp.mean(v, axis=-1, keepdims=True)
    d = v - mu
    var = jnp.mean(d * d, axis=-1, keepdims=True)
    vn = d * jax.lax.rsqrt(var + EPS) * lng_ref[...] + lnb_ref[...]
    return vn * jax.nn.sigmoid(vn)


def _merge_and_route(x, v_bf, pooled, gc, gp, gt1, sh2, sc2, carry, w):
    rows = x.shape[0]
    y_conv = _dot(v_bf, w["wco"][...])
    y_pool = jnp.concatenate(
        [_dot(pooled[g].astype(BF16), w["wpool"][g]) for g in range(len(POOL_WINDOWS))], axis=-1)
    y_pool = y_pool * w["pscale"][...]
    m = jax.nn.sigmoid(gc) * y_conv + jax.nn.sigmoid(gp) * y_pool
    x1 = x + gt1 * _dot(m.astype(BF16), w["wout"][...])
    h2 = _rms(x1, w["nffn"][...]) * (1.0 + sc2) + sh2
    logits = _dot(h2.astype(BF16), w["wr"][...]) + w["br"][...]
    lane = jax.lax.broadcasted_iota(I32, (rows, LANES), 1)
    lane_f = lane.astype(F32)
    neg = jnp.full((rows, LANES), -jnp.inf, F32)
    l = jnp.where(lane < N_EXPERTS, logits, neg)
    vals, idxs, sels = [], [], []
    for _ in range(TOP_K):
        mx = jnp.max(l, axis=-1, keepdims=True)
        ix = jnp.min(jnp.where(l == mx, lane_f, float(LANES)), axis=-1, keepdims=True)
        sel = lane_f == ix
        l = jnp.where(sel, neg, l)
        vals.append(mx)
        idxs.append(ix)
        sels.append(sel)
    es = [jnp.exp(v - vals[0]) for v in vals]
    den = es[0] + es[1] + es[2] + es[3]

    onehot = jnp.where(sels[0] | sels[1] | sels[2] | sels[3], 1.0, 0.0)
    r_i = jax.lax.broadcasted_iota(I32, (rows, rows), 0)
    c_i = jax.lax.broadcasted_iota(I32, (rows, rows), 1)
    before = jnp.where(c_i < r_i, 1.0, 0.0).astype(BF16)
    earlier = _dot(before, onehot.astype(BF16))
    count = jnp.sum(onehot, axis=0, keepdims=True)
    new_carry = carry + count
    e_i = jax.lax.broadcasted_iota(I32, (LANES, LANES), 0)
    e_j = jax.lax.broadcasted_iota(I32, (LANES, LANES), 1)
    lower_expert = jnp.where(e_i < e_j, 1.0, 0.0).astype(BF16)
    first = _dot(jnp.broadcast_to(count, (SUBLANES, LANES)).astype(BF16), lower_expert)[0:1, :]

    tw = jnp.zeros((rows, LANES), F32)
    pk = jnp.zeros((rows, LANES), F32)
    for k in range(TOP_K):
        in_tile = jnp.sum(jnp.where(sels[k], earlier, 0.0), axis=-1, keepdims=True)
        prior = jnp.sum(jnp.where(sels[k], carry, 0.0), axis=-1, keepdims=True)
        start = jnp.sum(jnp.where(sels[k], first, 0.0), axis=-1, keepdims=True)
        tw = jnp.where(lane == k, es[k] / den, tw)
        pk = jnp.where(lane == k, (prior + in_tile) * float(N_EXPERTS) + idxs[k], pk)
        pk = jnp.where(lane == TOP_K + k, (start + in_tile) * float(ROW_TILE), pk)
    pk8 = jnp.transpose(pk)[0:SUBLANES, :].astype(I32)
    runs = jnp.concatenate(
        [carry, count, first, jnp.zeros((SUBLANES - 3, LANES), F32)], axis=0).astype(I32)
    return x1, h2, tw, pk8, runs, new_carry


_WEIGHT_NAMES = ("nmix", "win", "bin", "cw", "cb", "lng", "lnb", "wco", "wpool", "pscale", "wout",
                 "nffn", "wr", "br")


def _weight_specs(l, n_grid):
    def spec(*shape):
        zeros = (0,) * len(shape)
        if n_grid == 2:
            return pl.BlockSpec((None,) + shape, lambda b, t: (l,) + zeros)
        return pl.BlockSpec((None,) + shape, lambda i: (l,) + zeros)
    return [
        spec(1, D_MODEL),
        spec(D_MODEL, IN_COLS),
        spec(1, IN_COLS),
        spec(CONV_WIDTH, D_CONV),
        spec(1, D_CONV),
        spec(1, D_CONV),
        spec(1, D_CONV),
        spec(D_CONV, D_MODEL),
        spec(len(POOL_WINDOWS), POOL_GROUP_IN, POOL_GROUP_OUT),
        spec(1, D_MODEL),
        spec(D_MODEL, D_MODEL),
        spec(1, D_MODEL),
        spec(D_MODEL, LANES),
        spec(1, LANES),
    ]


def _prompt_mixer_kernel(x_ref, mod_ref, *refs):
    w = dict(zip(_WEIGHT_NAMES, refs[:len(_WEIGHT_NAMES)]))
    (x1_ref, h2_ref, tw_ref, pk_ref, runs_ref, cnt_ref, nconv_ref, npool_ref,
     uhist, ushift, phist, vbuf, carry) = refs[len(_WEIGHT_NAMES):]

    b = pl.program_id(0)
    t = pl.program_id(1)
    nt = pl.num_programs(1)

    @pl.when((b == 0) & (t == 0))
    def _():
        carry[...] = jnp.zeros((SUBLANES, LANES), F32)

    @pl.when(t == 0)
    def _():
        for q in range(NSEQ):
            uhist[q, 0:CONV_HALO, :] = jnp.zeros((CONV_HALO, D_CONV), F32)
            phist[q, 0:POOL_HALO, :] = jnp.zeros((POOL_HALO, D_POOL), F32)

    gates = []
    for q in range(NSEQ):
        u, up, gc, gp = _in_proj(x_ref[q], mod_ref[0, q], mod_ref[1, q], w["nmix"], w["win"], w["bin"])
        uhist[q, CONV_HALO:CONV_HALO + TT, :] = u
        phist[q, POOL_HALO:POOL_HALO + TT, :] = up
        gates.append((gc, gp))

    def conv_and_pool(q):
        for r in range(1, SUBLANES):
            ushift[q, r - 1] = uhist[q, r:r + CONV_SHIFT_ROWS, :]
        for c in range(TT // CONV_ROWS):
            acc = jnp.zeros((CONV_ROWS, D_CONV), F32)
            for k in range(CONV_WIDTH):
                qq, r = divmod(CONV_HALO - CONV_BUF + k, SUBLANES)
                start = qq * SUBLANES + c * CONV_ROWS
                if r == 0:
                    tap = uhist[q, start:start + CONV_ROWS, :]
                else:
                    tap = ushift[q, r - 1, start:start + CONV_ROWS, :]
                acc = acc + w["cw"][k:k + 1, :] * tap
            s = _conv_act(acc, w["cb"], w["lng"], w["lnb"])
            vbuf[q, c * CONV_ROWS:(c + 1) * CONV_ROWS, :] = s.astype(BF16)

        pos = t * TT + jax.lax.broadcasted_iota(I32, (TT, 1), 0)
        pooled = []
        for g, win in enumerate(POOL_WINDOWS):
            lo, hi = g * POOL_GROUP_IN, (g + 1) * POOL_GROUP_IN
            cur = phist[q, POOL_HALO:POOL_HALO + TT, lo:hi]
            ssum = cur
            for i in range(1, win):
                ssum = ssum + phist[q, POOL_HALO - i:POOL_HALO - i + TT, lo:hi]
            cnt = jnp.minimum(pos + 1, win).astype(F32)
            pooled.append(ssum / cnt - cur)
        return pooled

    cur_carry = carry[0:1, :]
    for q in range(NSEQ):
        pooled = conv_and_pool(q)
        gc, gp = gates[q]
        x1, h2, tw, pk8, runs, cur_carry = _merge_and_route(
            x_ref[q], vbuf[q], pooled, gc, gp, mod_ref[2, q], mod_ref[3, q], mod_ref[4, q], cur_carry, w)
        x1_ref[q] = x1
        _store_token_tiles(h2_ref.at[q], h2)
        tw_ref[q] = tw
        pk_ref[q] = pk8
        runs_ref[q] = runs

        uhist[q, 0:CONV_HALO, :] = uhist[q, TT:TT + CONV_HALO, :]
        phist[q, 0:POOL_HALO, :] = phist[q, TT:TT + POOL_HALO, :]

    carry[...] = jnp.broadcast_to(cur_carry, (SUBLANES, LANES))
    cnt_ref[...] = jnp.broadcast_to(cur_carry, (SUBLANES, LANES))

    @pl.when(t == nt - 1)
    def _():
        for q in range(NSEQ):
            nconv_ref[q] = uhist[q, CONV_HALO + TT - CONV_BUF:CONV_HALO + TT, :]
            npool_ref[q] = phist[q, POOL_HALO + TT - POOL_BUF:POOL_HALO + TT, :]


def _prompt_mixer(l, x, mod, weights, batch, seq):
    nt = seq // TT
    n_tok = batch * seq
    half = n_tok // NSEQ
    row = lambda rows, width: pl.BlockSpec((NSEQ, rows, width), lambda b, t: (0, b * nt + t, 0))
    in_specs = [row(TT, D_MODEL),
                pl.BlockSpec((N_MOD, NSEQ, None, 1, D_MODEL), lambda b, t: (0, 0, b, 0, 0))]
    in_specs += _weight_specs(l, 2)
    out_shape = (
        jax.ShapeDtypeStruct((NSEQ, half, D_MODEL), F32),
        jax.ShapeDtypeStruct((NSEQ, half * ROW_TILE, LANES), F32),
        jax.ShapeDtypeStruct((NSEQ, half, LANES), F32),
        jax.ShapeDtypeStruct((NSEQ, SUBLANES, half), I32),
        jax.ShapeDtypeStruct((NSEQ, half // TT * SUBLANES, LANES), I32),
        jax.ShapeDtypeStruct((SUBLANES, LANES), F32),
        jax.ShapeDtypeStruct((NSEQ, batch // NSEQ, CONV_BUF, D_CONV), F32),
        jax.ShapeDtypeStruct((NSEQ, batch // NSEQ, POOL_BUF, D_POOL), F32),
    )
    out_specs = (
        row(TT, D_MODEL), row(TT * ROW_TILE, LANES), row(TT, LANES),
        pl.BlockSpec((NSEQ, SUBLANES, TT), lambda b, t: (0, 0, b * nt + t)),
        row(SUBLANES, LANES),
        pl.BlockSpec((SUBLANES, LANES), lambda b, t: (0, 0)),
        pl.BlockSpec((NSEQ, None, CONV_BUF, D_CONV), lambda b, t: (0, b, 0, 0)),
        pl.BlockSpec((NSEQ, None, POOL_BUF, D_POOL), lambda b, t: (0, b, 0, 0)),
    )
    x1, h2, tw, pk8, runs, cnt, nconv, npool = pl.pallas_call(
        _prompt_mixer_kernel,
        grid=(batch // NSEQ, nt),
        in_specs=in_specs,
        out_specs=out_specs,
        out_shape=out_shape,
        scratch_shapes=[
            pltpu.VMEM((NSEQ, CONV_HALO + TT, D_CONV), F32),
            pltpu.VMEM((NSEQ, SUBLANES - 1, CONV_SHIFT_ROWS, D_CONV), F32),
            pltpu.VMEM((NSEQ, POOL_HALO + TT, D_POOL), F32),
            pltpu.VMEM((NSEQ, TT, D_CONV), BF16),
            pltpu.VMEM((SUBLANES, LANES), F32),
        ],
        compiler_params=pltpu.CompilerParams(
            dimension_semantics=("arbitrary", "arbitrary"), vmem_limit_bytes=VMEM_LIMIT),
        name=f"prompt_mixer_l{l}",
    )(x.reshape(NSEQ, half, D_MODEL), mod.reshape(N_MOD, NSEQ, batch // NSEQ, 1, D_MODEL), *weights)
    return (x1.reshape(n_tok, D_MODEL), h2.reshape(n_tok * ROW_TILE, LANES), tw.reshape(n_tok, LANES),
            jnp.transpose(pk8, (1, 0, 2)).reshape(SUBLANES, n_tok),
            runs.reshape(n_tok // TT * SUBLANES, LANES), cnt,
            nconv.reshape(batch, CONV_BUF, D_CONV), npool.reshape(batch, POOL_BUF, D_POOL))


def _sample_mixer_kernel(x_ref, mod_ref, cs_ref, ps_ref, cnt0_ref, *refs, dec_seq, pos0):
    w = dict(zip(_WEIGHT_NAMES, refs[:len(_WEIGHT_NAMES)]))
    (x1_ref, h2_ref, tw_ref, pk_ref, runs_ref, cnt_ref, nconv_ref, npool_ref,
     ufull, pfull, vbuf, carry) = refs[len(_WEIGHT_NAMES):]

    def per_row(v):
        return jnp.concatenate([v] * dec_seq, axis=0)

    @pl.when(pl.program_id(0) == 0)
    def _():
        carry[...] = cnt0_ref[...]

    x = x_ref[...]
    sh1, sc1, gt1 = per_row(mod_ref[0]), per_row(mod_ref[1]), per_row(mod_ref[2])
    sh2, sc2 = per_row(mod_ref[3]), per_row(mod_ref[4])

    u, up, gc, gp = _in_proj(x, sh1, sc1, w["nmix"], w["win"], w["bin"])
    ufull[0:CONV_BUF] = cs_ref[...]
    pfull[0:POOL_BUF] = ps_ref[...]
    for j in range(dec_seq):
        ufull[CONV_BUF + j] = u[j * SB:(j + 1) * SB, :]
        pfull[POOL_BUF + j] = up[j * SB:(j + 1) * SB, :]
    nconv_ref[...] = ufull[dec_seq:dec_seq + CONV_BUF]
    npool_ref[...] = pfull[dec_seq:dec_seq + POOL_BUF]

    for j in range(dec_seq):
        acc = jnp.zeros((SB, D_CONV), F32)
        for k in range(CONV_WIDTH):
            acc = acc + w["cw"][k:k + 1, :] * ufull[j + k]
        s = _conv_act(acc, w["cb"], w["lng"], w["lnb"])
        vbuf[j * SB:(j + 1) * SB, :] = s.astype(BF16)

    pooled = []
    for g, win in enumerate(POOL_WINDOWS):
        lo, hi = g * POOL_GROUP_IN, (g + 1) * POOL_GROUP_IN
        parts = []
        for j in range(dec_seq):
            cur = pfull[POOL_BUF + j, :, lo:hi]
            ssum = cur
            for i in range(1, win):
                ssum = ssum + pfull[POOL_BUF + j - i, :, lo:hi]
            cnt = float(min(pos0 + j + 1, win))
            parts.append(ssum / cnt - cur)
        pooled.append(jnp.concatenate(parts, axis=0))

    x1, h2, tw, pk8, runs, new_carry = _merge_and_route(
        x, vbuf[...], pooled, gc, gp, gt1, sh2, sc2, carry[0:1, :], w)
    x1_ref[...] = x1
    _store_token_tiles(h2_ref, h2)
    tw_ref[...] = tw
    pk_ref[...] = pk8
    runs_ref[...] = runs
    carry[...] = jnp.broadcast_to(new_carry, (SUBLANES, LANES))
    cnt_ref[...] = jnp.broadcast_to(new_carry, (SUBLANES, LANES))


def _sample_mixer(l, x, mod, conv_state, pool_state, cnt0, weights, dec_batch, dec_seq, pos0):
    rows = dec_seq * SB
    n_s = dec_batch * dec_seq
    row = lambda width: pl.BlockSpec((rows, width), lambda i: (i, 0))
    conv_spec = pl.BlockSpec((None, CONV_BUF, SB, D_CONV), lambda i: (i, 0, 0, 0))
    pool_spec = pl.BlockSpec((None, POOL_BUF, SB, D_POOL), lambda i: (i, 0, 0, 0))
    cnt_spec = pl.BlockSpec((SUBLANES, LANES), lambda i: (0, 0))
    in_specs = [row(D_MODEL), pl.BlockSpec((N_MOD, SB, D_MODEL), lambda i: (0, i, 0)),
                conv_spec, pool_spec, cnt_spec]
    in_specs += _weight_specs(l, 1)
    out_shape = (
        jax.ShapeDtypeStruct((n_s, D_MODEL), F32),
        jax.ShapeDtypeStruct((n_s * ROW_TILE, LANES), F32),
        jax.ShapeDtypeStruct((n_s, LANES), F32),
        jax.ShapeDtypeStruct((SUBLANES, n_s), I32),
        jax.ShapeDtypeStruct((n_s // rows * SUBLANES, LANES), I32),
        jax.ShapeDtypeStruct((SUBLANES, LANES), F32),
        jax.ShapeDtypeStruct(conv_state.shape, F32),
        jax.ShapeDtypeStruct(pool_state.shape, F32),
    )
    out_specs = (row(D_MODEL), pl.BlockSpec((rows * ROW_TILE, LANES), lambda i: (i, 0)), row(LANES),
                 pl.BlockSpec((SUBLANES, rows), lambda i: (0, i)),
                 pl.BlockSpec((SUBLANES, LANES), lambda i: (i, 0)), cnt_spec, conv_spec, pool_spec)
    return pl.pallas_call(
        functools.partial(_sample_mixer_kernel, dec_seq=dec_seq, pos0=pos0),
        grid=(dec_batch // SB,),
        in_specs=in_specs,
        out_specs=out_specs,
        out_shape=out_shape,
        scratch_shapes=[
            pltpu.VMEM((CONV_BUF + dec_seq, SB, D_CONV), F32),
            pltpu.VMEM((POOL_BUF + dec_seq, SB, D_POOL), F32),
            pltpu.VMEM((rows, D_CONV), BF16),
            pltpu.VMEM((SUBLANES, LANES), F32),
        ],
        compiler_params=pltpu.CompilerParams(
            dimension_semantics=("arbitrary",), vmem_limit_bytes=VMEM_LIMIT),
        name=f"sample_mixer_l{l}",
    )(x, mod, conv_state, pool_state, cnt0, *weights)


RUN_ROWS = 3


def _table_copies(pk_hbm, runs_hbm, tile, pos, runs, slot, sem):
    copies = [
        pltpu.make_async_copy(pk_hbm.at[TOP_K + k, pl.ds(tile * TT, TT)],
                              pos.at[pl.ds((slot * TOP_K + k) * TT, TT)], sem)
        for k in range(TOP_K)]
    copies += [
        pltpu.make_async_copy(runs_hbm.at[tile * SUBLANES + r],
                              runs.at[pl.ds((slot * RUN_ROWS + r) * LANES, LANES)], sem)
        for r in range(RUN_ROWS)]
    return copies


def _run_copies(pstart_ref, runs, slot, sorted_hbm, run_ref, sem, to_hbm):
    def body(e, carry):
        count = runs[(slot * RUN_ROWS + 1) * LANES + e]

        @pl.when(count > 0)
        def _():
            glob = pl.multiple_of((pstart_ref[e] + runs[slot * RUN_ROWS * LANES + e]) * ROW_TILE, ROW_TILE)
            loc = pl.multiple_of(runs[(slot * RUN_ROWS + 2) * LANES + e] * ROW_TILE, ROW_TILE)
            hbm = sorted_hbm.at[pl.ds(glob, count * ROW_TILE), :]
            vmem = run_ref.at[pl.ds(loc, count * ROW_TILE), :]
            if to_hbm:
                pltpu.make_async_copy(vmem, hbm, sem).start()
            else:
                pltpu.make_async_copy(hbm, vmem, sem).start()
        return carry
    jax.lax.fori_loop(0, N_EXPERTS, body, 0, unroll=2)


def _dispatch_kernel(pstart_ref, pk_hbm, runs_hbm, h2p_ref, h2s_ref, xs_hbm, xrun0, xrun1, pos, runs,
                     isem, rsem, *, n_prompt_tiles):
    c = pl.program_id(0)
    n = pl.num_programs(0)
    xrun = (xrun0, xrun1)

    def tables(tile, slot):
        return _table_copies(pk_hbm, runs_hbm, tile, pos, runs, slot, isem.at[slot])

    def wait_runs(slot):
        pltpu.make_async_copy(xrun[slot], xs_hbm.at[pl.ds(0, TOP_K * TT * ROW_TILE), :], rsem.at[slot]).wait()

    @pl.when(c == 0)
    def _():
        for cp in tables(0, 0):
            cp.start()

    for s in range(2):
        @pl.when(c % 2 == s)
        def _():
            for cp in tables(c, s):
                cp.wait()

            @pl.when(c + 1 < n)
            def _():
                for cp in tables(c + 1, 1 - s):
                    cp.start()

            @pl.when(c >= 2)
            def _():
                wait_runs(s)

            def place_from(src_ref):
                def place(t, carry):
                    for k in range(TOP_K):
                        p = pos[(s * TOP_K + k) * TT + t]
                        _tile_at_row(xrun[s], p)[...] = _token_tile(src_ref, t)[...]
                    return carry
                jax.lax.fori_loop(0, TT, place, 0, unroll=8)

            @pl.when(c < n_prompt_tiles)
            def _():
                place_from(h2p_ref)

            @pl.when(c >= n_prompt_tiles)
            def _():
                place_from(h2s_ref)

            _run_copies(pstart_ref, runs, s, xs_hbm, xrun[s], rsem.at[s], to_hbm=True)

            @pl.when(c == n - 1)
            def _():
                @pl.when(c >= 1)
                def _():
                    wait_runs(1 - s)
                wait_runs(s)


def _dispatch(pstart, pk8, runs, h2p, h2s, n_slots):
    np_tiles = h2p.shape[0] // (TT * ROW_TILE)
    ns_tiles = h2s.shape[0] // (TT * ROW_TILE)
    tiles = pltpu.VMEM((TOP_K * TT * ROW_TILE, LANES), F32)
    grid_spec = pltpu.PrefetchScalarGridSpec(
        num_scalar_prefetch=1,
        grid=(np_tiles + ns_tiles,),
        in_specs=[
            pl.BlockSpec(memory_space=pl.ANY),
            pl.BlockSpec(memory_space=pl.ANY),
            pl.BlockSpec((TT * ROW_TILE, LANES), lambda c, ps: (jnp.minimum(c, np_tiles - 1), 0)),
            pl.BlockSpec((TT * ROW_TILE, LANES), lambda c, ps: (jnp.maximum(c - np_tiles, 0), 0)),
        ],
        out_specs=pl.BlockSpec(memory_space=pl.ANY),
        scratch_shapes=[
            tiles, tiles,
            pltpu.SMEM((2 * TOP_K * TT,), I32),
            pltpu.SMEM((2 * RUN_ROWS * LANES,), I32),
            pltpu.SemaphoreType.DMA((2,)),
            pltpu.SemaphoreType.DMA((2,)),
        ],
    )
    return pl.pallas_call(
        functools.partial(_dispatch_kernel, n_prompt_tiles=np_tiles),
        grid_spec=grid_spec,
        out_shape=jax.ShapeDtypeStruct((n_slots * ROW_TILE, LANES), F32),
        compiler_params=pltpu.CompilerParams(
            dimension_semantics=("arbitrary",), vmem_limit_bytes=VMEM_LIMIT),
        name="dispatch",
    )(pstart, pk8, runs, h2p, h2s)


def _experts_kernel(be_ref, nv_ref, nu_ref, nx_ref, x_ref, wup_hbm, bup_ref, wdn_hbm, bdn_ref, o_ref,
                    wup_f32, wdn_f32, wup_bf, wdn_bf, wsem, *, layer):
    b = pl.program_id(0)

    def weight_copies(e):
        return (pltpu.make_async_copy(wup_hbm.at[layer, e], wup_f32, wsem.at[0]),
                pltpu.make_async_copy(wdn_hbm.at[layer, e], wdn_f32, wsem.at[1]))

    @pl.when(b < nu_ref[0])
    def _():
        e = be_ref[b]
        prev = be_ref[jnp.maximum(b - 1, 0)]

        @pl.when((b == 0) | (e != prev))
        def _():
            @pl.when(b == 0)
            def _():
                for cp in weight_copies(e):
                    cp.start()
            for cp in weight_copies(e):
                cp.wait()
            for r in range(0, D_MODEL, LANES):
                wup_bf[r:r + LANES, :] = wup_f32[r:r + LANES, :].astype(BF16)
                wdn_bf[r:r + LANES, :] = wdn_f32[r:r + LANES, :].astype(BF16)
            nxt = nx_ref[e]

            @pl.when(nxt >= 0)
            def _():
                for cp in weight_copies(nxt):
                    cp.start()

        def ffn(first, rows, n_valid):
            out = o_ref.at[pl.ds(first * ROW_TILE, BM_PASS * ROW_TILE), :]
            rid = jax.lax.broadcasted_iota(I32, (rows, D_MODEL), 0)
            x = jnp.where(rid < n_valid, _load_token_tiles(x_ref, first, rows), 0.0)
            a = _dot(x.astype(BF16), wup_bf[...]) + bup_ref[...]
            a_glu = jnp.minimum(a[:, :D_EXPERT], SWIGLU_LIMIT)
            a_lin = jnp.clip(a[:, D_EXPERT:], -SWIGLU_LIMIT, SWIGLU_LIMIT)
            o = a_glu * jax.nn.sigmoid(SWIGLU_ALPHA * a_glu) * (a_lin + 1.0)
            _store_token_tiles(out, _dot(o.astype(BF16), wdn_bf[...]) + bdn_ref[...])
            if rows < BM_PASS:
                out[rows * ROW_TILE:, :] = jnp.zeros(((BM_PASS - rows) * ROW_TILE, LANES), F32)

        group = BM_PASS // BM_PARTS
        for first in range(0, BM, BM_PASS):
            n_valid = jnp.clip(nv_ref[b] - first, 0, BM_PASS)
            for part in range(1, BM_PARTS + 1):
                @pl.when((n_valid > (part - 1) * group) & (n_valid <= part * group))
                def _():
                    ffn(first, part * group, n_valid)

            @pl.when(n_valid == 0)
            def _():
                o_ref[first * ROW_TILE:(first + BM_PASS) * ROW_TILE, :] = jnp.zeros(
                    (BM_PASS * ROW_TILE, LANES), F32)

    @pl.when(b >= nu_ref[0])
    def _():
        o_ref[...] = jnp.zeros((BM * ROW_TILE, LANES), F32)


def _experts(l, x_sorted, block_expert, n_valid, n_used, next_expert, w_up, b_up, w_down, b_down):
    n_slots = x_sorted.shape[0] // ROW_TILE
    nb = n_slots // BM

    def used_map(b, be, nv, nu, nx):
        return (jnp.minimum(b, nu[0] - 1), 0)

    bmap = lambda b, be, nv, nu, nx: (l, be[b], 0, 0)
    grid_spec = pltpu.PrefetchScalarGridSpec(
        num_scalar_prefetch=4,
        grid=(nb,),
        in_specs=[
            pl.BlockSpec((BM * ROW_TILE, LANES), used_map),
            pl.BlockSpec(memory_space=pl.ANY),
            pl.BlockSpec((None, None, 1, 2 * D_EXPERT), bmap),
            pl.BlockSpec(memory_space=pl.ANY),
            pl.BlockSpec((None, None, 1, D_MODEL), bmap),
        ],
        out_specs=pl.BlockSpec((BM * ROW_TILE, LANES), lambda b, be, nv, nu, nx: (b, 0)),
        scratch_shapes=[
            pltpu.VMEM((D_MODEL, 2 * D_EXPERT), F32),
            pltpu.VMEM((D_EXPERT, D_MODEL), F32),
            pltpu.VMEM((D_MODEL, 2 * D_EXPERT), BF16),
            pltpu.VMEM((D_EXPERT, D_MODEL), BF16),
            pltpu.SemaphoreType.DMA((2,)),
        ],
    )
    return pl.pallas_call(
        functools.partial(_experts_kernel, layer=l),
        grid_spec=grid_spec,
        out_shape=jax.ShapeDtypeStruct((n_slots * ROW_TILE, LANES), F32),
        compiler_params=pltpu.CompilerParams(
            dimension_semantics=("arbitrary",), vmem_limit_bytes=VMEM_LIMIT),
        name=f"experts_l{l}",
    )(block_expert, n_valid, n_used, next_expert, x_sorted, w_up,
      b_up.reshape(DEPTH, N_EXPERTS, 1, 2 * D_EXPERT), w_down,
      b_down.reshape(DEPTH, N_EXPERTS, 1, D_MODEL))


def _tables_kernel(cnt_ref, pstart_ref, be_ref, nv_ref, nu_ref, nx_ref, *, nb):
    def clear(b, c):
        be_ref[b] = N_EXPERTS - 1
        nv_ref[b] = 0
        return c
    jax.lax.fori_loop(0, nb, clear, 0)

    def per_expert(e, carry):
        blk, last = carry
        cnt = cnt_ref[e]
        n = (cnt + (BM - 1)) // BM
        pstart_ref[e] = blk * BM
        nx_ref[e] = -1

        def per_block(j, c):
            be_ref[blk + j] = e
            nv_ref[blk + j] = jnp.minimum(cnt - j * BM, BM)
            return c
        jax.lax.fori_loop(0, n, per_block, 0)

        @pl.when((n > 0) & (last >= 0))
        def _():
            nx_ref[last] = e
        return blk + n, jnp.where(n > 0, e, last)

    blk, _ = jax.lax.fori_loop(0, N_EXPERTS, per_expert, (jnp.int32(0), jnp.int32(-1)))
    nu_ref[0] = blk


def _block_tables(counts, nb):
    smem = pl.BlockSpec(memory_space=pltpu.SMEM)
    return pl.pallas_call(
        functools.partial(_tables_kernel, nb=nb),
        in_specs=[smem],
        out_specs=(smem, smem, smem, smem, smem),
        out_shape=(
            jax.ShapeDtypeStruct((N_EXPERTS,), I32),
            jax.ShapeDtypeStruct((nb,), I32),
            jax.ShapeDtypeStruct((nb,), I32),
            jax.ShapeDtypeStruct((1,), I32),
            jax.ShapeDtypeStruct((N_EXPERTS,), I32),
        ),
        name="block_tables",
    )(counts)


def _combine_kernel(pstart_ref, pk_hbm, runs_hbm, y_hbm, x1_ref, tw_ref, gt_ref, g_ref, o_ref,
                    yrun0, yrun1, ybuf, pos, runs, isem, rsem, *, gate_copies, final):
    i = pl.program_id(0)
    n = pl.num_programs(0)
    yrun = (yrun0, yrun1)

    def tables(tile, slot):
        return _table_copies(pk_hbm, runs_hbm, tile, pos, runs, slot, isem.at[slot])

    @pl.when(i == 0)
    def _():
        for cp in tables(0, 0):
            cp.start()
        for cp in tables(0, 0):
            cp.wait()
        _run_copies(pstart_ref, runs, 0, y_hbm, yrun[0], rsem.at[0], to_hbm=False)

        @pl.when(n > 1)
        def _():
            for cp in tables(1, 1):
                cp.start()

    for s in range(2):
        @pl.when(i % 2 == s)
        def _():
            @pl.when(i + 1 < n)
            def _():
                for cp in tables(i + 1, 1 - s):
                    cp.wait()
                _run_copies(pstart_ref, runs, 1 - s, y_hbm, yrun[1 - s], rsem.at[1 - s], to_hbm=False)

            pltpu.make_async_copy(y_hbm.at[pl.ds(0, TOP_K * TT * ROW_TILE), :], yrun[s], rsem.at[s]).wait()

            def place(t, carry):
                for k in range(TOP_K):
                    p = pos[(s * TOP_K + k) * TT + t]
                    _token_tile(ybuf, k * TT + t)[...] = _tile_at_row(yrun[s], p)[...]
                return carry
            jax.lax.fori_loop(0, TT, place, 0, unroll=8)

            @pl.when(i + 2 < n)
            def _():
                for cp in tables(i + 2, s):
                    cp.start()

            tw = tw_ref[...]
            moe = tw[:, 0:1] * _load_token_tiles(ybuf, 0, TT)
            for k in range(1, TOP_K):
                moe = moe + tw[:, k:k + 1] * _load_token_tiles(ybuf, k * TT, TT)
            gt = gt_ref[...]
            if gate_copies > 1:
                gt = jnp.concatenate([gt] * gate_copies, axis=0)
            x = x1_ref[...] + gt * moe
            o_ref[...] = _rms(x, g_ref[...]) if final else x


def _combine(pstart, pk8, runs, y_sorted, x1, tw, gate, norm_final, *, gate_spec, gate_copies, final, name):
    n = x1.shape[0]
    row = lambda width: pl.BlockSpec((TT, width), lambda i, ps: (i, 0))
    tiles = pltpu.VMEM((TOP_K * TT * ROW_TILE, LANES), F32)
    grid_spec = pltpu.PrefetchScalarGridSpec(
        num_scalar_prefetch=1,
        grid=(n // TT,),
        in_specs=[
            pl.BlockSpec(memory_space=pl.ANY),
            pl.BlockSpec(memory_space=pl.ANY),
            pl.BlockSpec(memory_space=pl.ANY),
            row(D_MODEL), row(LANES), gate_spec,
            pl.BlockSpec((1, D_MODEL), lambda i, ps: (0, 0)),
        ],
        out_specs=row(D_MODEL),
        scratch_shapes=[
            tiles, tiles, tiles,
            pltpu.SMEM((2 * TOP_K * TT,), I32),
            pltpu.SMEM((2 * RUN_ROWS * LANES,), I32),
            pltpu.SemaphoreType.DMA((2,)),
            pltpu.SemaphoreType.DMA((2,)),
        ],
    )
    return pl.pallas_call(
        functools.partial(_combine_kernel, gate_copies=gate_copies, final=final),
        grid_spec=grid_spec,
        out_shape=jax.ShapeDtypeStruct((n, D_MODEL), F32),
        compiler_params=pltpu.CompilerParams(
            dimension_semantics=("arbitrary",), vmem_limit_bytes=VMEM_LIMIT),
        name=name,
    )(pstart, pk8, runs, y_sorted, x1, tw, gate, norm_final.reshape(1, D_MODEL))


def kernel(x_prompt, x_sample, state_conv, state_pool, c_prompt, c_sample, w_ada, b_ada, norm_mix, w_in, b_in, conv_w, conv_b, conv_ln_g, conv_ln_b, w_conv_out, w_pool, pool_scale, w_out, norm_ffn, w_router, b_router, w_up, b_up, w_down, b_down, norm_final):
    batch, seq, _ = x_prompt.shape
    dec_batch, dec_seq, _ = x_sample.shape
    n_p = batch * seq
    n_s = dec_batch * dec_seq
    n_blocks = dec_batch // SB
    assert dec_seq * SB == TT and seq % TT == 0 and dec_batch % SB == 0
    n_assign = (n_p + n_s) * TOP_K
    nb = n_assign // BM + N_EXPERTS

    vec = lambda a: a.reshape(DEPTH, 1, a.shape[-1])
    weights = (
        vec(norm_mix), w_in.astype(BF16), vec(b_in), conv_w, vec(conv_b), vec(conv_ln_g),
        vec(conv_ln_b), w_conv_out.astype(BF16), w_pool.astype(BF16), vec(pool_scale),
        w_out.astype(BF16), vec(norm_ffn),
        jnp.pad(w_router, ((0, 0), (0, 0), (0, LANES - N_EXPERTS))).astype(BF16),
        vec(jnp.pad(b_router, ((0, 0), (0, LANES - N_EXPERTS)))),
    )

    mod = _adaln(jnp.concatenate([c_prompt, c_sample], axis=0), w_ada, b_ada)
    mod_p = mod[:, :, :batch]
    mod_s = mod[:, :, batch:]

    def to_sample_order(a):
        lead = a.shape[:-3]
        a = a.reshape(lead + (n_blocks, SB) + a.shape[-2:])
        return jnp.swapaxes(a, -3, -2)

    def from_sample_order(a):
        a = jnp.swapaxes(a, -3, -2)
        return a.reshape(a.shape[:-4] + (dec_batch,) + a.shape[-2:])

    xp = x_prompt.reshape(n_p, D_MODEL)
    xs = to_sample_order(x_sample).reshape(n_s, D_MODEL)
    conv_state = to_sample_order(state_conv)
    pool_state = to_sample_order(state_pool)

    gate_p = pl.BlockSpec((None, 1, D_MODEL), lambda i, ps: (i // (seq // TT), 0, 0))
    gate_s = pl.BlockSpec((SB, D_MODEL), lambda i, ps: (i, 0))

    new_conv_p, new_pool_p, new_conv_s, new_pool_s = [], [], [], []
    for l in range(DEPTH):
        x1p, h2p, twp, pkp, runsp, cntp, ncp, npp = _prompt_mixer(l, xp, mod_p[l], weights, batch, seq)
        x1s, h2s, tws, pks, runss, cnt, ncs, nps = _sample_mixer(
            l, xs, mod_s[l], conv_state[l], pool_state[l], cntp, weights, dec_batch, dec_seq, PAST_LEN)
        new_conv_p.append(ncp)
        new_pool_p.append(npp)
        new_conv_s.append(ncs)
        new_pool_s.append(nps)

        counts = cnt[0, :N_EXPERTS].astype(I32)
        pstart, block_expert, n_valid, n_used, next_expert = _block_tables(counts, nb)
        x_sorted = _dispatch(pstart, jnp.concatenate([pkp, pks], axis=1),
                             jnp.concatenate([runsp, runss], axis=0), h2p, h2s, nb * BM)
        y_sorted = _experts(l, x_sorted, block_expert, n_valid, n_used, next_expert, w_up, b_up, w_down,
                            b_down)

        final = l == DEPTH - 1
        xp = _combine(pstart, pkp, runsp, y_sorted, x1p, twp, mod_p[l][5][:, None, :], norm_final,
                      gate_spec=gate_p, gate_copies=1, final=final, name=f"combine_prompt_l{l}")
        xs = _combine(pstart, pks, runss, y_sorted, x1s, tws, mod_s[l][5], norm_final,
                      gate_spec=gate_s, gate_copies=dec_seq, final=final, name=f"combine_sample_l{l}")

    y_prompt = xp.reshape(batch, seq, D_MODEL)
    y_sample = from_sample_order(xs.reshape(n_blocks, dec_seq, SB, D_MODEL))
    return (y_prompt, y_sample, jnp.stack(new_conv_p), jnp.stack(new_pool_p),
            from_sample_order(jnp.stack(new_conv_s)), from_sample_order(jnp.stack(new_pool_s)))
```

```python
import functools

import jax
import jax.numpy as jnp
from jax.experimental import pallas as pl
from jax.experimental.pallas import tpu as pltpu

F32 = jnp.float32
BF16 = jnp.bfloat16
I32 = jnp.int32

D_MODEL = 1024
DEPTH = 2
D_CONV = 512
CONV_WIDTH = 31
CONV_BUF = CONV_WIDTH - 1
D_POOL = 512
POOL_WINDOWS = (2, 4, 8, 16)
POOL_GROUP_IN = D_POOL // len(POOL_WINDOWS)
POOL_GROUP_OUT = D_MODEL // len(POOL_WINDOWS)
POOL_BUF = max(POOL_WINDOWS) - 1
IN_COLS = 2 * D_CONV + D_POOL + 2 * D_MODEL
N_EXPERTS = 32
TOP_K = 4
D_EXPERT = D_MODEL
SWIGLU_LIMIT = 7.0
SWIGLU_ALPHA = 1.702
N_MOD = 6
EPS = 1e-6
PAST_LEN = 16384

LANES = 128
SUBLANES = 8
VMEM_LIMIT = 52 * 1024 * 1024

TT = 256
NSEQ = 2
CONV_HALO = 32
POOL_HALO = 16
CONV_ROWS = 64
SB = 64
BM = 512
BM_PARTS = 4
ROW_TILE = D_MODEL // LANES
EXPERT_BITS = 5
CONV_SHIFT_ROWS = TT + CONV_HALO - SUBLANES
assert N_EXPERTS == 1 << EXPERT_BITS and ROW_TILE == SUBLANES


def _rms(x, g):
    return x * jax.lax.rsqrt(jnp.mean(x * x, axis=-1, keepdims=True) + EPS) * g


def _dot(a, b):
    return jnp.dot(a, b, preferred_element_type=F32)


def _store_token_tiles(ref, v):
    rows = v.shape[0]
    for s in range(ROW_TILE):
        ref[pl.ds(s, rows, stride=ROW_TILE), :] = v[:, s * LANES:(s + 1) * LANES]


def _load_token_tiles(ref, first_token, rows):
    return jnp.concatenate(
        [ref[pl.ds(first_token * ROW_TILE + s, rows, stride=ROW_TILE), :] for s in range(ROW_TILE)], axis=-1)


def _tile_at_row(ref, row):
    return ref.at[pl.ds(pl.multiple_of(row, ROW_TILE), ROW_TILE), :]


def _token_tile(ref, t):
    return _tile_at_row(ref, t * ROW_TILE)


def _adaln_kernel(c_ref, w_ref, b_ref, o_ref):
    c = c_ref[...]
    a = (c * jax.nn.sigmoid(c)).astype(BF16)
    o_ref[...] = _dot(a, w_ref[...].astype(BF16)) + b_ref[...]


def _adaln(c_all, w_ada, b_ada):
    n = c_all.shape[0]
    cols = N_MOD * D_MODEL
    return pl.pallas_call(
        _adaln_kernel,
        grid=(DEPTH, N_MOD),
        in_specs=[
            pl.BlockSpec((n, D_MODEL), lambda l, j: (0, 0)),
            pl.BlockSpec((None, D_MODEL, D_MODEL), lambda l, j: (l, 0, j)),
            pl.BlockSpec((None, 1, D_MODEL), lambda l, j: (l, 0, j)),
        ],
        out_specs=pl.BlockSpec((None, None, n, D_MODEL), lambda l, j: (l, j, 0, 0)),
        out_shape=jax.ShapeDtypeStruct((DEPTH, N_MOD, n, D_MODEL), F32),
        compiler_params=pltpu.CompilerParams(
            dimension_semantics=("arbitrary", "arbitrary"), vmem_limit_bytes=VMEM_LIMIT),
        name="adaln",
    )(c_all, w_ada, b_ada.reshape(DEPTH, 1, cols))


def _in_proj(x, sh1, sc1, nmix_ref, win_ref, bin_ref):
    h = (_rms(x, nmix_ref[...]) * (1.0 + sc1) + sh1).astype(BF16)
    c0, c1, c2 = 2 * D_CONV, 2 * D_CONV + D_POOL, IN_COLS
    zu = _dot(h, win_ref[:, 0:c0]) + bin_ref[:, 0:c0]
    u = zu[:, :D_CONV] * jax.nn.sigmoid(zu[:, D_CONV:])
    up = _dot(h, win_ref[:, c0:c1]) + bin_ref[:, c0:c1]
    zg = _dot(h, win_ref[:, c1:c2]) + bin_ref[:, c1:c2]
    return u, up, zg[:, :D_MODEL], zg[:, D_MODEL:]


def _conv_act(acc, cb_ref, lng_ref, lnb_ref):
    v = acc + cb_ref[...]
    mu = jnp.mean(v, axis=-1, keepdims=True)
    d = v - mu
    var = jnp.mean(d * d, axis=-1, keepdims=True)
    vn = d * jax.lax.rsqrt(var + EPS) * lng_ref[...] + lnb_ref[...]
    return vn * jax.nn.sigmoid(vn)


def _merge_and_route(x, v_bf, pooled, gc, gp, gt1, sh2, sc2, carry, w):
    rows = x.shape[0]
    y_conv = _dot(v_bf, w["wco"][...])
    y_pool = jnp.concatenate(
        [_dot(pooled[g].astype(BF16), w["wpool"][g]) for g in range(len(POOL_WINDOWS))], axis=-1)
    y_pool = y_pool * w["pscale"][...]
    m = jax.nn.sigmoid(gc) * y_conv + jax.nn.sigmoid(gp) * y_pool
    x1 = x + gt1 * _dot(m.astype(BF16), w["wout"][...])
    h2 = _rms(x1, w["nffn"][...]) * (1.0 + sc2) + sh2
    logits = _dot(h2.astype(BF16), w["wr"][...]) + w["br"][...]
    lane = jax.lax.broadcasted_iota(I32, (rows, LANES), 1)
    lane_f = lane.astype(F32)
    neg = jnp.full((rows, LANES), -jnp.inf, F32)
    l = jnp.where(lane < N_EXPERTS, logits, neg)
    vals, idxs, sels = [], [], []
    for _ in range(TOP_K):
        mx = jnp.max(l, axis=-1, keepdims=True)
        ix = jnp.min(jnp.where(l == mx, lane_f, float(LANES)), axis=-1, keepdims=True)
        sel = lane_f == ix
        l = jnp.where(sel, neg, l)
        vals.append(mx)
        idxs.append(ix)
        sels.append(sel)
    es = [jnp.exp(v - vals[0]) for v in vals]
    den = es[0] + es[1] + es[2] + es[3]

    onehot = jnp.where(sels[0] | sels[1] | sels[2] | sels[3], 1.0, 0.0)
    r_i = jax.lax.broadcasted_iota(I32, (rows, rows), 0)
    c_i = jax.lax.broadcasted_iota(I32, (rows, rows), 1)
    before = jnp.where(c_i < r_i, 1.0, 0.0).astype(BF16)
    earlier = _dot(before, onehot.astype(BF16))
    count = jnp.sum(onehot, axis=0, keepdims=True)
    new_carry = carry + count
    e_i = jax.lax.broadcasted_iota(I32, (LANES, LANES), 0)
    e_j = jax.lax.broadcasted_iota(I32, (LANES, LANES), 1)
    lower_expert = jnp.where(e_i < e_j, 1.0, 0.0).astype(BF16)
    first = _dot(jnp.broadcast_to(count, (SUBLANES, LANES)).astype(BF16), lower_expert)[0:1, :]

    tw = jnp.zeros((rows, LANES), F32)
    pk = jnp.zeros((rows, LANES), F32)
    for k in range(TOP_K):
        in_tile = jnp.sum(jnp.where(sels[k], earlier, 0.0), axis=-1, keepdims=True)
        prior = jnp.sum(jnp.where(sels[k], carry, 0.0), axis=-1, keepdims=True)
        start = jnp.sum(jnp.where(sels[k], first, 0.0), axis=-1, keepdims=True)
        tw = jnp.where(lane == k, es[k] / den, tw)
        pk = jnp.where(lane == k, (prior + in_tile) * float(N_EXPERTS) + idxs[k], pk)
        pk = jnp.where(lane == TOP_K + k, (start + in_tile) * float(ROW_TILE), pk)
    pk8 = jnp.transpose(pk)[0:SUBLANES, :].astype(I32)
    runs = jnp.concatenate(
        [carry, count, first, jnp.zeros((SUBLANES - 3, LANES), F32)], axis=0).astype(I32)
    return x1, h2, tw, pk8, runs, new_carry


_WEIGHT_NAMES = ("nmix", "win", "bin", "cw", "cb", "lng", "lnb", "wco", "wpool", "pscale", "wout",
                 "nffn", "wr", "br")


def _weight_specs(l, n_grid):
    def spec(*shape):
        zeros = (0,) * len(shape)
        if n_grid == 2:
            return pl.BlockSpec((None,) + shape, lambda b, t: (l,) + zeros)
        return pl.BlockSpec((None,) + shape, lambda i: (l,) + zeros)
    return [
        spec(1, D_MODEL),
        spec(D_MODEL, IN_COLS),
        spec(1, IN_COLS),
        spec(CONV_WIDTH, D_CONV),
        spec(1, D_CONV),
        spec(1, D_CONV),
        spec(1, D_CONV),
        spec(D_CONV, D_MODEL),
        spec(len(POOL_WINDOWS), POOL_GROUP_IN, POOL_GROUP_OUT),
        spec(1, D_MODEL),
        spec(D_MODEL, D_MODEL),
        spec(1, D_MODEL),
        spec(D_MODEL, LANES),
        spec(1, LANES),
    ]


def _prompt_mixer_kernel(x_ref, mod_ref, *refs):
    w = dict(zip(_WEIGHT_NAMES, refs[:len(_WEIGHT_NAMES)]))
    (x1_ref, h2_ref, tw_ref, pk_ref, runs_ref, cnt_ref, nconv_ref, npool_ref,
     uhist, ushift, phist, vbuf, carry) = refs[len(_WEIGHT_NAMES):]

    b = pl.program_id(0)
    t = pl.program_id(1)
    nt = pl.num_programs(1)

    @pl.when((b == 0) & (t == 0))
    def _():
        carry[...] = jnp.zeros((SUBLANES, LANES), F32)

    @pl.when(t == 0)
    def _():
        for q in range(NSEQ):
            uhist[q, 0:CONV_HALO, :] = jnp.zeros((CONV_HALO, D_CONV), F32)
            phist[q, 0:POOL_HALO, :] = jnp.zeros((POOL_HALO, D_POOL), F32)

    gates = []
    for q in range(NSEQ):
        u, up, gc, gp = _in_proj(x_ref[q], mod_ref[0, q], mod_ref[1, q], w["nmix"], w["win"], w["bin"])
        uhist[q, CONV_HALO:CONV_HALO + TT, :] = u
        phist[q, POOL_HALO:POOL_HALO + TT, :] = up
        gates.append((gc, gp))

    def conv_and_pool(q):
        for r in range(1, SUBLANES):
            ushift[q, r - 1] = uhist[q, r:r + CONV_SHIFT_ROWS, :]
        for c in range(TT // CONV_ROWS):
            acc = jnp.zeros((CONV_ROWS, D_CONV), F32)
            for k in range(CONV_WIDTH):
                qq, r = divmod(CONV_HALO - CONV_BUF + k, SUBLANES)
                start = qq * SUBLANES + c * CONV_ROWS
                if r == 0:
                    tap = uhist[q, start:start + CONV_ROWS, :]
                else:
                    tap = ushift[q, r - 1, start:start + CONV_ROWS, :]
                acc = acc + w["cw"][k:k + 1, :] * tap
            s = _conv_act(acc, w["cb"], w["lng"], w["lnb"])
            vbuf[q, c * CONV_ROWS:(c + 1) * CONV_ROWS, :] = s.astype(BF16)

        pos = t * TT + jax.lax.broadcasted_iota(I32, (TT, 1), 0)
        pooled = []
        for g, win in enumerate(POOL_WINDOWS):
            lo, hi = g * POOL_GROUP_IN, (g + 1) * POOL_GROUP_IN
            cur = phist[q, POOL_HALO:POOL_HALO + TT, lo:hi]
            ssum = cur
            for i in range(1, win):
                ssum = ssum + phist[q, POOL_HALO - i:POOL_HALO - i + TT, lo:hi]
            cnt = jnp.minimum(pos + 1, win).astype(F32)
            pooled.append(ssum / cnt - cur)
        return pooled

    cur_carry = carry[0:1, :]
    for q in range(NSEQ):
        pooled = conv_and_pool(q)
        gc, gp = gates[q]
        x1, h2, tw, pk8, runs, cur_carry = _merge_and_route(
            x_ref[q], vbuf[q], pooled, gc, gp, mod_ref[2, q], mod_ref[3, q], mod_ref[4, q], cur_carry, w)
        x1_ref[q] = x1
        _store_token_tiles(h2_ref.at[q], h2)
        tw_ref[q] = tw
        pk_ref[q] = pk8
        runs_ref[q] = runs

        uhist[q, 0:CONV_HALO, :] = uhist[q, TT:TT + CONV_HALO, :]
        phist[q, 0:POOL_HALO, :] = phist[q, TT:TT + POOL_HALO, :]

    carry[...] = jnp.broadcast_to(cur_carry, (SUBLANES, LANES))
    cnt_ref[...] = jnp.broadcast_to(cur_carry, (SUBLANES, LANES))

    @pl.when(t == nt - 1)
    def _():
        for q in range(NSEQ):
            nconv_ref[q] = uhist[q, CONV_HALO + TT - CONV_BUF:CONV_HALO + TT, :]
            npool_ref[q] = phist[q, POOL_HALO + TT - POOL_BUF:POOL_HALO + TT, :]


def _prompt_mixer(l, x, mod, weights, batch, seq):
    nt = seq // TT
    n_tok = batch * seq
    half = n_tok // NSEQ
    row = lambda rows, width: pl.BlockSpec((NSEQ, rows, width), lambda b, t: (0, b * nt + t, 0))
    in_specs = [row(TT, D_MODEL),
                pl.BlockSpec((N_MOD, NSEQ, None, 1, D_MODEL), lambda b, t: (0, 0, b, 0, 0))]
    in_specs += _weight_specs(l, 2)
    out_shape = (
        jax.ShapeDtypeStruct((NSEQ, half, D_MODEL), F32),
        jax.ShapeDtypeStruct((NSEQ, half * ROW_TILE, LANES), F32),
        jax.ShapeDtypeStruct((NSEQ, half, LANES), F32),
        jax.ShapeDtypeStruct((NSEQ, SUBLANES, half), I32),
        jax.ShapeDtypeStruct((NSEQ, half // TT * SUBLANES, LANES), I32),
        jax.ShapeDtypeStruct((SUBLANES, LANES), F32),
        jax.ShapeDtypeStruct((NSEQ, batch // NSEQ, CONV_BUF, D_CONV), F32),
        jax.ShapeDtypeStruct((NSEQ, batch // NSEQ, POOL_BUF, D_POOL), F32),
    )
    out_specs = (
        row(TT, D_MODEL), row(TT * ROW_TILE, LANES), row(TT, LANES),
        pl.BlockSpec((NSEQ, SUBLANES, TT), lambda b, t: (0, 0, b * nt + t)),
        row(SUBLANES, LANES),
        pl.BlockSpec((SUBLANES, LANES), lambda b, t: (0, 0)),
        pl.BlockSpec((NSEQ, None, CONV_BUF, D_CONV), lambda b, t: (0, b, 0, 0)),
        pl.BlockSpec((NSEQ, None, POOL_BUF, D_POOL), lambda b, t: (0, b, 0, 0)),
    )
    x1, h2, tw, pk8, runs, cnt, nconv, npool = pl.pallas_call(
        _prompt_mixer_kernel,
        grid=(batch // NSEQ, nt),
        in_specs=in_specs,
        out_specs=out_specs,
        out_shape=out_shape,
        scratch_shapes=[
            pltpu.VMEM((NSEQ, CONV_HALO + TT, D_CONV), F32),
            pltpu.VMEM((NSEQ, SUBLANES - 1, CONV_SHIFT_ROWS, D_CONV), F32),
            pltpu.VMEM((NSEQ, POOL_HALO + TT, D_POOL), F32),
            pltpu.VMEM((NSEQ, TT, D_CONV), BF16),
            pltpu.VMEM((SUBLANES, LANES), F32),
        ],
        compiler_params=pltpu.CompilerParams(
            dimension_semantics=("arbitrary", "arbitrary"), vmem_limit_bytes=VMEM_LIMIT),
        name=f"prompt_mixer_l{l}",
    )(x.reshape(NSEQ, half, D_MODEL), mod.reshape(N_MOD, NSEQ, batch // NSEQ, 1, D_MODEL), *weights)
    return (x1.reshape(n_tok, D_MODEL), h2.reshape(n_tok * ROW_TILE, LANES), tw.reshape(n_tok, LANES),
            jnp.transpose(pk8, (1, 0, 2)).reshape(SUBLANES, n_tok),
            runs.reshape(n_tok // TT * SUBLANES, LANES), cnt,
            nconv.reshape(batch, CONV_BUF, D_CONV), npool.reshape(batch, POOL_BUF, D_POOL))


def _sample_mixer_kernel(x_ref, mod_ref, cs_ref, ps_ref, cnt0_ref, *refs, dec_seq, pos0):
    w = dict(zip(_WEIGHT_NAMES, refs[:len(_WEIGHT_NAMES)]))
    (x1_ref, h2_ref, tw_ref, pk_ref, runs_ref, cnt_ref, nconv_ref, npool_ref,
     ufull, pfull, vbuf, carry) = refs[len(_WEIGHT_NAMES):]

    def per_row(v):
        return jnp.concatenate([v] * dec_seq, axis=0)

    @pl.when(pl.program_id(0) == 0)
    def _():
        carry[...] = cnt0_ref[...]

    x = x_ref[...]
    sh1, sc1, gt1 = per_row(mod_ref[0]), per_row(mod_ref[1]), per_row(mod_ref[2])
    sh2, sc2 = per_row(mod_ref[3]), per_row(mod_ref[4])

    u, up, gc, gp = _in_proj(x, sh1, sc1, w["nmix"], w["win"], w["bin"])
    ufull[0:CONV_BUF] = cs_ref[...]
    pfull[0:POOL_BUF] = ps_ref[...]
    for j in range(dec_seq):
        ufull[CONV_BUF + j] = u[j * SB:(j + 1) * SB, :]
        pfull[POOL_BUF + j] = up[j * SB:(j + 1) * SB, :]
    nconv_ref[...] = ufull[dec_seq:dec_seq + CONV_BUF]
    npool_ref[...] = pfull[dec_seq:dec_seq + POOL_BUF]

    for j in range(dec_seq):
        acc = jnp.zeros((SB, D_CONV), F32)
        for k in range(CONV_WIDTH):
            acc = acc + w["cw"][k:k + 1, :] * ufull[j + k]
        s = _conv_act(acc, w["cb"], w["lng"], w["lnb"])
        vbuf[j * SB:(j + 1) * SB, :] = s.astype(BF16)

    pooled = []
    for g, win in enumerate(POOL_WINDOWS):
        lo, hi = g * POOL_GROUP_IN, (g + 1) * POOL_GROUP_IN
        parts = []
        for j in range(dec_seq):
            cur = pfull[POOL_BUF + j, :, lo:hi]
            ssum = cur
            for i in range(1, win):
                ssum = ssum + pfull[POOL_BUF + j - i, :, lo:hi]
            cnt = float(min(pos0 + j + 1, win))
            parts.append(ssum / cnt - cur)
        pooled.append(jnp.concatenate(parts, axis=0))

    x1, h2, tw, pk8, runs, new_carry = _merge_and_route(
        x, vbuf[...], pooled, gc, gp, gt1, sh2, sc2, carry[0:1, :], w)
    x1_ref[...] = x1
    _store_token_tiles(h2_ref, h2)
    tw_ref[...] = tw
    pk_ref[...] = pk8
    runs_ref[...] = runs
    carry[...] = jnp.broadcast_to(new_carry, (SUBLANES, LANES))
    cnt_ref[...] = jnp.broadcast_to(new_carry, (SUBLANES, LANES))


def _sample_mixer(l, x, mod, conv_state, pool_state, cnt0, weights, dec_batch, dec_seq, pos0):
    rows = dec_seq * SB
    n_s = dec_batch * dec_seq
    row = lambda width: pl.BlockSpec((rows, width), lambda i: (i, 0))
    conv_spec = pl.BlockSpec((None, CONV_BUF, SB, D_CONV), lambda i: (i, 0, 0, 0))
    pool_spec = pl.BlockSpec((None, POOL_BUF, SB, D_POOL), lambda i: (i, 0, 0, 0))
    cnt_spec = pl.BlockSpec((SUBLANES, LANES), lambda i: (0, 0))
    in_specs = [row(D_MODEL), pl.BlockSpec((N_MOD, SB, D_MODEL), lambda i: (0, i, 0)),
                conv_spec, pool_spec, cnt_spec]
    in_specs += _weight_specs(l, 1)
    out_shape = (
        jax.ShapeDtypeStruct((n_s, D_MODEL), F32),
        jax.ShapeDtypeStruct((n_s * ROW_TILE, LANES), F32),
        jax.ShapeDtypeStruct((n_s, LANES), F32),
        jax.ShapeDtypeStruct((SUBLANES, n_s), I32),
        jax.ShapeDtypeStruct((n_s // rows * SUBLANES, LANES), I32),
        jax.ShapeDtypeStruct((SUBLANES, LANES), F32),
        jax.ShapeDtypeStruct(conv_state.shape, F32),
        jax.ShapeDtypeStruct(pool_state.shape, F32),
    )
    out_specs = (row(D_MODEL), pl.BlockSpec((rows * ROW_TILE, LANES), lambda i: (i, 0)), row(LANES),
                 pl.BlockSpec((SUBLANES, rows), lambda i: (0, i)),
                 pl.BlockSpec((SUBLANES, LANES), lambda i: (i, 0)), cnt_spec, conv_spec, pool_spec)
    return pl.pallas_call(
        functools.partial(_sample_mixer_kernel, dec_seq=dec_seq, pos0=pos0),
        grid=(dec_batch // SB,),
        in_specs=in_specs,
        out_specs=out_specs,
        out_shape=out_shape,
        scratch_shapes=[
            pltpu.VMEM((CONV_BUF + dec_seq, SB, D_CONV), F32),
            pltpu.VMEM((POOL_BUF + dec_seq, SB, D_POOL), F32),
            pltpu.VMEM((rows, D_CONV), BF16),
            pltpu.VMEM((SUBLANES, LANES), F32),
        ],
        compiler_params=pltpu.CompilerParams(
            dimension_semantics=("arbitrary",), vmem_limit_bytes=VMEM_LIMIT),
        name=f"sample_mixer_l{l}",
    )(x, mod, conv_state, pool_state, cnt0, *weights)


RUN_ROWS = 3


def _table_copies(pk_hbm, runs_hbm, tile, pos, runs, slot, sem):
    copies = [
        pltpu.make_async_copy(pk_hbm.at[TOP_K + k, pl.ds(tile * TT, TT)],
                              pos.at[pl.ds((slot * TOP_K + k) * TT, TT)], sem)
        for k in range(TOP_K)]
    copies += [
        pltpu.make_async_copy(runs_hbm.at[tile * SUBLANES + r],
                              runs.at[pl.ds((slot * RUN_ROWS + r) * LANES, LANES)], sem)
        for r in range(RUN_ROWS)]
    return copies


def _run_copies(pstart_ref, runs, slot, sorted_hbm, run_ref, sem, to_hbm):
    def body(e, carry):
        count = runs[(slot * RUN_ROWS + 1) * LANES + e]

        @pl.when(count > 0)
        def _():
            glob = pl.multiple_of((pstart_ref[e] + runs[slot * RUN_ROWS * LANES + e]) * ROW_TILE, ROW_TILE)
            loc = pl.multiple_of(runs[(slot * RUN_ROWS + 2) * LANES + e] * ROW_TILE, ROW_TILE)
            hbm = sorted_hbm.at[pl.ds(glob, count * ROW_TILE), :]
            vmem = run_ref.at[pl.ds(loc, count * ROW_TILE), :]
            if to_hbm:
                pltpu.make_async_copy(vmem, hbm, sem).start()
            else:
                pltpu.make_async_copy(hbm, vmem, sem).start()
        return carry
    jax.lax.fori_loop(0, N_EXPERTS, body, 0, unroll=2)


def _dispatch_kernel(pstart_ref, pk_hbm, runs_hbm, h2p_ref, h2s_ref, xs_hbm, xrun0, xrun1, pos, runs,
                     isem, rsem, *, n_prompt_tiles):
    c = pl.program_id(0)
    n = pl.num_programs(0)
    xrun = (xrun0, xrun1)

    def tables(tile, slot):
        return _table_copies(pk_hbm, runs_hbm, tile, pos, runs, slot, isem.at[slot])

    def wait_runs(slot):
        pltpu.make_async_copy(xrun[slot], xs_hbm.at[pl.ds(0, TOP_K * TT * ROW_TILE), :], rsem.at[slot]).wait()

    @pl.when(c == 0)
    def _():
        for cp in tables(0, 0):
            cp.start()

    for s in range(2):
        @pl.when(c % 2 == s)
        def _():
            for cp in tables(c, s):
                cp.wait()

            @pl.when(c + 1 < n)
            def _():
                for cp in tables(c + 1, 1 - s):
                    cp.start()

            @pl.when(c >= 2)
            def _():
                wait_runs(s)

            def place_from(src_ref):
                def place(t, carry):
                    for k in range(TOP_K):
                        p = pos[(s * TOP_K + k) * TT + t]
                        _tile_at_row(xrun[s], p)[...] = _token_tile(src_ref, t)[...]
                    return carry
                jax.lax.fori_loop(0, TT, place, 0, unroll=8)

            @pl.when(c < n_prompt_tiles)
            def _():
                place_from(h2p_ref)

            @pl.when(c >= n_prompt_tiles)
            def _():
                place_from(h2s_ref)

            _run_copies(pstart_ref, runs, s, xs_hbm, xrun[s], rsem.at[s], to_hbm=True)

            @pl.when(c == n - 1)
            def _():
                @pl.when(c >= 1)
                def _():
                    wait_runs(1 - s)
                wait_runs(s)


def _dispatch(pstart, pk8, runs, h2p, h2s, n_slots):
    np_tiles = h2p.shape[0] // (TT * ROW_TILE)
    ns_tiles = h2s.shape[0] // (TT * ROW_TILE)
    tiles = pltpu.VMEM((TOP_K * TT * ROW_TILE, LANES), F32)
    grid_spec = pltpu.PrefetchScalarGridSpec(
        num_scalar_prefetch=1,
        grid=(np_tiles + ns_tiles,),
        in_specs=[
            pl.BlockSpec(memory_space=pl.ANY),
            pl.BlockSpec(memory_space=pl.ANY),
            pl.BlockSpec((TT * ROW_TILE, LANES), lambda c, ps: (jnp.minimum(c, np_tiles - 1), 0)),
            pl.BlockSpec((TT * ROW_TILE, LANES), lambda c, ps: (jnp.maximum(c - np_tiles, 0), 0)),
        ],
        out_specs=pl.BlockSpec(memory_space=pl.ANY),
        scratch_shapes=[
            tiles, tiles,
            pltpu.SMEM((2 * TOP_K * TT,), I32),
            pltpu.SMEM((2 * RUN_ROWS * LANES,), I32),
            pltpu.SemaphoreType.DMA((2,)),
            pltpu.SemaphoreType.DMA((2,)),
        ],
    )
    return pl.pallas_call(
        functools.partial(_dispatch_kernel, n_prompt_tiles=np_tiles),
        grid_spec=grid_spec,
        out_shape=jax.ShapeDtypeStruct((n_slots * ROW_TILE, LANES), F32),
        compiler_params=pltpu.CompilerParams(
            dimension_semantics=("arbitrary",), vmem_limit_bytes=VMEM_LIMIT),
        name="dispatch",
    )(pstart, pk8, runs, h2p, h2s)


def _experts_kernel(be_ref, nv_ref, nu_ref, nx_ref, x_ref, wup_hbm, bup_ref, wdn_hbm, bdn_ref, o_ref,
                    wup_f32, wdn_f32, wup_bf, wdn_bf, wsem, *, layer):
    b = pl.program_id(0)

    def weight_copies(e):
        return (pltpu.make_async_copy(wup_hbm.at[layer, e], wup_f32, wsem.at[0]),
                pltpu.make_async_copy(wdn_hbm.at[layer, e], wdn_f32, wsem.at[1]))

    @pl.when(b < nu_ref[0])
    def _():
        e = be_ref[b]
        prev = be_ref[jnp.maximum(b - 1, 0)]

        @pl.when((b == 0) | (e != prev))
        def _():
            @pl.when(b == 0)
            def _():
                for cp in weight_copies(e):
                    cp.start()
            for cp in weight_copies(e):
                cp.wait()
            for r in range(0, D_MODEL, LANES):
                wup_bf[r:r + LANES, :] = wup_f32[r:r + LANES, :].astype(BF16)
                wdn_bf[r:r + LANES, :] = wdn_f32[r:r + LANES, :].astype(BF16)
            nxt = nx_ref[e]

            @pl.when(nxt >= 0)
            def _():
                for cp in weight_copies(nxt):
                    cp.start(priority=1)

        n_valid = nv_ref[b]

        def ffn(rows):
            rid = jax.lax.broadcasted_iota(I32, (rows, D_MODEL), 0)
            x = jnp.where(rid < n_valid, _load_token_tiles(x_ref, 0, rows), 0.0)
            a = _dot(x.astype(BF16), wup_bf[...]) + bup_ref[...]
            a_glu = jnp.minimum(a[:, :D_EXPERT], SWIGLU_LIMIT)
            a_lin = jnp.clip(a[:, D_EXPERT:], -SWIGLU_LIMIT, SWIGLU_LIMIT)
            o = a_glu * jax.nn.sigmoid(SWIGLU_ALPHA * a_glu) * (a_lin + 1.0)
            _store_token_tiles(o_ref, _dot(o.astype(BF16), wdn_bf[...]) + bdn_ref[...])
            if rows < BM:
                o_ref[rows * ROW_TILE:, :] = jnp.zeros(((BM - rows) * ROW_TILE, LANES), F32)

        for part in range(1, BM_PARTS + 1):
            rows = part * (BM // BM_PARTS)

            @pl.when((n_valid > rows - BM // BM_PARTS) & (n_valid <= rows))
            def _():
                ffn(rows)

    @pl.when(b >= nu_ref[0])
    def _():
        o_ref[...] = jnp.zeros((BM * ROW_TILE, LANES), F32)


def _experts(l, x_sorted, block_expert, n_valid, n_used, next_expert, w_up, b_up, w_down, b_down):
    n_slots = x_sorted.shape[0] // ROW_TILE
    nb = n_slots // BM

    def used_map(b, be, nv, nu, nx):
        return (jnp.minimum(b, nu[0] - 1), 0)

    bmap = lambda b, be, nv, nu, nx: (l, be[b], 0, 0)
    grid_spec = pltpu.PrefetchScalarGridSpec(
        num_scalar_prefetch=4,
        grid=(nb,),
        in_specs=[
            pl.BlockSpec((BM * ROW_TILE, LANES), used_map),
            pl.BlockSpec(memory_space=pl.ANY),
            pl.BlockSpec((None, None, 1, 2 * D_EXPERT), bmap),
            pl.BlockSpec(memory_space=pl.ANY),
            pl.BlockSpec((None, None, 1, D_MODEL), bmap),
        ],
        out_specs=pl.BlockSpec((BM * ROW_TILE, LANES), lambda b, be, nv, nu, nx: (b, 0)),
        scratch_shapes=[
            pltpu.VMEM((D_MODEL, 2 * D_EXPERT), F32),
            pltpu.VMEM((D_EXPERT, D_MODEL), F32),
            pltpu.VMEM((D_MODEL, 2 * D_EXPERT), BF16),
            pltpu.VMEM((D_EXPERT, D_MODEL), BF16),
            pltpu.SemaphoreType.DMA((2,)),
        ],
    )
    return pl.pallas_call(
        functools.partial(_experts_kernel, layer=l),
        grid_spec=grid_spec,
        out_shape=jax.ShapeDtypeStruct((n_slots * ROW_TILE, LANES), F32),
        compiler_params=pltpu.CompilerParams(
            dimension_semantics=("arbitrary",), vmem_limit_bytes=VMEM_LIMIT),
        name=f"experts_l{l}",
    )(block_expert, n_valid, n_used, next_expert, x_sorted, w_up,
      b_up.reshape(DEPTH, N_EXPERTS, 1, 2 * D_EXPERT), w_down,
      b_down.reshape(DEPTH, N_EXPERTS, 1, D_MODEL))


def _tables_kernel(cnt_ref, pstart_ref, be_ref, nv_ref, nu_ref, nx_ref, *, nb):
    def clear(b, c):
        be_ref[b] = N_EXPERTS - 1
        nv_ref[b] = 0
        return c
    jax.lax.fori_loop(0, nb, clear, 0)

    def per_expert(e, carry):
        blk, last = carry
        cnt = cnt_ref[e]
        n = (cnt + (BM - 1)) // BM
        pstart_ref[e] = blk * BM
        nx_ref[e] = -1

        def per_block(j, c):
            be_ref[blk + j] = e
            nv_ref[blk + j] = jnp.minimum(cnt - j * BM, BM)
            return c
        jax.lax.fori_loop(0, n, per_block, 0)

        @pl.when((n > 0) & (last >= 0))
        def _():
            nx_ref[last] = e
        return blk + n, jnp.where(n > 0, e, last)

    blk, _ = jax.lax.fori_loop(0, N_EXPERTS, per_expert, (jnp.int32(0), jnp.int32(-1)))
    nu_ref[0] = blk


def _block_tables(counts, nb):
    smem = pl.BlockSpec(memory_space=pltpu.SMEM)
    return pl.pallas_call(
        functools.partial(_tables_kernel, nb=nb),
        in_specs=[smem],
        out_specs=(smem, smem, smem, smem, smem),
        out_shape=(
            jax.ShapeDtypeStruct((N_EXPERTS,), I32),
            jax.ShapeDtypeStruct((nb,), I32),
            jax.ShapeDtypeStruct((nb,), I32),
            jax.ShapeDtypeStruct((1,), I32),
            jax.ShapeDtypeStruct((N_EXPERTS,), I32),
        ),
        name="block_tables",
    )(counts)


def _combine_kernel(pstart_ref, pk_hbm, runs_hbm, y_hbm, x1_ref, tw_ref, gt_ref, g_ref, o_ref,
                    yrun0, yrun1, ybuf, pos, runs, isem, rsem, *, gate_copies, final):
    i = pl.program_id(0)
    n = pl.num_programs(0)
    yrun = (yrun0, yrun1)

    def tables(tile, slot):
        return _table_copies(pk_hbm, runs_hbm, tile, pos, runs, slot, isem.at[slot])

    @pl.when(i == 0)
    def _():
        for cp in tables(0, 0):
            cp.start()
        for cp in tables(0, 0):
            cp.wait()
        _run_copies(pstart_ref, runs, 0, y_hbm, yrun[0], rsem.at[0], to_hbm=False)

        @pl.when(n > 1)
        def _():
            for cp in tables(1, 1):
                cp.start()

    for s in range(2):
        @pl.when(i % 2 == s)
        def _():
            @pl.when(i + 1 < n)
            def _():
                for cp in tables(i + 1, 1 - s):
                    cp.wait()
                _run_copies(pstart_ref, runs, 1 - s, y_hbm, yrun[1 - s], rsem.at[1 - s], to_hbm=False)

            pltpu.make_async_copy(y_hbm.at[pl.ds(0, TOP_K * TT * ROW_TILE), :], yrun[s], rsem.at[s]).wait()

            def place(t, carry):
                for k in range(TOP_K):
                    p = pos[(s * TOP_K + k) * TT + t]
                    _token_tile(ybuf, k * TT + t)[...] = _tile_at_row(yrun[s], p)[...]
                return carry
            jax.lax.fori_loop(0, TT, place, 0, unroll=8)

            @pl.when(i + 2 < n)
            def _():
                for cp in tables(i + 2, s):
                    cp.start()

            tw = tw_ref[...]
            moe = tw[:, 0:1] * _load_token_tiles(ybuf, 0, TT)
            for k in range(1, TOP_K):
                moe = moe + tw[:, k:k + 1] * _load_token_tiles(ybuf, k * TT, TT)
            gt = gt_ref[...]
            if gate_copies > 1:
                gt = jnp.concatenate([gt] * gate_copies, axis=0)
            x = x1_ref[...] + gt * moe
            o_ref[...] = _rms(x, g_ref[...]) if final else x


def _combine(pstart, pk8, runs, y_sorted, x1, tw, gate, norm_final, *, gate_spec, gate_copies, final, name):
    n = x1.shape[0]
    row = lambda width: pl.BlockSpec((TT, width), lambda i, ps: (i, 0))
    tiles = pltpu.VMEM((TOP_K * TT * ROW_TILE, LANES), F32)
    grid_spec = pltpu.PrefetchScalarGridSpec(
        num_scalar_prefetch=1,
        grid=(n // TT,),
        in_specs=[
            pl.BlockSpec(memory_space=pl.ANY),
            pl.BlockSpec(memory_space=pl.ANY),
            pl.BlockSpec(memory_space=pl.ANY),
            row(D_MODEL), row(LANES), gate_spec,
            pl.BlockSpec((1, D_MODEL), lambda i, ps: (0, 0)),
        ],
        out_specs=row(D_MODEL),
        scratch_shapes=[
            tiles, tiles, tiles,
            pltpu.SMEM((2 * TOP_K * TT,), I32),
            pltpu.SMEM((2 * RUN_ROWS * LANES,), I32),
            pltpu.SemaphoreType.DMA((2,)),
            pltpu.SemaphoreType.DMA((2,)),
        ],
    )
    return pl.pallas_call(
        functools.partial(_combine_kernel, gate_copies=gate_copies, final=final),
        grid_spec=grid_spec,
        out_shape=jax.ShapeDtypeStruct((n, D_MODEL), F32),
        compiler_params=pltpu.CompilerParams(
            dimension_semantics=("arbitrary",), vmem_limit_bytes=VMEM_LIMIT),
        name=name,
    )(pstart, pk8, runs, y_sorted, x1, tw, gate, norm_final.reshape(1, D_MODEL))


def kernel(x_prompt, x_sample, state_conv, state_pool, c_prompt, c_sample, w_ada, b_ada, norm_mix, w_in, b_in, conv_w, conv_b, conv_ln_g, conv_ln_b, w_conv_out, w_pool, pool_scale, w_out, norm_ffn, w_router, b_router, w_up, b_up, w_down, b_down, norm_final):
    batch, seq, _ = x_prompt.shape
    dec_batch, dec_seq, _ = x_sample.shape
    n_p = batch * seq
    n_s = dec_batch * dec_seq
    n_blocks = dec_batch // SB
    assert dec_seq * SB == TT and seq % TT == 0 and dec_batch % SB == 0
    n_assign = (n_p + n_s) * TOP_K
    nb = n_assign // BM + N_EXPERTS

    vec = lambda a: a.reshape(DEPTH, 1, a.shape[-1])
    weights = (
        vec(norm_mix), w_in.astype(BF16), vec(b_in), conv_w, vec(conv_b), vec(conv_ln_g),
        vec(conv_ln_b), w_conv_out.astype(BF16), w_pool.astype(BF16), vec(pool_scale),
        w_out.astype(BF16), vec(norm_ffn),
        jnp.pad(w_router, ((0, 0), (0, 0), (0, LANES - N_EXPERTS))).astype(BF16),
        vec(jnp.pad(b_router, ((0, 0), (0, LANES - N_EXPERTS)))),
    )

    mod = _adaln(jnp.concatenate([c_prompt, c_sample], axis=0), w_ada, b_ada)
    mod_p = mod[:, :, :batch]
    mod_s = mod[:, :, batch:]

    def to_sample_order(a):
        lead = a.shape[:-3]
        a = a.reshape(lead + (n_blocks, SB) + a.shape[-2:])
        return jnp.swapaxes(a, -3, -2)

    def from_sample_order(a):
        a = jnp.swapaxes(a, -3, -2)
        return a.reshape(a.shape[:-4] + (dec_batch,) + a.shape[-2:])

    xp = x_prompt.reshape(n_p, D_MODEL)
    xs = to_sample_order(x_sample).reshape(n_s, D_MODEL)
    conv_state = to_sample_order(state_conv)
    pool_state = to_sample_order(state_pool)

    gate_p = pl.BlockSpec((None, 1, D_MODEL), lambda i, ps: (i // (seq // TT), 0, 0))
    gate_s = pl.BlockSpec((SB, D_MODEL), lambda i, ps: (i, 0))

    new_conv_p, new_pool_p, new_conv_s, new_pool_s = [], [], [], []
    for l in range(DEPTH):
        x1p, h2p, twp, pkp, runsp, cntp, ncp, npp = _prompt_mixer(l, xp, mod_p[l], weights, batch, seq)
        x1s, h2s, tws, pks, runss, cnt, ncs, nps = _sample_mixer(
            l, xs, mod_s[l], conv_state[l], pool_state[l], cntp, weights, dec_batch, dec_seq, PAST_LEN)
        new_conv_p.append(ncp)
        new_pool_p.append(npp)
        new_conv_s.append(ncs)
        new_pool_s.append(nps)

        counts = cnt[0, :N_EXPERTS].astype(I32)
        pstart, block_expert, n_valid, n_used, next_expert = _block_tables(counts, nb)
        x_sorted = _dispatch(pstart, jnp.concatenate([pkp, pks], axis=1),
                             jnp.concatenate([runsp, runss], axis=0), h2p, h2s, nb * BM)
        y_sorted = _experts(l, x_sorted, block_expert, n_valid, n_used, next_expert, w_up, b_up, w_down,
                            b_down)

        final = l == DEPTH - 1
        xp = _combine(pstart, pkp, runsp, y_sorted, x1p, twp, mod_p[l][5][:, None, :], norm_final,
                      gate_spec=gate_p, gate_copies=1, final=final, name=f"combine_prompt_l{l}")
        xs = _combine(pstart, pks, runss, y_sorted, x1s, tws, mod_s[l][5], norm_final,
                      gate_spec=gate_s, gate_copies=dec_seq, final=final, name=f"combine_sample_l{l}")

    y_prompt = xp.reshape(batch, seq, D_MODEL)
    y_sample = from_sample_order(xs.reshape(n_blocks, dec_seq, SB, D_MODEL))
    return (y_prompt, y_sample, jnp.stack(new_conv_p), jnp.stack(new_pool_p),
            from_sample_order(jnp.stack(new_conv_s)), from_sample_order(jnp.stack(new_pool_s)))
```

```python
import functools

import jax
import jax.numpy as jnp
from jax.experimental import pallas as pl
from jax.experimental.pallas import tpu as pltpu

F32 = jnp.float32
BF16 = jnp.bfloat16
I32 = jnp.int32

D_MODEL = 1024
DEPTH = 2
D_CONV = 512
CONV_WIDTH = 31
CONV_BUF = CONV_WIDTH - 1
D_POOL = 512
POOL_WINDOWS = (2, 4, 8, 16)
POOL_GROUP_IN = D_POOL // len(POOL_WINDOWS)
POOL_GROUP_OUT = D_MODEL // len(POOL_WINDOWS)
POOL_BUF = max(POOL_WINDOWS) - 1
IN_COLS = 2 * D_CONV + D_POOL + 2 * D_MODEL
N_EXPERTS = 32
TOP_K = 4
D_EXPERT = D_MODEL
SWIGLU_LIMIT = 7.0
SWIGLU_ALPHA = 1.702
N_MOD = 6
EPS = 1e-6
PAST_LEN = 16384

LANES = 128
SUBLANES = 8
VMEM_LIMIT = 52 * 1024 * 1024

TT = 256
NSEQ = 2
CONV_HALO = 32
POOL_HALO = 16
CONV_ROWS = 64
SB = 64
BM = 512
BM_PARTS = 4
ROW_TILE = D_MODEL // LANES
EXPERT_BITS = 5
CONV_SHIFT_ROWS = TT + CONV_HALO - SUBLANES
assert N_EXPERTS == 1 << EXPERT_BITS and ROW_TILE == SUBLANES


def _rms(x, g):
    return x * jax.lax.rsqrt(jnp.mean(x * x, axis=-1, keepdims=True) + EPS) * g


def _dot(a, b):
    return jnp.dot(a, b, preferred_element_type=F32)


def _store_token_tiles(ref, v):
    rows = v.shape[0]
    for s in range(ROW_TILE):
        ref[pl.ds(s, rows, stride=ROW_TILE), :] = v[:, s * LANES:(s + 1) * LANES]


def _load_token_tiles(ref, first_token, rows):
    return jnp.concatenate(
        [ref[pl.ds(first_token * ROW_TILE + s, rows, stride=ROW_TILE), :] for s in range(ROW_TILE)], axis=-1)


def _tile_at_row(ref, row):
    return ref.at[pl.ds(pl.multiple_of(row, ROW_TILE), ROW_TILE), :]


def _token_tile(ref, t):
    return _tile_at_row(ref, t * ROW_TILE)


def _adaln_kernel(c_ref, w_ref, b_ref, o_ref):
    c = c_ref[...]
    a = (c * jax.nn.sigmoid(c)).astype(BF16)
    o_ref[...] = _dot(a, w_ref[...].astype(BF16)) + b_ref[...]


def _adaln(c_all, w_ada, b_ada):
    n = c_all.shape[0]
    cols = N_MOD * D_MODEL
    return pl.pallas_call(
        _adaln_kernel,
        grid=(DEPTH, N_MOD),
        in_specs=[
            pl.BlockSpec((n, D_MODEL), lambda l, j: (0, 0)),
            pl.BlockSpec((None, D_MODEL, D_MODEL), lambda l, j: (l, 0, j)),
            pl.BlockSpec((None, 1, D_MODEL), lambda l, j: (l, 0, j)),
        ],
        out_specs=pl.BlockSpec((None, None, n, D_MODEL), lambda l, j: (l, j, 0, 0)),
        out_shape=jax.ShapeDtypeStruct((DEPTH, N_MOD, n, D_MODEL), F32),
        compiler_params=pltpu.CompilerParams(
            dimension_semantics=("arbitrary", "arbitrary"), vmem_limit_bytes=VMEM_LIMIT),
        name="adaln",
    )(c_all, w_ada, b_ada.reshape(DEPTH, 1, cols))


def _in_proj(x, sh1, sc1, nmix_ref, win_ref, bin_ref):
    h = (_rms(x, nmix_ref[...]) * (1.0 + sc1) + sh1).astype(BF16)
    c0, c1, c2 = 2 * D_CONV, 2 * D_CONV + D_POOL, IN_COLS
    zu = _dot(h, win_ref[:, 0:c0]) + bin_ref[:, 0:c0]
    u = zu[:, :D_CONV] * jax.nn.sigmoid(zu[:, D_CONV:])
    up = _dot(h, win_ref[:, c0:c1]) + bin_ref[:, c0:c1]
    zg = _dot(h, win_ref[:, c1:c2]) + bin_ref[:, c1:c2]
    return u, up, zg[:, :D_MODEL], zg[:, D_MODEL:]


def _conv_act(acc, cb_ref, lng_ref, lnb_ref):
    v = acc + cb_ref[...]
    mu = jnp.mean(v, axis=-1, keepdims=True)
    d = v - mu
    var = jnp.mean(d * d, axis=-1, keepdims=True)
    vn = d * jax.lax.rsqrt(var + EPS) * lng_ref[...] + lnb_ref[...]
    return vn * jax.nn.sigmoid(vn)


def _merge_and_route(x, v_bf, pooled, gc, gp, gt1, sh2, sc2, carry, w):
    rows = x.shape[0]
    y_conv = _dot(v_bf, w["wco"][...])
    y_pool = jnp.concatenate(
        [_dot(pooled[g].astype(BF16), w["wpool"][g]) for g in range(len(POOL_WINDOWS))], axis=-1)
    y_pool = y_pool * w["pscale"][...]
    m = jax.nn.sigmoid(gc) * y_conv + jax.nn.sigmoid(gp) * y_pool
    x1 = x + gt1 * _dot(m.astype(BF16), w["wout"][...])
    h2 = _rms(x1, w["nffn"][...]) * (1.0 + sc2) + sh2
    logits = _dot(h2.astype(BF16), w["wr"][...]) + w["br"][...]
    lane = jax.lax.broadcasted_iota(I32, (rows, LANES), 1)
    lane_f = lane.astype(F32)
    neg = jnp.full((rows, LANES), -jnp.inf, F32)
    l = jnp.where(lane < N_EXPERTS, logits, neg)
    vals, idxs, sels = [], [], []
    for _ in range(TOP_K):
        mx = jnp.max(l, axis=-1, keepdims=True)
        ix = jnp.min(jnp.where(l == mx, lane_f, float(LANES)), axis=-1, keepdims=True)
        sel = lane_f == ix
        l = jnp.where(sel, neg, l)
        vals.append(mx)
        idxs.append(ix)
        sels.append(sel)
    es = [jnp.exp(v - vals[0]) for v in vals]
    den = es[0] + es[1] + es[2] + es[3]

    onehot = jnp.where(sels[0] | sels[1] | sels[2] | sels[3], 1.0, 0.0)
    earlier = _dot(w["tri"][...], onehot.astype(BF16))
    count = jnp.sum(onehot, axis=0, keepdims=True)
    new_carry = carry + count
    first = _dot(jnp.broadcast_to(count, (SUBLANES, LANES)).astype(BF16), w["etri"][...])[0:1, :]

    tw = jnp.zeros((rows, LANES), F32)
    pk = jnp.zeros((rows, LANES), F32)
    for k in range(TOP_K):
        in_tile = jnp.sum(jnp.where(sels[k], earlier, 0.0), axis=-1, keepdims=True)
        prior = jnp.sum(jnp.where(sels[k], carry, 0.0), axis=-1, keepdims=True)
        start = jnp.sum(jnp.where(sels[k], first, 0.0), axis=-1, keepdims=True)
        tw = jnp.where(lane == k, es[k] / den, tw)
        pk = jnp.where(lane == k, (prior + in_tile) * float(N_EXPERTS) + idxs[k], pk)
        pk = jnp.where(lane == TOP_K + k, (start + in_tile) * float(ROW_TILE), pk)
    pk8 = jnp.transpose(pk)[0:SUBLANES, :].astype(I32)
    runs = jnp.concatenate(
        [carry, count, first, jnp.zeros((SUBLANES - 3, LANES), F32)], axis=0).astype(I32)
    return x1, h2, tw, pk8, runs, new_carry


_WEIGHT_NAMES = ("nmix", "win", "bin", "cw", "cb", "lng", "lnb", "wco", "wpool", "pscale", "wout",
                 "nffn", "wr", "br", "tri", "etri")


def _weight_specs(l, n_grid):
    def spec(*shape):
        zeros = (0,) * len(shape)
        if n_grid == 2:
            return pl.BlockSpec((None,) + shape, lambda b, t: (l,) + zeros)
        return pl.BlockSpec((None,) + shape, lambda i: (l,) + zeros)
    return [
        spec(1, D_MODEL),
        spec(D_MODEL, IN_COLS),
        spec(1, IN_COLS),
        spec(CONV_WIDTH, D_CONV),
        spec(1, D_CONV),
        spec(1, D_CONV),
        spec(1, D_CONV),
        spec(D_CONV, D_MODEL),
        spec(len(POOL_WINDOWS), POOL_GROUP_IN, POOL_GROUP_OUT),
        spec(1, D_MODEL),
        spec(D_MODEL, D_MODEL),
        spec(1, D_MODEL),
        spec(D_MODEL, LANES),
        spec(1, LANES),
        spec(TT, TT),
        spec(LANES, LANES),
    ]


def _prompt_mixer_kernel(x_ref, mod_ref, *refs):
    w = dict(zip(_WEIGHT_NAMES, refs[:len(_WEIGHT_NAMES)]))
    (x1_ref, h2_ref, tw_ref, pk_ref, runs_ref, cnt_ref, nconv_ref, npool_ref,
     uhist, ushift, phist, vbuf, carry) = refs[len(_WEIGHT_NAMES):]

    b = pl.program_id(0)
    t = pl.program_id(1)
    nt = pl.num_programs(1)

    @pl.when((b == 0) & (t == 0))
    def _():
        carry[...] = jnp.zeros((SUBLANES, LANES), F32)

    @pl.when(t == 0)
    def _():
        for q in range(NSEQ):
            uhist[q, 0:CONV_HALO, :] = jnp.zeros((CONV_HALO, D_CONV), F32)
            phist[q, 0:POOL_HALO, :] = jnp.zeros((POOL_HALO, D_POOL), F32)

    gates = []
    for q in range(NSEQ):
        u, up, gc, gp = _in_proj(x_ref[q], mod_ref[0, q], mod_ref[1, q], w["nmix"], w["win"], w["bin"])
        uhist[q, CONV_HALO:CONV_HALO + TT, :] = u
        phist[q, POOL_HALO:POOL_HALO + TT, :] = up
        gates.append((gc, gp))

    def conv_and_pool(q):
        for r in range(1, SUBLANES):
            ushift[q, r - 1] = uhist[q, r:r + CONV_SHIFT_ROWS, :]
        for c in range(TT // CONV_ROWS):
            acc = jnp.zeros((CONV_ROWS, D_CONV), F32)
            for k in range(CONV_WIDTH):
                qq, r = divmod(CONV_HALO - CONV_BUF + k, SUBLANES)
                start = qq * SUBLANES + c * CONV_ROWS
                if r == 0:
                    tap = uhist[q, start:start + CONV_ROWS, :]
                else:
                    tap = ushift[q, r - 1, start:start + CONV_ROWS, :]
                acc = acc + w["cw"][k:k + 1, :] * tap
            s = _conv_act(acc, w["cb"], w["lng"], w["lnb"])
            vbuf[q, c * CONV_ROWS:(c + 1) * CONV_ROWS, :] = s.astype(BF16)

        pos = t * TT + jax.lax.broadcasted_iota(I32, (TT, 1), 0)
        pooled = []
        for g, win in enumerate(POOL_WINDOWS):
            lo, hi = g * POOL_GROUP_IN, (g + 1) * POOL_GROUP_IN
            cur = phist[q, POOL_HALO:POOL_HALO + TT, lo:hi]
            ssum = cur
            for i in range(1, win):
                ssum = ssum + phist[q, POOL_HALO - i:POOL_HALO - i + TT, lo:hi]
            cnt = jnp.minimum(pos + 1, win).astype(F32)
            pooled.append(ssum / cnt - cur)
        return pooled

    cur_carry = carry[0:1, :]
    for q in range(NSEQ):
        pooled = conv_and_pool(q)
        gc, gp = gates[q]
        x1, h2, tw, pk8, runs, cur_carry = _merge_and_route(
            x_ref[q], vbuf[q], pooled, gc, gp, mod_ref[2, q], mod_ref[3, q], mod_ref[4, q], cur_carry, w)
        x1_ref[q] = x1
        _store_token_tiles(h2_ref.at[q], h2)
        tw_ref[q] = tw
        pk_ref[q] = pk8
        runs_ref[q] = runs

        uhist[q, 0:CONV_HALO, :] = uhist[q, TT:TT + CONV_HALO, :]
        phist[q, 0:POOL_HALO, :] = phist[q, TT:TT + POOL_HALO, :]

    carry[...] = jnp.broadcast_to(cur_carry, (SUBLANES, LANES))
    cnt_ref[...] = jnp.broadcast_to(cur_carry, (SUBLANES, LANES))

    @pl.when(t == nt - 1)
    def _():
        for q in range(NSEQ):
            nconv_ref[q] = uhist[q, CONV_HALO + TT - CONV_BUF:CONV_HALO + TT, :]
            npool_ref[q] = phist[q, POOL_HALO + TT - POOL_BUF:POOL_HALO + TT, :]


def _prompt_mixer(l, x, mod, weights, batch, seq):
    nt = seq // TT
    n_tok = batch * seq
    half = n_tok // NSEQ
    row = lambda rows, width: pl.BlockSpec((NSEQ, rows, width), lambda b, t: (0, b * nt + t, 0))
    in_specs = [row(TT, D_MODEL),
                pl.BlockSpec((N_MOD, NSEQ, None, 1, D_MODEL), lambda b, t: (0, 0, b, 0, 0))]
    in_specs += _weight_specs(l, 2)
    out_shape = (
        jax.ShapeDtypeStruct((NSEQ, half, D_MODEL), F32),
        jax.ShapeDtypeStruct((NSEQ, half * ROW_TILE, LANES), F32),
        jax.ShapeDtypeStruct((NSEQ, half, LANES), F32),
        jax.ShapeDtypeStruct((NSEQ, SUBLANES, half), I32),
        jax.ShapeDtypeStruct((NSEQ, half // TT * SUBLANES, LANES), I32),
        jax.ShapeDtypeStruct((SUBLANES, LANES), F32),
        jax.ShapeDtypeStruct((NSEQ, batch // NSEQ, CONV_BUF, D_CONV), F32),
        jax.ShapeDtypeStruct((NSEQ, batch // NSEQ, POOL_BUF, D_POOL), F32),
    )
    out_specs = (
        row(TT, D_MODEL), row(TT * ROW_TILE, LANES), row(TT, LANES),
        pl.BlockSpec((NSEQ, SUBLANES, TT), lambda b, t: (0, 0, b * nt + t)),
        row(SUBLANES, LANES),
        pl.BlockSpec((SUBLANES, LANES), lambda b, t: (0, 0)),
        pl.BlockSpec((NSEQ, None, CONV_BUF, D_CONV), lambda b, t: (0, b, 0, 0)),
        pl.BlockSpec((NSEQ, None, POOL_BUF, D_POOL), lambda b, t: (0, b, 0, 0)),
    )
    x1, h2, tw, pk8, runs, cnt, nconv, npool = pl.pallas_call(
        _prompt_mixer_kernel,
        grid=(batch // NSEQ, nt),
        in_specs=in_specs,
        out_specs=out_specs,
        out_shape=out_shape,
        scratch_shapes=[
            pltpu.VMEM((NSEQ, CONV_HALO + TT, D_CONV), F32),
            pltpu.VMEM((NSEQ, SUBLANES - 1, CONV_SHIFT_ROWS, D_CONV), F32),
            pltpu.VMEM((NSEQ, POOL_HALO + TT, D_POOL), F32),
            pltpu.VMEM((NSEQ, TT, D_CONV), BF16),
            pltpu.VMEM((SUBLANES, LANES), F32),
        ],
        compiler_params=pltpu.CompilerParams(
            dimension_semantics=("arbitrary", "arbitrary"), vmem_limit_bytes=VMEM_LIMIT),
        name=f"prompt_mixer_l{l}",
    )(x.reshape(NSEQ, half, D_MODEL), mod.reshape(N_MOD, NSEQ, batch // NSEQ, 1, D_MODEL), *weights)
    return (x1.reshape(n_tok, D_MODEL), h2.reshape(n_tok * ROW_TILE, LANES), tw.reshape(n_tok, LANES),
            jnp.transpose(pk8, (1, 0, 2)).reshape(SUBLANES, n_tok),
            runs.reshape(n_tok // TT * SUBLANES, LANES), cnt,
            nconv.reshape(batch, CONV_BUF, D_CONV), npool.reshape(batch, POOL_BUF, D_POOL))


def _sample_mixer_kernel(x_ref, mod_ref, cs_ref, ps_ref, cnt0_ref, *refs, dec_seq, pos0):
    w = dict(zip(_WEIGHT_NAMES, refs[:len(_WEIGHT_NAMES)]))
    (x1_ref, h2_ref, tw_ref, pk_ref, runs_ref, cnt_ref, nconv_ref, npool_ref,
     ufull, pfull, vbuf, carry) = refs[len(_WEIGHT_NAMES):]

    def per_row(v):
        return jnp.concatenate([v] * dec_seq, axis=0)

    @pl.when(pl.program_id(0) == 0)
    def _():
        carry[...] = cnt0_ref[...]

    x = x_ref[...]
    sh1, sc1, gt1 = per_row(mod_ref[0]), per_row(mod_ref[1]), per_row(mod_ref[2])
    sh2, sc2 = per_row(mod_ref[3]), per_row(mod_ref[4])

    u, up, gc, gp = _in_proj(x, sh1, sc1, w["nmix"], w["win"], w["bin"])
    ufull[0:CONV_BUF] = cs_ref[...]
    pfull[0:POOL_BUF] = ps_ref[...]
    for j in range(dec_seq):
        ufull[CONV_BUF + j] = u[j * SB:(j + 1) * SB, :]
        pfull[POOL_BUF + j] = up[j * SB:(j + 1) * SB, :]
    nconv_ref[...] = ufull[dec_seq:dec_seq + CONV_BUF]
    npool_ref[...] = pfull[dec_seq:dec_seq + POOL_BUF]

    for j in range(dec_seq):
        acc = jnp.zeros((SB, D_CONV), F32)
        for k in range(CONV_WIDTH):
            acc = acc + w["cw"][k:k + 1, :] * ufull[j + k]
        s = _conv_act(acc, w["cb"], w["lng"], w["lnb"])
        vbuf[j * SB:(j + 1) * SB, :] = s.astype(BF16)

    pooled = []
    for g, win in enumerate(POOL_WINDOWS):
        lo, hi = g * POOL_GROUP_IN, (g + 1) * POOL_GROUP_IN
        parts = []
        for j in range(dec_seq):
            cur = pfull[POOL_BUF + j, :, lo:hi]
            ssum = cur
            for i in range(1, win):
                ssum = ssum + pfull[POOL_BUF + j - i, :, lo:hi]
            cnt = float(min(pos0 + j + 1, win))
            parts.append(ssum / cnt - cur)
        pooled.append(jnp.concatenate(parts, axis=0))

    x1, h2, tw, pk8, runs, new_carry = _merge_and_route(
        x, vbuf[...], pooled, gc, gp, gt1, sh2, sc2, carry[0:1, :], w)
    x1_ref[...] = x1
    _store_token_tiles(h2_ref, h2)
    tw_ref[...] = tw
    pk_ref[...] = pk8
    runs_ref[...] = runs
    carry[...] = jnp.broadcast_to(new_carry, (SUBLANES, LANES))
    cnt_ref[...] = jnp.broadcast_to(new_carry, (SUBLANES, LANES))


def _sample_mixer(l, x, mod, conv_state, pool_state, cnt0, weights, dec_batch, dec_seq, pos0):
    rows = dec_seq * SB
    n_s = dec_batch * dec_seq
    row = lambda width: pl.BlockSpec((rows, width), lambda i: (i, 0))
    conv_spec = pl.BlockSpec((None, CONV_BUF, SB, D_CONV), lambda i: (i, 0, 0, 0))
    pool_spec = pl.BlockSpec((None, POOL_BUF, SB, D_POOL), lambda i: (i, 0, 0, 0))
    cnt_spec = pl.BlockSpec((SUBLANES, LANES), lambda i: (0, 0))
    in_specs = [row(D_MODEL), pl.BlockSpec((N_MOD, SB, D_MODEL), lambda i: (0, i, 0)),
                conv_spec, pool_spec, cnt_spec]
    in_specs += _weight_specs(l, 1)
    out_shape = (
        jax.ShapeDtypeStruct((n_s, D_MODEL), F32),
        jax.ShapeDtypeStruct((n_s * ROW_TILE, LANES), F32),
        jax.ShapeDtypeStruct((n_s, LANES), F32),
        jax.ShapeDtypeStruct((SUBLANES, n_s), I32),
        jax.ShapeDtypeStruct((n_s // rows * SUBLANES, LANES), I32),
        jax.ShapeDtypeStruct((SUBLANES, LANES), F32),
        jax.ShapeDtypeStruct(conv_state.shape, F32),
        jax.ShapeDtypeStruct(pool_state.shape, F32),
    )
    out_specs = (row(D_MODEL), pl.BlockSpec((rows * ROW_TILE, LANES), lambda i: (i, 0)), row(LANES),
                 pl.BlockSpec((SUBLANES, rows), lambda i: (0, i)),
                 pl.BlockSpec((SUBLANES, LANES), lambda i: (i, 0)), cnt_spec, conv_spec, pool_spec)
    return pl.pallas_call(
        functools.partial(_sample_mixer_kernel, dec_seq=dec_seq, pos0=pos0),
        grid=(dec_batch // SB,),
        in_specs=in_specs,
        out_specs=out_specs,
        out_shape=out_shape,
        scratch_shapes=[
            pltpu.VMEM((CONV_BUF + dec_seq, SB, D_CONV), F32),
            pltpu.VMEM((POOL_BUF + dec_seq, SB, D_POOL), F32),
            pltpu.VMEM((rows, D_CONV), BF16),
            pltpu.VMEM((SUBLANES, LANES), F32),
        ],
        compiler_params=pltpu.CompilerParams(
            dimension_semantics=("arbitrary",), vmem_limit_bytes=VMEM_LIMIT),
        name=f"sample_mixer_l{l}",
    )(x, mod, conv_state, pool_state, cnt0, *weights)


RUN_ROWS = 3


def _table_copies(pk_hbm, runs_hbm, tile, pos, runs, slot, sem):
    copies = [
        pltpu.make_async_copy(pk_hbm.at[TOP_K + k, pl.ds(tile * TT, TT)],
                              pos.at[pl.ds((slot * TOP_K + k) * TT, TT)], sem)
        for k in range(TOP_K)]
    copies += [
        pltpu.make_async_copy(runs_hbm.at[tile * SUBLANES + r],
                              runs.at[pl.ds((slot * RUN_ROWS + r) * LANES, LANES)], sem)
        for r in range(RUN_ROWS)]
    return copies


def _run_copies(pstart_ref, runs, slot, sorted_hbm, run_ref, sem, to_hbm):
    def body(e, carry):
        count = runs[(slot * RUN_ROWS + 1) * LANES + e]

        @pl.when(count > 0)
        def _():
            glob = pl.multiple_of((pstart_ref[e] + runs[slot * RUN_ROWS * LANES + e]) * ROW_TILE, ROW_TILE)
            loc = pl.multiple_of(runs[(slot * RUN_ROWS + 2) * LANES + e] * ROW_TILE, ROW_TILE)
            hbm = sorted_hbm.at[pl.ds(glob, count * ROW_TILE), :]
            vmem = run_ref.at[pl.ds(loc, count * ROW_TILE), :]
            if to_hbm:
                pltpu.make_async_copy(vmem, hbm, sem).start()
            else:
                pltpu.make_async_copy(hbm, vmem, sem).start()
        return carry
    jax.lax.fori_loop(0, N_EXPERTS, body, 0, unroll=2)


def _dispatch_kernel(pstart_ref, pk_hbm, runs_hbm, h2p_ref, h2s_ref, xs_hbm, xrun0, xrun1, pos, runs,
                     isem, rsem, *, n_prompt_tiles):
    c = pl.program_id(0)
    n = pl.num_programs(0)
    xrun = (xrun0, xrun1)

    def tables(tile, slot):
        return _table_copies(pk_hbm, runs_hbm, tile, pos, runs, slot, isem.at[slot])

    def wait_runs(slot):
        pltpu.make_async_copy(xrun[slot], xs_hbm.at[pl.ds(0, TOP_K * TT * ROW_TILE), :], rsem.at[slot]).wait()

    @pl.when(c == 0)
    def _():
        for cp in tables(0, 0):
            cp.start()

    for s in range(2):
        @pl.when(c % 2 == s)
        def _():
            for cp in tables(c, s):
                cp.wait()

            @pl.when(c + 1 < n)
            def _():
                for cp in tables(c + 1, 1 - s):
                    cp.start()

            @pl.when(c >= 2)
            def _():
                wait_runs(s)

            def place_from(src_ref):
                def place(t, carry):
                    for k in range(TOP_K):
                        p = pos[(s * TOP_K + k) * TT + t]
                        _tile_at_row(xrun[s], p)[...] = _token_tile(src_ref, t)[...]
                    return carry
                jax.lax.fori_loop(0, TT, place, 0, unroll=8)

            @pl.when(c < n_prompt_tiles)
            def _():
                place_from(h2p_ref)

            @pl.when(c >= n_prompt_tiles)
            def _():
                place_from(h2s_ref)

            _run_copies(pstart_ref, runs, s, xs_hbm, xrun[s], rsem.at[s], to_hbm=True)

            @pl.when(c == n - 1)
            def _():
                @pl.when(c >= 1)
                def _():
                    wait_runs(1 - s)
                wait_runs(s)


def _dispatch(pstart, pk8, runs, h2p, h2s, n_slots):
    np_tiles = h2p.shape[0] // (TT * ROW_TILE)
    ns_tiles = h2s.shape[0] // (TT * ROW_TILE)
    tiles = pltpu.VMEM((TOP_K * TT * ROW_TILE, LANES), F32)
    grid_spec = pltpu.PrefetchScalarGridSpec(
        num_scalar_prefetch=1,
        grid=(np_tiles + ns_tiles,),
        in_specs=[
            pl.BlockSpec(memory_space=pl.ANY),
            pl.BlockSpec(memory_space=pl.ANY),
            pl.BlockSpec((TT * ROW_TILE, LANES), lambda c, ps: (jnp.minimum(c, np_tiles - 1), 0)),
            pl.BlockSpec((TT * ROW_TILE, LANES), lambda c, ps: (jnp.maximum(c - np_tiles, 0), 0)),
        ],
        out_specs=pl.BlockSpec(memory_space=pl.ANY),
        scratch_shapes=[
            tiles, tiles,
            pltpu.SMEM((2 * TOP_K * TT,), I32),
            pltpu.SMEM((2 * RUN_ROWS * LANES,), I32),
            pltpu.SemaphoreType.DMA((2,)),
            pltpu.SemaphoreType.DMA((2,)),
        ],
    )
    return pl.pallas_call(
        functools.partial(_dispatch_kernel, n_prompt_tiles=np_tiles),
        grid_spec=grid_spec,
        out_shape=jax.ShapeDtypeStruct((n_slots * ROW_TILE, LANES), F32),
        compiler_params=pltpu.CompilerParams(
            dimension_semantics=("arbitrary",), vmem_limit_bytes=VMEM_LIMIT),
        name="dispatch",
    )(pstart, pk8, runs, h2p, h2s)


def _experts_kernel(be_ref, nv_ref, nu_ref, nx_ref, x_ref, wup_hbm, bup_ref, wdn_hbm, bdn_ref, o_ref,
                    wup_f32, wdn_f32, wup_bf, wdn_bf, wsem, *, layer):
    b = pl.program_id(0)

    def weight_copies(e):
        return (pltpu.make_async_copy(wup_hbm.at[layer, e], wup_f32, wsem.at[0]),
                pltpu.make_async_copy(wdn_hbm.at[layer, e], wdn_f32, wsem.at[1]))

    @pl.when(b < nu_ref[0])
    def _():
        e = be_ref[b]
        prev = be_ref[jnp.maximum(b - 1, 0)]

        @pl.when((b == 0) | (e != prev))
        def _():
            @pl.when(b == 0)
            def _():
                for cp in weight_copies(e):
                    cp.start()
            for cp in weight_copies(e):
                cp.wait()
            for r in range(0, D_MODEL, LANES):
                wup_bf[r:r + LANES, :] = wup_f32[r:r + LANES, :].astype(BF16)
                wdn_bf[r:r + LANES, :] = wdn_f32[r:r + LANES, :].astype(BF16)
            nxt = nx_ref[e]

            @pl.when(nxt >= 0)
            def _():
                for cp in weight_copies(nxt):
                    cp.start()

        n_valid = nv_ref[b]

        def ffn(rows):
            rid = jax.lax.broadcasted_iota(I32, (rows, D_MODEL), 0)
            x = jnp.where(rid < n_valid, _load_token_tiles(x_ref, 0, rows), 0.0)
            a = _dot(x.astype(BF16), wup_bf[...]) + bup_ref[...]
            a_glu = jnp.minimum(a[:, :D_EXPERT], SWIGLU_LIMIT)
            a_lin = jnp.clip(a[:, D_EXPERT:], -SWIGLU_LIMIT, SWIGLU_LIMIT)
            o = a_glu * jax.nn.sigmoid(SWIGLU_ALPHA * a_glu) * (a_lin + 1.0)
            _store_token_tiles(o_ref, _dot(o.astype(BF16), wdn_bf[...]) + bdn_ref[...])
            if rows < BM:
                o_ref[rows * ROW_TILE:, :] = jnp.zeros(((BM - rows) * ROW_TILE, LANES), F32)

        for part in range(1, BM_PARTS + 1):
            rows = part * (BM // BM_PARTS)

            @pl.when((n_valid > rows - BM // BM_PARTS) & (n_valid <= rows))
            def _():
                ffn(rows)

    @pl.when(b >= nu_ref[0])
    def _():
        o_ref[...] = jnp.zeros((BM * ROW_TILE, LANES), F32)


def _experts(l, x_sorted, block_expert, n_valid, n_used, next_expert, w_up, b_up, w_down, b_down):
    n_slots = x_sorted.shape[0] // ROW_TILE
    nb = n_slots // BM

    def used_map(b, be, nv, nu, nx):
        return (jnp.minimum(b, nu[0] - 1), 0)

    bmap = lambda b, be, nv, nu, nx: (l, be[b], 0, 0)
    grid_spec = pltpu.PrefetchScalarGridSpec(
        num_scalar_prefetch=4,
        grid=(nb,),
        in_specs=[
            pl.BlockSpec((BM * ROW_TILE, LANES), used_map),
            pl.BlockSpec(memory_space=pl.ANY),
            pl.BlockSpec((None, None, 1, 2 * D_EXPERT), bmap),
            pl.BlockSpec(memory_space=pl.ANY),
            pl.BlockSpec((None, None, 1, D_MODEL), bmap),
        ],
        out_specs=pl.BlockSpec((BM * ROW_TILE, LANES), lambda b, be, nv, nu, nx: (b, 0)),
        scratch_shapes=[
            pltpu.VMEM((D_MODEL, 2 * D_EXPERT), F32),
            pltpu.VMEM((D_EXPERT, D_MODEL), F32),
            pltpu.VMEM((D_MODEL, 2 * D_EXPERT), BF16),
            pltpu.VMEM((D_EXPERT, D_MODEL), BF16),
            pltpu.SemaphoreType.DMA((2,)),
        ],
    )
    return pl.pallas_call(
        functools.partial(_experts_kernel, layer=l),
        grid_spec=grid_spec,
        out_shape=jax.ShapeDtypeStruct((n_slots * ROW_TILE, LANES), F32),
        compiler_params=pltpu.CompilerParams(
            dimension_semantics=("arbitrary",), vmem_limit_bytes=VMEM_LIMIT),
        name=f"experts_l{l}",
    )(block_expert, n_valid, n_used, next_expert, x_sorted, w_up,
      b_up.reshape(DEPTH, N_EXPERTS, 1, 2 * D_EXPERT), w_down,
      b_down.reshape(DEPTH, N_EXPERTS, 1, D_MODEL))


def _tables_kernel(cnt_ref, pstart_ref, be_ref, nv_ref, nu_ref, nx_ref, *, nb):
    def clear(b, c):
        be_ref[b] = N_EXPERTS - 1
        nv_ref[b] = 0
        return c
    jax.lax.fori_loop(0, nb, clear, 0)

    def per_expert(e, carry):
        blk, last = carry
        cnt = cnt_ref[e]
        n = (cnt + (BM - 1)) // BM
        pstart_ref[e] = blk * BM
        nx_ref[e] = -1

        def per_block(j, c):
            be_ref[blk + j] = e
            nv_ref[blk + j] = jnp.minimum(cnt - j * BM, BM)
            return c
        jax.lax.fori_loop(0, n, per_block, 0)

        @pl.when((n > 0) & (last >= 0))
        def _():
            nx_ref[last] = e
        return blk + n, jnp.where(n > 0, e, last)

    blk, _ = jax.lax.fori_loop(0, N_EXPERTS, per_expert, (jnp.int32(0), jnp.int32(-1)))
    nu_ref[0] = blk


def _block_tables(counts, nb):
    smem = pl.BlockSpec(memory_space=pltpu.SMEM)
    return pl.pallas_call(
        functools.partial(_tables_kernel, nb=nb),
        in_specs=[smem],
        out_specs=(smem, smem, smem, smem, smem),
        out_shape=(
            jax.ShapeDtypeStruct((N_EXPERTS,), I32),
            jax.ShapeDtypeStruct((nb,), I32),
            jax.ShapeDtypeStruct((nb,), I32),
            jax.ShapeDtypeStruct((1,), I32),
            jax.ShapeDtypeStruct((N_EXPERTS,), I32),
        ),
        name="block_tables",
    )(counts)


def _combine_kernel(pstart_ref, pk_hbm, runs_hbm, y_hbm, x1_ref, tw_ref, gt_ref, g_ref, o_ref,
                    yrun0, yrun1, ybuf, pos, runs, isem, rsem, *, gate_copies, final):
    i = pl.program_id(0)
    n = pl.num_programs(0)
    yrun = (yrun0, yrun1)

    def tables(tile, slot):
        return _table_copies(pk_hbm, runs_hbm, tile, pos, runs, slot, isem.at[slot])

    @pl.when(i == 0)
    def _():
        for cp in tables(0, 0):
            cp.start()
        for cp in tables(0, 0):
            cp.wait()
        _run_copies(pstart_ref, runs, 0, y_hbm, yrun[0], rsem.at[0], to_hbm=False)

        @pl.when(n > 1)
        def _():
            for cp in tables(1, 1):
                cp.start()

    for s in range(2):
        @pl.when(i % 2 == s)
        def _():
            @pl.when(i + 1 < n)
            def _():
                for cp in tables(i + 1, 1 - s):
                    cp.wait()
                _run_copies(pstart_ref, runs, 1 - s, y_hbm, yrun[1 - s], rsem.at[1 - s], to_hbm=False)

            pltpu.make_async_copy(y_hbm.at[pl.ds(0, TOP_K * TT * ROW_TILE), :], yrun[s], rsem.at[s]).wait()

            def place(t, carry):
                for k in range(TOP_K):
                    p = pos[(s * TOP_K + k) * TT + t]
                    _token_tile(ybuf, k * TT + t)[...] = _tile_at_row(yrun[s], p)[...]
                return carry
            jax.lax.fori_loop(0, TT, place, 0, unroll=8)

            @pl.when(i + 2 < n)
            def _():
                for cp in tables(i + 2, s):
                    cp.start()

            tw = tw_ref[...]
            moe = tw[:, 0:1] * _load_token_tiles(ybuf, 0, TT)
            for k in range(1, TOP_K):
                moe = moe + tw[:, k:k + 1] * _load_token_tiles(ybuf, k * TT, TT)
            gt = gt_ref[...]
            if gate_copies > 1:
                gt = jnp.concatenate([gt] * gate_copies, axis=0)
            x = x1_ref[...] + gt * moe
            o_ref[...] = _rms(x, g_ref[...]) if final else x


def _combine(pstart, pk8, runs, y_sorted, x1, tw, gate, norm_final, *, gate_spec, gate_copies, final, name):
    n = x1.shape[0]
    row = lambda width: pl.BlockSpec((TT, width), lambda i, ps: (i, 0))
    tiles = pltpu.VMEM((TOP_K * TT * ROW_TILE, LANES), F32)
    grid_spec = pltpu.PrefetchScalarGridSpec(
        num_scalar_prefetch=1,
        grid=(n // TT,),
        in_specs=[
            pl.BlockSpec(memory_space=pl.ANY),
            pl.BlockSpec(memory_space=pl.ANY),
            pl.BlockSpec(memory_space=pl.ANY),
            row(D_MODEL), row(LANES), gate_spec,
            pl.BlockSpec((1, D_MODEL), lambda i, ps: (0, 0)),
        ],
        out_specs=row(D_MODEL),
        scratch_shapes=[
            tiles, tiles, tiles,
            pltpu.SMEM((2 * TOP_K * TT,), I32),
            pltpu.SMEM((2 * RUN_ROWS * LANES,), I32),
            pltpu.SemaphoreType.DMA((2,)),
            pltpu.SemaphoreType.DMA((2,)),
        ],
    )
    return pl.pallas_call(
        functools.partial(_combine_kernel, gate_copies=gate_copies, final=final),
        grid_spec=grid_spec,
        out_shape=jax.ShapeDtypeStruct((n, D_MODEL), F32),
        compiler_params=pltpu.CompilerParams(
            dimension_semantics=("arbitrary",), vmem_limit_bytes=VMEM_LIMIT),
        name=name,
    )(pstart, pk8, runs, y_sorted, x1, tw, gate, norm_final.reshape(1, D_MODEL))


def kernel(x_prompt, x_sample, state_conv, state_pool, c_prompt, c_sample, w_ada, b_ada, norm_mix, w_in, b_in, conv_w, conv_b, conv_ln_g, conv_ln_b, w_conv_out, w_pool, pool_scale, w_out, norm_ffn, w_router, b_router, w_up, b_up, w_down, b_down, norm_final):
    batch, seq, _ = x_prompt.shape
    dec_batch, dec_seq, _ = x_sample.shape
    n_p = batch * seq
    n_s = dec_batch * dec_seq
    n_blocks = dec_batch // SB
    assert dec_seq * SB == TT and seq % TT == 0 and dec_batch % SB == 0
    n_assign = (n_p + n_s) * TOP_K
    nb = n_assign // BM + N_EXPERTS

    vec = lambda a: a.reshape(DEPTH, 1, a.shape[-1])
    weights = (
        vec(norm_mix), w_in.astype(BF16), vec(b_in), conv_w, vec(conv_b), vec(conv_ln_g),
        vec(conv_ln_b), w_conv_out.astype(BF16), w_pool.astype(BF16), vec(pool_scale),
        w_out.astype(BF16), vec(norm_ffn),
        jnp.pad(w_router, ((0, 0), (0, 0), (0, LANES - N_EXPERTS))).astype(BF16),
        vec(jnp.pad(b_router, ((0, 0), (0, LANES - N_EXPERTS)))),
        jnp.broadcast_to(jnp.tril(jnp.ones((TT, TT), BF16), -1), (DEPTH, TT, TT)),
        jnp.broadcast_to(jnp.triu(jnp.ones((LANES, LANES), BF16), 1), (DEPTH, LANES, LANES)),
    )

    mod = _adaln(jnp.concatenate([c_prompt, c_sample], axis=0), w_ada, b_ada)
    mod_p = mod[:, :, :batch]
    mod_s = mod[:, :, batch:]

    def to_sample_order(a):
        lead = a.shape[:-3]
        a = a.reshape(lead + (n_blocks, SB) + a.shape[-2:])
        return jnp.swapaxes(a, -3, -2)

    def from_sample_order(a):
        a = jnp.swapaxes(a, -3, -2)
        return a.reshape(a.shape[:-4] + (dec_batch,) + a.shape[-2:])

    xp = x_prompt.reshape(n_p, D_MODEL)
    xs = to_sample_order(x_sample).reshape(n_s, D_MODEL)
    conv_state = to_sample_order(state_conv)
    pool_state = to_sample_order(state_pool)

    gate_p = pl.BlockSpec((None, 1, D_MODEL), lambda i, ps: (i // (seq // TT), 0, 0))
    gate_s = pl.BlockSpec((SB, D_MODEL), lambda i, ps: (i, 0))

    new_conv_p, new_pool_p, new_conv_s, new_pool_s = [], [], [], []
    for l in range(DEPTH):
        x1p, h2p, twp, pkp, runsp, cntp, ncp, npp = _prompt_mixer(l, xp, mod_p[l], weights, batch, seq)
        x1s, h2s, tws, pks, runss, cnt, ncs, nps = _sample_mixer(
            l, xs, mod_s[l], conv_state[l], pool_state[l], cntp, weights, dec_batch, dec_seq, PAST_LEN)
        new_conv_p.append(ncp)
        new_pool_p.append(npp)
        new_conv_s.append(ncs)
        new_pool_s.append(nps)

        counts = cnt[0, :N_EXPERTS].astype(I32)
        pstart, block_expert, n_valid, n_used, next_expert = _block_tables(counts, nb)
        x_sorted = _dispatch(pstart, jnp.concatenate([pkp, pks], axis=1),
                             jnp.concatenate([runsp, runss], axis=0), h2p, h2s, nb * BM)
        y_sorted = _experts(l, x_sorted, block_expert, n_valid, n_used, next_expert, w_up, b_up, w_down,
                            b_down)

        final = l == DEPTH - 1
        xp = _combine(pstart, pkp, runsp, y_sorted, x1p, twp, mod_p[l][5][:, None, :], norm_final,
                      gate_spec=gate_p, gate_copies=1, final=final, name=f"combine_prompt_l{l}")
        xs = _combine(pstart, pks, runss, y_sorted, x1s, tws, mod_s[l][5], norm_final,
                      gate_spec=gate_s, gate_copies=dec_seq, final=final, name=f"combine_sample_l{l}")

    y_prompt = xp.reshape(batch, seq, D_MODEL)
    y_sample = from_sample_order(xs.reshape(n_blocks, dec_seq, SB, D_MODEL))
    return (y_prompt, y_sample, jnp.stack(new_conv_p), jnp.stack(new_pool_p),
            from_sample_order(jnp.stack(new_conv_s)), from_sample_order(jnp.stack(new_pool_s)))
```
